```python
import jax, jax.numpy as jnp
from jax import lax
import numpy as np

D_MODEL = 2048
BATCH = 4
SEQ = 2048
DEPTH = 2
DEC_BATCH = 8
DEC_SEQ = 8
PAST_LEN = 16384
PAGE_SIZE = 128

HEAD_DIM = 128
N_HEADS = D_MODEL // HEAD_DIM
H_MOBA = (3 * N_HEADS) // 8
H_SB = (N_HEADS - H_MOBA) // 2
H_FOX = N_HEADS - H_MOBA - H_SB
D_ATTN = N_HEADS * HEAD_DIM
N_BRANCH = 3
PROJ_COLS = 3 * D_ATTN + H_FOX + N_BRANCH * D_MODEL
MOBA_BLOCK = 256
MOBA_TOPK = 3
Q_BLOCK = 128
N_EXPERTS = 32
TOP_K = 4
D_EXPERT = D_MODEL
SWIGLU_ALPHA = 1.702
SWIGLU_LIMIT = 7.0
EXPERT_BLOCK = 128
RMS_EPS = 1e-5
NEG_INF = -1e30

kernel_name = "hybrid_moba_stickbreak_fox_moe_decode_step"


def _rmsnorm(x, g):
    x32 = x.astype(jnp.float32)
    y = x32 * lax.rsqrt(jnp.mean(x32 * x32, axis=-1, keepdims=True) + RMS_EPS)
    return (y * g.astype(jnp.float32)).astype(x.dtype)


def _alibi_slopes(n):
    return jnp.exp2(-8.0 * jnp.arange(1, n + 1, dtype=jnp.float32) / n)


def _sweep_query_blocks(block_fn, tq):
    qb = Q_BLOCK if tq % Q_BLOCK == 0 else tq
    nb = tq // qb
    out = lax.map(lambda i: block_fn(i * qb, qb), jnp.arange(nb, dtype=jnp.int32))
    out = jnp.moveaxis(out, 0, 1)
    return out.reshape(out.shape[0], tq, out.shape[3], out.shape[4])


def moba_attention(q, k, v, q_offset):
    B, tq, H, Dh = q.shape
    tk = k.shape[1]
    n_kb = -(-tk // MOBA_BLOCK)
    pad = n_kb * MOBA_BLOCK - tk
    kb = jnp.pad(k, ((0, 0), (0, pad), (0, 0), (0, 0))).reshape(B, n_kb, MOBA_BLOCK, H, Dh).transpose(0, 3, 1, 2, 4)
    vb = jnp.pad(v, ((0, 0), (0, pad), (0, 0), (0, 0))).reshape(B, n_kb, MOBA_BLOCK, H, Dh).transpose(0, 3, 1, 2, 4)
    k_mean = jnp.mean(kb.astype(jnp.float32), axis=3)
    n_sel = min(MOBA_TOPK, n_kb)
    slopes = _alibi_slopes(H)[None, :, None, None]
    scale = Dh ** -0.5
    blk_ids = jnp.arange(n_kb)
    offs = jnp.arange(MOBA_BLOCK)
    b_ix = jnp.arange(B)[:, None, None, None]
    h_ix = jnp.arange(H)[None, :, None, None]

    def block(start, qb):
        qs = lax.dynamic_slice_in_dim(q, start, qb, axis=1).transpose(0, 2, 1, 3)
        q_pos = q_offset + start + jnp.arange(qb)
        own = (q_offset + start) // MOBA_BLOCK
        past_ok = blk_ids < own
        gate = jnp.einsum('bhqd,bhnd->bhqn', qs.astype(jnp.float32), k_mean)
        gate = jnp.where(past_ok, gate, NEG_INF)
        _, sel = lax.top_k(gate, n_sel)
        sel_ok = past_ok[sel]
        k_sel = kb[b_ix, h_ix, sel]
        v_sel = vb[b_ix, h_ix, sel]
        pos_sel = sel[..., None] * MOBA_BLOCK + offs
        s_sel = (jnp.einsum('bhqd,bhqnkd->bhqnk', qs, k_sel).astype(jnp.float32) * scale
                 - slopes[..., None] * (q_pos[:, None, None] - pos_sel))
        s_sel = jnp.where(sel_ok[..., None], s_sel, NEG_INF).reshape(B, H, qb, n_sel * MOBA_BLOCK)
        k_own = lax.dynamic_index_in_dim(kb, own, axis=2, keepdims=False)
        v_own = lax.dynamic_index_in_dim(vb, own, axis=2, keepdims=False)
        pos_own = own * MOBA_BLOCK + offs
        s_own = (jnp.einsum('bhqd,bhkd->bhqk', qs, k_own).astype(jnp.float32) * scale
                 - slopes * (q_pos[:, None] - pos_own[None, :]))
        s_own = jnp.where(pos_own[None, :] <= q_pos[:, None], s_own, NEG_INF)
        p = jax.nn.softmax(jnp.concatenate([s_sel, s_own], axis=-1), axis=-1)
        p_sel = p[..., :n_sel * MOBA_BLOCK].reshape(B, H, qb, n_sel, MOBA_BLOCK)
        p_own = p[..., n_sel * MOBA_BLOCK:]
        o = (jnp.einsum('bhqnk,bhqnkd->bqhd', p_sel.astype(v.dtype), v_sel)
             + jnp.einsum('bhqk,bhkd->bqhd', p_own.astype(v.dtype), v_own))
        return o.astype(q.dtype)

    return _sweep_query_blocks(block, tq)


def stick_breaking_attention(q, k, v, q_offset):
    tk = k.shape[1]
    scale = q.shape[-1] ** -0.5
    k_pos = jnp.arange(tk)

    def block(start, qb):
        qs = lax.dynamic_slice_in_dim(q, start, qb, axis=1)
        z = jnp.einsum('bqhd,bkhd->bhqk', qs, k).astype(jnp.float32) * scale
        q_pos = q_offset + start + jnp.arange(qb)
        strict = k_pos[None, :] < q_pos[:, None]
        log_keep = jnp.where(strict, jax.nn.log_sigmoid(-z), 0.0)
        between = lax.cumsum(log_keep, axis=3, reverse=True) - log_keep
        w = jnp.where(strict, jnp.exp(jax.nn.log_sigmoid(z) + between), 0.0)
        return jnp.einsum('bhqk,bkhd->bqhd', w.astype(v.dtype), v).astype(q.dtype)

    return _sweep_query_blocks(block, q.shape[1])


def forgetting_attention(q, k, v, log_f, q_offset):
    tk = k.shape[1]
    scale = q.shape[-1] ** -0.5
    k_pos = jnp.arange(tk)
    f_cum = jnp.cumsum(log_f.astype(jnp.float32), axis=1).transpose(0, 2, 1)

    def block(start, qb):
        qs = lax.dynamic_slice_in_dim(q, start, qb, axis=1)
        f_q = lax.dynamic_slice_in_dim(f_cum, q_offset + start, qb, axis=2)
        z = (jnp.einsum('bqhd,bkhd->bhqk', qs, k).astype(jnp.float32) * scale
             + f_q[..., None] - f_cum[:, :, None, :])
        q_pos = q_offset + start + jnp.arange(qb)
        z = jnp.where(k_pos[None, :] <= q_pos[:, None], z, NEG_INF)
        p = jax.nn.softmax(z, axis=-1)
        return jnp.einsum('bhqk,bkhd->bqhd', p.astype(v.dtype), v).astype(q.dtype)

    return _sweep_query_blocks(block, q.shape[1])


def moe_ffn(h, w_router, b_router, w_gu, b_gu, w_down, b_down):
    B, T, D = h.shape
    x = h.reshape(B * T, D)
    n_tok = x.shape[0]
    logits = (x @ w_router + b_router).astype(jnp.float32)
    top_val, top_idx = lax.top_k(logits, TOP_K)
    gates = jax.nn.softmax(top_val, axis=-1)
    n_assign = n_tok * TOP_K
    blk = min(EXPERT_BLOCK, n_tok)
    n_blocks = -(-(n_assign + N_EXPERTS * (blk - 1)) // blk)
    flat_e = top_idx.reshape(-1).astype(jnp.int32)
    sizes = jnp.zeros((N_EXPERTS,), jnp.int32).at[flat_e].add(1)
    padded = (sizes + blk - 1) // blk * blk
    pad_end = jnp.cumsum(padded)
    pad_start = pad_end - padded
    grp_start = jnp.cumsum(sizes) - sizes
    order = jnp.argsort(flat_e)
    sorted_e = flat_e[order]
    dest_sorted = pad_start[sorted_e] + jnp.arange(n_assign, dtype=jnp.int32) - grp_start[sorted_e]
    dest = jnp.zeros((n_assign,), jnp.int32).at[order].set(dest_sorted)
    slot_tok = jnp.zeros((n_blocks * blk,), jnp.int32).at[dest].set(jnp.arange(n_assign, dtype=jnp.int32) // TOP_K)
    block_e = jnp.minimum(jnp.searchsorted(pad_end, jnp.arange(n_blocks, dtype=jnp.int32) * blk, side='right'),
                          N_EXPERTS - 1).astype(jnp.int32)

    def expert_block(args):
        tok, e = args
        xb = x[tok]
        gu = xb @ w_gu[e] + b_gu[e]
        g, u = jnp.split(gu, 2, axis=-1)
        g = jnp.minimum(g, SWIGLU_LIMIT)
        u = jnp.clip(u, -SWIGLU_LIMIT, SWIGLU_LIMIT)
        act = (u + 1) * g * jax.nn.sigmoid(SWIGLU_ALPHA * g)
        return act @ w_down[e] + b_down[e]

    slot_out = lax.map(expert_block, (slot_tok.reshape(n_blocks, blk), block_e))
    y_assign = slot_out.reshape(n_blocks * blk, D)[dest].reshape(n_tok, TOP_K, D)
    y = jnp.einsum('tk,tkd->td', gates.astype(y_assign.dtype), y_assign)
    return y.reshape(B, T, D).astype(h.dtype)


def _layer(x, c, past_k, past_v, past_log_f, w_mod, b_mod, g_attn, w_in, b_forget, w_br_moba, w_br_sb, w_br_fox,
           w_out, g_ffn, w_router, b_router, w_gu, b_gu, w_down, b_down):
    B, T, _ = x.shape
    mod = jnp.einsum('bd,de->be', jax.nn.silu(c), w_mod) + b_mod
    shift1, scale1, gate1, shift2, scale2, gate2 = jnp.split(mod[:, None, :], 6, axis=-1)
    h = _rmsnorm(x, g_attn) * (1 + scale1) + shift1
    proj = h @ w_in
    q, k, v, f_logit, br_logit = jnp.split(proj, [D_ATTN, 2 * D_ATTN, 3 * D_ATTN, 3 * D_ATTN + H_FOX], axis=-1)
    q = q.reshape(B, T, N_HEADS, HEAD_DIM)
    k = k.reshape(B, T, N_HEADS, HEAD_DIM)
    v = v.reshape(B, T, N_HEADS, HEAD_DIM)
    log_f = jax.nn.log_sigmoid((f_logit + b_forget).astype(jnp.float32))
    if past_k is None:
        k_all, v_all, lf_all = k, v, log_f
    else:
        k_all = jnp.concatenate([past_k.astype(k.dtype), k], axis=1)
        v_all = jnp.concatenate([past_v.astype(v.dtype), v], axis=1)
        lf_all = jnp.concatenate([past_log_f.astype(jnp.float32), log_f], axis=1)
    q_offset = k_all.shape[1] - T
    a0, a1 = H_MOBA, H_MOBA + H_SB
    o_moba = moba_attention(q[:, :, :a0], k_all[:, :, :a0], v_all[:, :, :a0], q_offset)
    o_sb = stick_breaking_attention(q[:, :, a0:a1], k_all[:, :, a0:a1], v_all[:, :, a0:a1], q_offset)
    o_fox = forgetting_attention(q[:, :, a1:], k_all[:, :, a1:], v_all[:, :, a1:], lf_all, q_offset)
    gates = jax.nn.sigmoid(br_logit.astype(jnp.float32)).astype(x.dtype).reshape(B, T, N_BRANCH, D_MODEL)
    merged = (gates[:, :, 0] * (o_moba.reshape(B, T, H_MOBA * HEAD_DIM) @ w_br_moba)
              + gates[:, :, 1] * (o_sb.reshape(B, T, H_SB * HEAD_DIM) @ w_br_sb)
              + gates[:, :, 2] * (o_fox.reshape(B, T, H_FOX * HEAD_DIM) @ w_br_fox))
    x = x + gate1 * (merged @ w_out)
    h2 = _rmsnorm(x, g_ffn) * (1 + scale2) + shift2
    x = x + gate2 * moe_ffn(h2, w_router, b_router, w_gu, b_gu, w_down, b_down)
    return x, k, v, log_f


def _gather_pages(cache_l, page_table):
    pages = cache_l[page_table]
    return pages.reshape(page_table.shape[0], page_table.shape[1] * cache_l.shape[1], *cache_l.shape[2:])


def setup_inputs(seed: int = 0) -> dict:
    key = jax.random.key(seed)
    ks = jax.random.split(key, 32)
    n_pages = PAST_LEN // PAGE_SIZE
    n_used = DEC_BATCH * n_pages
    n_pool = n_used + (n_used + 3) // 4

    def nrm(k, shape, scale):
        return jax.random.normal(k, shape, jnp.float32) * scale

    d = D_MODEL
    return {
        "x_prompt": nrm(ks[0], (BATCH, SEQ, d), 1.0),
        "x_sample": nrm(ks[1], (DEC_BATCH, DEC_SEQ, d), 1.0),
        "c_prompt": nrm(ks[2], (BATCH, d), 1.0),
        "c_sample": nrm(ks[3], (DEC_BATCH, d), 1.0),
        "cache_k": nrm(ks[4], (DEPTH, n_pool, PAGE_SIZE, N_HEADS, HEAD_DIM), 1.0),
        "cache_v": nrm(ks[5], (DEPTH, n_pool, PAGE_SIZE, N_HEADS, HEAD_DIM), 1.0),
        "cache_log_f": jax.nn.log_sigmoid(2.0 + nrm(ks[6], (DEPTH, n_pool, PAGE_SIZE, H_FOX), 0.5)),
        "page_table": jax.random.permutation(ks[7], n_pool)[:n_used].reshape(DEC_BATCH, n_pages).astype(jnp.int32),
        "w_mod": nrm(ks[8], (DEPTH, d, 6 * d), 0.5 * d ** -0.5),
        "b_mod": nrm(ks[9], (DEPTH, 6 * d), 0.02),
        "g_attn": 1.0 + nrm(ks[10], (DEPTH, d), 0.05),
        "w_in": nrm(ks[11], (DEPTH, d, PROJ_COLS), d ** -0.5),
        "b_forget": 2.0 + nrm(ks[12], (DEPTH, H_FOX), 0.1),
        "w_br_moba": nrm(ks[13], (DEPTH, H_MOBA * HEAD_DIM, d), (H_MOBA * HEAD_DIM) ** -0.5),
        "w_br_sb": nrm(ks[14], (DEPTH, H_SB * HEAD_DIM, d), (H_SB * HEAD_DIM) ** -0.5),
        "w_br_fox": nrm(ks[15], (DEPTH, H_FOX * HEAD_DIM, d), (H_FOX * HEAD_DIM) ** -0.5),
        "w_out": nrm(ks[16], (DEPTH, d, d), d ** -0.5),
        "g_ffn": 1.0 + nrm(ks[17], (DEPTH, d), 0.05),
        "w_router": nrm(ks[18], (DEPTH, d, N_EXPERTS), d ** -0.5),
        "b_router": nrm(ks[19], (DEPTH, N_EXPERTS), 0.01),
        "w_gu": nrm(ks[20], (DEPTH, N_EXPERTS, d, 2 * D_EXPERT), d ** -0.5),
        "b_gu": nrm(ks[21], (DEPTH, N_EXPERTS, 2 * D_EXPERT), 0.01),
        "w_down": nrm(ks[22], (DEPTH, N_EXPERTS, D_EXPERT, d), D_EXPERT ** -0.5),
        "b_down": nrm(ks[23], (DEPTH, N_EXPERTS, d), 0.01),
        "g_final": 1.0 + nrm(ks[24], (d,), 0.05),
    }


def reference(x_prompt, x_sample, c_prompt, c_sample, cache_k, cache_v, cache_log_f, page_table, w_mod, b_mod,
              g_attn, w_in, b_forget, w_br_moba, w_br_sb, w_br_fox, w_out, g_ffn, w_router, b_router, w_gu, b_gu,
              w_down, b_down, g_final):
    xp, xs = x_prompt, x_sample
    kp, vp, fp, ksm, vsm, fsm = [], [], [], [], [], []
    for l in range(DEPTH):
        lw = (w_mod[l], b_mod[l], g_attn[l], w_in[l], b_forget[l], w_br_moba[l], w_br_sb[l], w_br_fox[l],
              w_out[l], g_ffn[l], w_router[l], b_router[l], w_gu[l], b_gu[l], w_down[l], b_down[l])
        xp, k_new, v_new, f_new = _layer(xp, c_prompt, None, None, None, *lw)
        kp.append(k_new); vp.append(v_new); fp.append(f_new)
        past_k = _gather_pages(cache_k[l], page_table)
        past_v = _gather_pages(cache_v[l], page_table)
        past_f = _gather_pages(cache_log_f[l], page_table)
        xs, k_new, v_new, f_new = _layer(xs, c_sample, past_k, past_v, past_f, *lw)
        ksm.append(k_new); vsm.append(v_new); fsm.append(f_new)
    y_prompt = _rmsnorm(xp, g_final)
    y_sample = _rmsnorm(xs, g_final)
    return (y_prompt, y_sample, jnp.stack(kp), jnp.stack(vp), jnp.stack(fp), jnp.stack(ksm), jnp.stack(vsm), jnp.stack(fsm))
```

```python
import functools

import jax
import jax.numpy as jnp
from jax import lax
from jax.experimental import pallas as pl
from jax.experimental.pallas import tpu as pltpu

F32 = jnp.float32
BF16 = jnp.bfloat16
I32 = jnp.int32

HEAD_DIM = 128
N_HEADS = 16
H_MOBA = 6
H_SB = 5
H_FOX = 5
D_ATTN = N_HEADS * HEAD_DIM
MOBA_BLOCK = 256
MOBA_TOPK = 3
N_EXPERTS = 32
TOP_K = 4
SWIGLU_ALPHA = 1.702
SWIGLU_LIMIT = 7.0
RMS_EPS = 1e-5
NEG_INF = -1e30
PAGE = 128
LANES = 128
VMEM_LIMIT_BYTES = 56 * 1024 * 1024
EXPERT_TILE = 256


def _cparams(*sem):
    return pltpu.CompilerParams(dimension_semantics=sem, vmem_limit_bytes=VMEM_LIMIT_BYTES)


def _dot(a, b):
    return jnp.dot(a, b, preferred_element_type=F32)


def _dot_nt(a, b):
    return lax.dot_general(a, b, (((1,), (1,)), ((), ())), preferred_element_type=F32)


def _split2(x):
    hi = x.astype(BF16)
    lo = (x - hi.astype(F32)).astype(BF16)
    return hi, lo


def _split3(x):
    hi = x.astype(BF16)
    r = x - hi.astype(F32)
    mid = r.astype(BF16)
    lo = (r - mid.astype(F32)).astype(BF16)
    return hi, mid, lo


def _dot_acc(a, b):
    a_hi, a_lo = _split2(a)
    b_hi, b_lo = _split2(b)
    return _dot(a_hi, b_hi) + _dot(a_lo, b_hi) + _dot(a_hi, b_lo)


def _dot_nt_acc(a, b):
    a_hi, a_lo = _split2(a)
    b_hi, b_lo = _split2(b)
    return _dot_nt(a_hi, b_hi) + _dot_nt(a_lo, b_hi) + _dot_nt(a_hi, b_lo)


def _dot_exact_rhs(a, b_exact):
    hi, mid, lo = _split3(a)
    return _dot(hi, b_exact) + _dot(mid, b_exact) + _dot(lo, b_exact)


def _dot_exact_lhs(a_exact, b):
    hi, mid, lo = _split3(b)
    return _dot(a_exact, hi) + _dot(a_exact, mid) + _dot(a_exact, lo)


def _softplus_neg_abs(z):
    return jnp.log1p(jnp.exp(-jnp.abs(z)))


def _log_sigmoid(z):
    return -(jnp.maximum(-z, 0.0) + _softplus_neg_abs(z))


def _rms_modulate(x, g, scale, shift):
    y = x * lax.rsqrt(jnp.mean(x * x, axis=-1, keepdims=True) + RMS_EPS)
    return (y * g) * (1.0 + scale) + shift


def _mod_kernel(c_ref, w_ref, b_ref, o_ref):
    c = c_ref[...]
    a = (c * jax.nn.sigmoid(c)).astype(BF16)
    o_ref[...] = _dot(a, w_ref[...].astype(BF16)) + b_ref[...]


def _mod(c, w, b, layer):
    m, d = c.shape
    depth, _, n = w.shape
    tn = 1024
    return pl.pallas_call(
        _mod_kernel,
        grid=(n // tn,),
        in_specs=[
            pl.BlockSpec((m, d), lambda j: (0, 0)),
            pl.BlockSpec((None, d, tn), lambda j: (layer, 0, j)),
            pl.BlockSpec((None, 1, tn), lambda j: (layer, 0, j)),
        ],
        out_specs=pl.BlockSpec((m, tn), lambda j: (0, j)),
        out_shape=jax.ShapeDtypeStruct((m, n), F32),
        compiler_params=_cparams("parallel"),
        name="mod",
    )(c, w, b.reshape(depth, 1, n))


def _ln_proj_kernel(x_ref, sh_ref, sc_ref, g_ref, w_ref, wf_ref, bf_ref, o_ref, lf_ref, h_scr, *, n_plain):
    j = pl.program_id(2)

    @pl.when(j == 0)
    def _():
        h = _rms_modulate(x_ref[...], g_ref[...], sc_ref[...], sh_ref[...])
        h_scr[...] = h.astype(BF16)
        lf_ref[...] = _log_sigmoid(_dot_acc(h, wf_ref[...]) + bf_ref[...])

    acc = _dot(h_scr[...], w_ref[...])

    @pl.when(j < n_plain)
    def _():
        o_ref[...] = acc

    @pl.when(j >= n_plain)
    def _():
        o_ref[...] = jax.nn.sigmoid(acc)


def _ln_proj(x, mod, g, w_main, w_f, b_f, *, tr):
    G, R, D = x.shape
    rs = mod.shape[1]
    tn = 1024
    per = D // tn
    n_col = w_main.shape[1] // tn
    rblk = 1 if rs == 1 else tr
    ridx = (lambda i: 0) if rs == 1 else (lambda i: i)
    kern = functools.partial(_ln_proj_kernel, n_plain=3 * per)
    return pl.pallas_call(
        kern,
        grid=(G, R // tr, n_col),
        in_specs=[
            pl.BlockSpec((None, tr, D), lambda g_, i, j: (g_, i, 0)),
            pl.BlockSpec((None, rblk, D), lambda g_, i, j: (g_, ridx(i), 0)),
            pl.BlockSpec((None, rblk, D), lambda g_, i, j: (g_, ridx(i), 1)),
            pl.BlockSpec((1, D), lambda g_, i, j: (0, 0)),
            pl.BlockSpec((D, tn), lambda g_, i, j: (0, j)),
            pl.BlockSpec((D, LANES), lambda g_, i, j: (0, 0)),
            pl.BlockSpec((1, LANES), lambda g_, i, j: (0, 0)),
        ],
        out_specs=[
            pl.BlockSpec((None, None, tr, tn), lambda g_, i, j: (j // per, g_, i, j % per)),
            pl.BlockSpec((None, tr, LANES), lambda g_, i, j: (g_, i, 0)),
        ],
        out_shape=[
            jax.ShapeDtypeStruct((n_col // per, G, R, D), F32),
            jax.ShapeDtypeStruct((G, R, LANES), F32),
        ],
        scratch_shapes=[pltpu.VMEM((tr, D), BF16)],
        compiler_params=_cparams("parallel", "parallel", "arbitrary"),
        name="ln_proj",
    )(x, mod, mod, g, w_main, w_f, b_f)


def _fcum_kernel(lf_ref, fc_ref, fr_ref):
    t = lf_ref.shape[0]
    r = lax.broadcasted_iota(I32, (t, t), 0)
    c = lax.broadcasted_iota(I32, (t, t), 1)
    tri = (r >= c).astype(BF16)
    f = _dot_exact_lhs(tri, lf_ref[...])
    fc_ref[...] = f
    fr_ref[...] = f.T[:8, :]


def _fcum(logf):
    B, T, _ = logf.shape
    return pl.pallas_call(
        _fcum_kernel,
        grid=(B,),
        in_specs=[pl.BlockSpec((None, T, LANES), lambda b: (b, 0, 0))],
        out_specs=[
            pl.BlockSpec((None, T, LANES), lambda b: (b, 0, 0)),
            pl.BlockSpec((None, 8, T), lambda b: (b, 0, 0)),
        ],
        out_shape=[jax.ShapeDtypeStruct((B, T, LANES), F32), jax.ShapeDtypeStruct((B, 8, T), F32)],
        compiler_params=_cparams("parallel"),
        name="fcum",
    )(logf)


def _head_specs(group, head0, n_heads, rows, row_idx):
    return [
        pl.BlockSpec((None, None, rows, HEAD_DIM), lambda b, i, h=h: (group, b, row_idx(i), head0 + h))
        for h in range(n_heads)
    ]


def _moba_kernel(*refs, n_heads, tq, n_kb):
    q_refs, k_refs, v_refs = refs[:n_heads], refs[n_heads:2 * n_heads], refs[2 * n_heads:3 * n_heads]
    o_ref = refs[3 * n_heads]
    i = pl.program_id(1)
    scale = HEAD_DIM ** -0.5
    n_sel = min(MOBA_TOPK, n_kb)
    lane = lax.broadcasted_iota(I32, (tq, LANES), 1)
    qpos = i * tq + lax.broadcasted_iota(I32, (tq, MOBA_BLOCK), 0)
    koff = lax.broadcasted_iota(I32, (tq, MOBA_BLOCK), 1)
    for h in range(n_heads):
        slope = 2.0 ** (-8.0 * (h + 1) / n_heads)
        q = q_refs[h][...]
        qb = q.astype(BF16)
        k_ref, v_ref = k_refs[h], v_refs[h]
        kmean = jnp.sum(k_ref[...].reshape(n_kb, MOBA_BLOCK, HEAD_DIM), axis=1) * (1.0 / MOBA_BLOCK)
        kmean = jnp.concatenate([kmean, jnp.zeros((LANES - n_kb, HEAD_DIM), F32)], axis=0)
        gate = jnp.where(lane < i, _dot_nt_acc(q, kmean), NEG_INF)
        sel = jnp.zeros((tq, LANES), jnp.bool_)
        for _ in range(n_sel):
            m = jnp.max(gate, axis=-1, keepdims=True)
            idx = jnp.min(jnp.where(gate == m, lane, LANES), axis=-1, keepdims=True)
            hit = lane == idx
            sel = jnp.logical_or(sel, hit)
            gate = jnp.where(hit, -jnp.inf, gate)
        selmask = jnp.where(jnp.logical_and(sel, lane < i), 1.0, 0.0)

        def body(n, carry, qb=qb, k_ref=k_ref, v_ref=v_ref, selmask=selmask, slope=slope):
            m_run, l_run, acc = carry
            start = pl.multiple_of(n * MOBA_BLOCK, MOBA_BLOCK)
            kn = k_ref[pl.ds(start, MOBA_BLOCK), :].astype(BF16)
            vn = v_ref[pl.ds(start, MOBA_BLOCK), :].astype(BF16)
            kpos = n * MOBA_BLOCK + koff
            s = _dot_nt(qb, kn) * scale - slope * (qpos - kpos).astype(F32)
            sel_n = jnp.sum(jnp.where(lane == n, selmask, 0.0), axis=-1, keepdims=True) > 0.5
            own = (koff * 0 + n) == i
            valid = jnp.logical_or(jnp.logical_and(own, kpos <= qpos),
                                   jnp.logical_and(jnp.logical_not(own), sel_n))
            s = jnp.where(valid, s, NEG_INF)
            m_new = jnp.maximum(m_run, jnp.max(s, axis=-1, keepdims=True))
            alpha = jnp.exp(m_run - m_new)
            p = jnp.where(valid, jnp.exp(s - m_new), 0.0)
            l_new = alpha * l_run + jnp.sum(p, axis=-1, keepdims=True)
            acc_new = alpha * acc + _dot(p.astype(BF16), vn)
            return m_new, l_new, acc_new

        init = (jnp.full((tq, 1), NEG_INF, F32), jnp.zeros((tq, 1), F32), jnp.zeros((tq, HEAD_DIM), F32))
        _, l_fin, acc = lax.fori_loop(0, i + 1, body, init)
        o_ref[:, h * HEAD_DIM:(h + 1) * HEAD_DIM] = acc / l_fin


def _moba(proj):
    _, B, T, _ = proj.shape
    tq = MOBA_BLOCK
    n_kb = T // MOBA_BLOCK
    kern = functools.partial(_moba_kernel, n_heads=H_MOBA, tq=tq, n_kb=n_kb)
    specs = (_head_specs(0, 0, H_MOBA, tq, lambda i: i) + _head_specs(1, 0, H_MOBA, T, lambda i: 0)
             + _head_specs(2, 0, H_MOBA, T, lambda i: 0))
    return pl.pallas_call(
        kern,
        grid=(B, T // tq),
        in_specs=specs,
        out_specs=pl.BlockSpec((None, tq, H_MOBA * HEAD_DIM), lambda b, i: (b, i, 0)),
        out_shape=jax.ShapeDtypeStruct((B, T, H_MOBA * HEAD_DIM), F32),
        compiler_params=_cparams("parallel", "arbitrary"),
        name="moba",
    )(*([proj] * (3 * H_MOBA)))


def _strict_upper(n):
    j = lax.broadcasted_iota(I32, (n, n), 0)
    s = lax.broadcasted_iota(I32, (n, n), 1)
    return (j > s).astype(BF16)


def _sb_kernel(*refs, n_heads, tq, tk):
    q_refs, k_refs, v_refs = refs[:n_heads], refs[n_heads:2 * n_heads], refs[2 * n_heads:3 * n_heads]
    o_ref = refs[3 * n_heads]
    i = pl.program_id(1)
    scale = HEAD_DIM ** -0.5
    qpos = i * tq + lax.broadcasted_iota(I32, (tq, tk), 0)
    koff = lax.broadcasted_iota(I32, (tq, tk), 1)
    after = _strict_upper(tk)
    for h in range(n_heads):
        qb = q_refs[h][...].astype(BF16)
        k_ref, v_ref = k_refs[h], v_refs[h]

        def body(t, carry, qb=qb, k_ref=k_ref, v_ref=v_ref):
            c_run, acc = carry
            n = i - t
            start = pl.multiple_of(n * tk, tk)
            kn = k_ref[pl.ds(start, tk), :].astype(BF16)
            vn = v_ref[pl.ds(start, tk), :].astype(BF16)
            z = _dot_nt(qb, kn) * scale
            strict = (n * tk + koff) < qpos
            sp = _softplus_neg_abs(z)
            log_beta = -(jnp.maximum(-z, 0.0) + sp)
            log_keep = jnp.where(strict, -(jnp.maximum(z, 0.0) + sp), 0.0)
            between = _dot_exact_rhs(log_keep, after) + c_run
            w = jnp.where(strict, jnp.exp(log_beta + between), 0.0)
            acc = acc + _dot(w.astype(BF16), vn)
            c_run = c_run + jnp.sum(log_keep, axis=-1, keepdims=True)
            return c_run, acc

        init = (jnp.zeros((tq, 1), F32), jnp.zeros((tq, HEAD_DIM), F32))
        _, acc = lax.fori_loop(0, i + 1, body, init)
        o_ref[:, h * HEAD_DIM:(h + 1) * HEAD_DIM] = acc


def _sb(proj):
    _, B, T, _ = proj.shape
    tq = tk = 256
    kern = functools.partial(_sb_kernel, n_heads=H_SB, tq=tq, tk=tk)
    specs = (_head_specs(0, H_MOBA, H_SB, tq, lambda i: i) + _head_specs(1, H_MOBA, H_SB, T, lambda i: 0)
             + _head_specs(2, H_MOBA, H_SB, T, lambda i: 0))
    return pl.pallas_call(
        kern,
        grid=(B, T // tq),
        in_specs=specs,
        out_specs=pl.BlockSpec((None, tq, H_SB * HEAD_DIM), lambda b, i: (b, i, 0)),
        out_shape=jax.ShapeDtypeStruct((B, T, H_SB * HEAD_DIM), F32),
        compiler_params=_cparams("parallel", "arbitrary"),
        name="sb",
    )(*([proj] * (3 * H_SB)))


def _fox_kernel(*refs, n_heads, tq, tk):
    q_refs, k_refs, v_refs = refs[:n_heads], refs[n_heads:2 * n_heads], refs[2 * n_heads:3 * n_heads]
    fc_ref, fr_ref, o_ref = refs[3 * n_heads:3 * n_heads + 3]
    i = pl.program_id(1)
    scale = HEAD_DIM ** -0.5
    qpos = i * tq + lax.broadcasted_iota(I32, (tq, tk), 0)
    koff = lax.broadcasted_iota(I32, (tq, tk), 1)
    for h in range(n_heads):
        qb = q_refs[h][...].astype(BF16)
        k_ref, v_ref = k_refs[h], v_refs[h]
        fq = fc_ref[:, h:h + 1]

        def body(n, carry, qb=qb, k_ref=k_ref, v_ref=v_ref, fq=fq, h=h):
            m_run, l_run, acc = carry
            start = pl.multiple_of(n * tk, tk)
            kn = k_ref[pl.ds(start, tk), :].astype(BF16)
            vn = v_ref[pl.ds(start, tk), :].astype(BF16)
            fk = fr_ref[h:h + 1, pl.ds(start, tk)]
            s = _dot_nt(qb, kn) * scale + (fq - fk)
            valid = (n * tk + koff) <= qpos
            s = jnp.where(valid, s, NEG_INF)
            m_new = jnp.maximum(m_run, jnp.max(s, axis=-1, keepdims=True))
            alpha = jnp.exp(m_run - m_new)
            p = jnp.where(valid, jnp.exp(s - m_new), 0.0)
            l_new = alpha * l_run + jnp.sum(p, axis=-1, keepdims=True)
            acc_new = alpha * acc + _dot(p.astype(BF16), vn)
            return m_new, l_new, acc_new

        init = (jnp.full((tq, 1), NEG_INF, F32), jnp.zeros((tq, 1), F32), jnp.zeros((tq, HEAD_DIM), F32))
        _, l_fin, acc = lax.fori_loop(0, i + 1, body, init)
        o_ref[:, h * HEAD_DIM:(h + 1) * HEAD_DIM] = acc / l_fin


def _fox(proj, fc, fr):
    _, B, T, _ = proj.shape
    tq = tk = 256
    h0 = H_MOBA + H_SB
    kern = functools.partial(_fox_kernel, n_heads=H_FOX, tq=tq, tk=tk)
    specs = (_head_specs(0, h0, H_FOX, tq, lambda i: i) + _head_specs(1, h0, H_FOX, T, lambda i: 0)
             + _head_specs(2, h0, H_FOX, T, lambda i: 0)
             + [pl.BlockSpec((None, tq, LANES), lambda b, i: (b, i, 0)),
                pl.BlockSpec((None, 8, T), lambda b, i: (b, 0, 0))])
    return pl.pallas_call(
        kern,
        grid=(B, T // tq),
        in_specs=specs,
        out_specs=pl.BlockSpec((None, tq, H_FOX * HEAD_DIM), lambda b, i: (b, i, 0)),
        out_shape=jax.ShapeDtypeStruct((B, T, H_FOX * HEAD_DIM), F32),
        compiler_params=_cparams("parallel", "arbitrary"),
        name="fox",
    )(*([proj] * (3 * H_FOX)), fc, fr)


def _decode_kernel(pt_ref, q_ref, kn_ref, vn_ref, lfn_ref, lfnt_ref, kc_ref, vc_ref, lfc_ref, o_ref,
                   mb_m, mb_l, mb_g, mb_o, own_m, own_l, own_o, sb_acc, sb_c, fx_m, fx_l, fx_acc, fx_c, a_scr,
                   *, n_pages, tq):
    j = pl.program_id(1)
    page = n_pages - 1 - j
    t_past = n_pages * PAGE
    scale = HEAD_DIM ** -0.5
    lane = lax.broadcasted_iota(I32, (tq, LANES), 1)
    row = lax.broadcasted_iota(I32, (tq, LANES), 0)
    after = _strict_upper(LANES)
    q_all = q_ref[...]

    def hs(h):
        return slice(h * HEAD_DIM, (h + 1) * HEAD_DIM)

    def process(k2d, v2d, lft, is_new):
        kb = k2d.astype(BF16)
        vb = v2d.astype(BF16)
        if is_new:
            causal = lane <= row
            strict = lane < row
            dist = (row - lane).astype(F32)
        else:
            dist = (t_past + row - (page * PAGE + lane)).astype(F32)
        for h in range(H_MOBA):
            slope = 2.0 ** (-8.0 * (h + 1) / H_MOBA)
            qh = q_all[:, hs(h)]
            s = _dot_nt(qh.astype(BF16), kb[:, hs(h)]) * scale - slope * dist
            if is_new:
                s = jnp.where(causal, s, NEG_INF)
            m = jnp.max(s, axis=-1, keepdims=True)
            p = jnp.exp(s - m)
            if is_new:
                p = jnp.where(causal, p, 0.0)
            l = jnp.sum(p, axis=-1, keepdims=True)
            o = _dot(p.astype(BF16), vb[:, hs(h)])
            if is_new:
                own_m[h] = m
                own_l[h] = l
                own_o[h] = o
            else:
                ksum = jnp.sum(k2d[:, hs(h)], axis=0, keepdims=True)
                g = jnp.sum(qh * ksum, axis=-1, keepdims=True)
                here = lane == page
                mb_m[h] = jnp.where(here, m, mb_m[h])
                mb_l[h] = jnp.where(here, l, mb_l[h])
                mb_g[h] = jnp.where(here, g, mb_g[h])
                for r in range(tq):
                    mb_o[h, r, pl.ds(page, 1), :] = o[r:r + 1, :]
        for hh in range(H_SB):
            h = H_MOBA + hh
            z = _dot_nt(q_all[:, hs(h)].astype(BF16), kb[:, hs(h)]) * scale
            sp = _softplus_neg_abs(z)
            log_beta = -(jnp.maximum(-z, 0.0) + sp)
            log_keep = -(jnp.maximum(z, 0.0) + sp)
            if is_new:
                log_keep = jnp.where(strict, log_keep, 0.0)
            between = _dot_exact_rhs(log_keep, after) + sb_c[hh]
            w = jnp.exp(log_beta + between)
            if is_new:
                w = jnp.where(strict, w, 0.0)
            sb_acc[hh] = sb_acc[hh] + _dot(w.astype(BF16), vb[:, hs(h)])
            sb_c[hh] = sb_c[hh] + jnp.sum(log_keep, axis=-1, keepdims=True)
        suffix = _dot_exact_rhs(lft, after)
        a_new = a_scr[...]
        carry = fx_c[...]
        for hh in range(H_FOX):
            h = H_MOBA + H_SB + hh
            bias = suffix[hh:hh + 1, :] + a_new[:, hh:hh + 1] + carry[hh:hh + 1, :]
            s = _dot_nt(q_all[:, hs(h)].astype(BF16), kb[:, hs(h)]) * scale + bias
            if is_new:
                s = jnp.where(causal, s, NEG_INF)
            m_run = fx_m[hh]
            m_new = jnp.maximum(m_run, jnp.max(s, axis=-1, keepdims=True))
            alpha = jnp.exp(m_run - m_new)
            p = jnp.exp(s - m_new)
            if is_new:
                p = jnp.where(causal, p, 0.0)
            fx_l[hh] = alpha * fx_l[hh] + jnp.sum(p, axis=-1, keepdims=True)
            fx_acc[hh] = alpha * fx_acc[hh] + _dot(p.astype(BF16), vb[:, hs(h)])
            fx_m[hh] = m_new
        fx_c[...] = carry + jnp.sum(lft, axis=-1, keepdims=True)

    @pl.when(j == 0)
    def _():
        mb_m[...] = jnp.full(mb_m.shape, NEG_INF, F32)
        mb_l[...] = jnp.zeros(mb_l.shape, F32)
        mb_g[...] = jnp.full(mb_g.shape, NEG_INF, F32)
        mb_o[...] = jnp.zeros(mb_o.shape, F32)
        sb_acc[...] = jnp.zeros(sb_acc.shape, F32)
        sb_c[...] = jnp.zeros(sb_c.shape, F32)
        fx_m[...] = jnp.full(fx_m.shape, NEG_INF, F32)
        fx_l[...] = jnp.zeros(fx_l.shape, F32)
        fx_acc[...] = jnp.zeros(fx_acc.shape, F32)
        lfn = lfn_ref[...]
        a_new = jnp.zeros((tq, LANES), F32)
        for r in range(tq):
            a_new = a_new + jnp.where(row >= r, lfn[r:r + 1, :], 0.0)
        a_scr[...] = a_new
        lfnt = lfnt_ref[...]
        fx_c[...] = -jnp.sum(lfnt, axis=-1, keepdims=True)
        process(kn_ref[...], vn_ref[...], lfnt, True)

    process(kc_ref[...], vc_ref[...], lfc_ref[...], False)

    @pl.when(j == n_pages - 1)
    def _():
        pj = lax.broadcasted_iota(I32, (LANES, LANES), 0)
        ps = lax.broadcasted_iota(I32, (LANES, LANES), 1)
        pair = ((pj >> 1) == (ps >> 1)).astype(BF16)
        n_kb = n_pages * PAGE // MOBA_BLOCK + 1
        for h in range(H_MOBA):
            gate = jnp.where(lane < n_pages, _dot_exact_rhs(mb_g[h], pair), NEG_INF)
            sel = jnp.zeros((tq, LANES), jnp.bool_)
            for _ in range(min(MOBA_TOPK, n_kb)):
                m = jnp.max(gate, axis=-1, keepdims=True)
                idx = jnp.min(jnp.where(gate == m, lane, LANES), axis=-1, keepdims=True)
                hit = (lane >> 1) == (idx >> 1)
                sel = jnp.logical_or(sel, hit)
                gate = jnp.where(hit, -jnp.inf, gate)
            sel = jnp.logical_and(sel, lane < n_pages)
            m_pg = mb_m[h]
            m_fin = jnp.maximum(jnp.max(jnp.where(sel, m_pg, NEG_INF), axis=-1, keepdims=True), own_m[h])
            wp = jnp.where(sel, jnp.exp(m_pg - m_fin), 0.0)
            w_own = jnp.exp(own_m[h] - m_fin)
            l_fin = jnp.sum(wp * mb_l[h], axis=-1, keepdims=True) + w_own * own_l[h]
            rows = [_dot_acc(wp[r:r + 1, :n_pages], mb_o[h, r]) for r in range(tq)]
            o_fin = jnp.concatenate(rows, axis=0) + w_own * own_o[h]
            o_ref[:, hs(h)] = o_fin / l_fin
        for hh in range(H_SB):
            o_ref[:, hs(H_MOBA + hh)] = sb_acc[hh]
        for hh in range(H_FOX):
            o_ref[:, hs(H_MOBA + H_SB + hh)] = fx_acc[hh] / fx_l[hh]


def _decode_attn(proj_s, logf_s, cache_k, cache_v, cache_lft, page_table, layer):
    nb, n_pages = page_table.shape
    d = proj_s.shape[-1]
    tq = proj_s.shape[2] // nb
    assert tq == 8 and n_pages % 2 == 0 and n_pages <= LANES
    knew = jnp.pad(proj_s[1, 0].reshape(nb, tq, d), ((0, 0), (0, PAGE - tq), (0, 0)))
    vnew = jnp.pad(proj_s[2, 0].reshape(nb, tq, d), ((0, 0), (0, PAGE - tq), (0, 0)))
    lfnt = jnp.pad(jnp.swapaxes(logf_s[0, :, :8].reshape(nb, tq, 8), 1, 2), ((0, 0), (0, 0), (0, LANES - tq)))
    lfnt = jnp.where(lax.broadcasted_iota(I32, lfnt.shape, 1) < H_FOX, lfnt, 0.0)
    kern = functools.partial(_decode_kernel, n_pages=n_pages, tq=tq)
    cache_idx = lambda b, j, pt: (layer, pt[b, n_pages - 1 - j], 0, 0)
    vm = pltpu.VMEM
    return pl.pallas_call(
        kern,
        grid_spec=pltpu.PrefetchScalarGridSpec(
            num_scalar_prefetch=1,
            grid=(nb, n_pages),
            in_specs=[
                pl.BlockSpec((None, None, tq, d), lambda b, j, pt: (0, 0, b, 0)),
                pl.BlockSpec((None, PAGE, d), lambda b, j, pt: (b, 0, 0)),
                pl.BlockSpec((None, PAGE, d), lambda b, j, pt: (b, 0, 0)),
                pl.BlockSpec((None, tq, LANES), lambda b, j, pt: (0, b, 0)),
                pl.BlockSpec((None, 8, LANES), lambda b, j, pt: (b, 0, 0)),
                pl.BlockSpec((None, None, PAGE, d), cache_idx),
                pl.BlockSpec((None, None, PAGE, d), cache_idx),
                pl.BlockSpec((None, None, 8, LANES), cache_idx),
            ],
            out_specs=pl.BlockSpec((None, tq, d), lambda b, j, pt: (b, 0, 0)),
            scratch_shapes=[
                vm((H_MOBA, tq, LANES), F32), vm((H_MOBA, tq, LANES), F32), vm((H_MOBA, tq, LANES), F32),
                vm((H_MOBA, tq, n_pages, HEAD_DIM), F32),
                vm((H_MOBA, tq, 1), F32), vm((H_MOBA, tq, 1), F32), vm((H_MOBA, tq, HEAD_DIM), F32),
                vm((H_SB, tq, HEAD_DIM), F32), vm((H_SB, tq, 1), F32),
                vm((H_FOX, tq, 1), F32), vm((H_FOX, tq, 1), F32), vm((H_FOX, tq, HEAD_DIM), F32),
                vm((8, 1), F32), vm((tq, LANES), F32),
            ],
        ),
        out_shape=jax.ShapeDtypeStruct((nb, tq, d), F32),
        compiler_params=_cparams("parallel", "arbitrary"),
        name="decode_attn",
    )(page_table, proj_s, knew, vnew, logf_s, lfnt, cache_k, cache_v, cache_lft)


def _merge_kernel(om_ref, os_ref, of_ref, gm_ref, gs_ref, gf_ref, x_ref, g1_ref,
                  wm_ref, ws_ref, wf_ref, wo_ref, o_ref, mrg_scr):
    j = pl.program_id(2)

    @pl.when(j == 0)
    def _():
        merged = (gm_ref[...] * _dot(om_ref[...].astype(BF16), wm_ref[...])
                  + gs_ref[...] * _dot(os_ref[...].astype(BF16), ws_ref[...])
                  + gf_ref[...] * _dot(of_ref[...].astype(BF16), wf_ref[...]))
        mrg_scr[...] = merged.astype(BF16)

    o_ref[...] = x_ref[...] + g1_ref[...] * _dot(mrg_scr[...], wo_ref[...])


def _merge(o_moba, o_sb, o_fox, proj, x, mod, w_m, w_s, w_f, w_o, *, tr):
    G, R, D = x.shape
    rs = mod.shape[1]
    tn = 1024
    per = D // tn
    rblk = 1 if rs == 1 else tr
    ridx = (lambda i: 0) if rs == 1 else (lambda i: i)
    full = lambda a: pl.BlockSpec(a.shape, lambda g_, i, j: (0, 0))
    return pl.pallas_call(
        _merge_kernel,
        grid=(G, R // tr, per),
        in_specs=[
            pl.BlockSpec((None, tr, o_moba.shape[-1]), lambda g_, i, j: (g_, i, 0)),
            pl.BlockSpec((None, tr, o_sb.shape[-1]), lambda g_, i, j: (g_, i, 0)),
            pl.BlockSpec((None, tr, o_fox.shape[-1]), lambda g_, i, j: (g_, i, 0)),
            pl.BlockSpec((None, None, tr, D), lambda g_, i, j: (3, g_, i, 0)),
            pl.BlockSpec((None, None, tr, D), lambda g_, i, j: (4, g_, i, 0)),
            pl.BlockSpec((None, None, tr, D), lambda g_, i, j: (5, g_, i, 0)),
            pl.BlockSpec((None, tr, tn), lambda g_, i, j: (g_, i, j)),
            pl.BlockSpec((None, rblk, tn), lambda g_, i, j: (g_, ridx(i), 2 * per + j)),
            full(w_m), full(w_s), full(w_f),
            pl.BlockSpec((D, tn), lambda g_, i, j: (0, j)),
        ],
        out_specs=pl.BlockSpec((None, tr, tn), lambda g_, i, j: (g_, i, j)),
        out_shape=jax.ShapeDtypeStruct((G, R, D), F32),
        scratch_shapes=[pltpu.VMEM((tr, D), BF16)],
        compiler_params=_cparams("parallel", "parallel", "arbitrary"),
        name="merge",
    )(o_moba, o_sb, o_fox, proj, proj, proj, x, mod, w_m, w_s, w_f, w_o)


def _norm_router_kernel(x_ref, sh_ref, sc_ref, g_ref, wr_ref, br_ref, h_ref, idx_ref, gate_ref):
    h = _rms_modulate(x_ref[...], g_ref[...], sc_ref[...], sh_ref[...])
    h_ref[...] = h
    tr = h.shape[0]
    lane = lax.broadcasted_iota(I32, (tr, LANES), 1)
    logits = _dot_acc(h, wr_ref[...]) + br_ref[...]
    vals, idxs = [], []
    for _ in range(TOP_K):
        m = jnp.max(logits, axis=-1, keepdims=True)
        idx = jnp.min(jnp.where(logits == m, lane, LANES), axis=-1, keepdims=True)
        vals.append(m)
        idxs.append(idx)
        logits = jnp.where(lane == idx, -jnp.inf, logits)
    es = [jnp.exp(v - vals[0]) for v in vals]
    denom = es[0] + es[1] + es[2] + es[3]
    idx_out = jnp.zeros((tr, LANES), I32)
    gate_out = jnp.zeros((tr, LANES), F32)
    for k in range(TOP_K):
        idx_out = jnp.where(lane == k, idxs[k], idx_out)
        gate_out = jnp.where(lane == k, es[k] / denom, gate_out)
    idx_ref[...] = idx_out
    gate_ref[...] = gate_out


def _norm_router(x, mod, g, w_r, b_r, *, tr):
    G, R, D = x.shape
    rs = mod.shape[1]
    rblk = 1 if rs == 1 else tr
    ridx = (lambda i: 0) if rs == 1 else (lambda i: i)
    return pl.pallas_call(
        _norm_router_kernel,
        grid=(G, R // tr),
        in_specs=[
            pl.BlockSpec((None, tr, D), lambda g_, i: (g_, i, 0)),
            pl.BlockSpec((None, rblk, D), lambda g_, i: (g_, ridx(i), 3)),
            pl.BlockSpec((None, rblk, D), lambda g_, i: (g_, ridx(i), 4)),
            pl.BlockSpec((1, D), lambda g_, i: (0, 0)),
            pl.BlockSpec((D, LANES), lambda g_, i: (0, 0)),
            pl.BlockSpec((1, LANES), lambda g_, i: (0, 0)),
        ],
        out_specs=[
            pl.BlockSpec((None, tr, D), lambda g_, i: (g_, i, 0)),
            pl.BlockSpec((None, tr, LANES), lambda g_, i: (g_, i, 0)),
            pl.BlockSpec((None, tr, LANES), lambda g_, i: (g_, i, 0)),
        ],
        out_shape=[
            jax.ShapeDtypeStruct((G, R, D), F32),
            jax.ShapeDtypeStruct((G, R, LANES), I32),
            jax.ShapeDtypeStruct((G, R, LANES), F32),
        ],
        compiler_params=_cparams("parallel", "parallel"),
        name="norm_router",
    )(x, mod, mod, g, w_r, b_r)


def _route(top_idx, tm):
    n_tok = top_idx.shape[0]
    n_assign = n_tok * TOP_K
    n_tiles = -(-(n_assign + N_EXPERTS * (tm - 1)) // tm)
    flat_e = top_idx.reshape(-1)
    onehot = (flat_e[:, None] == jnp.arange(N_EXPERTS, dtype=I32)[None, :]).astype(I32)
    sizes = jnp.sum(onehot, axis=0)
    rank = jnp.sum(jnp.cumsum(onehot, axis=0) * onehot, axis=1) - 1
    padded = (sizes + tm - 1) // tm * tm
    pad_end = jnp.cumsum(padded)
    pad_start = pad_end - padded
    dest = (pad_start[flat_e] + rank).astype(I32)
    slot_tok = jnp.zeros((n_tiles * tm,), I32).at[dest].set(jnp.arange(n_assign, dtype=I32) // TOP_K)
    tile_e = jnp.minimum(jnp.searchsorted(pad_end, jnp.arange(n_tiles, dtype=I32) * tm, side="right"),
                         N_EXPERTS - 1).astype(I32)
    n_used = (pad_end[-1] // tm).astype(I32).reshape(1)
    return dest, slot_tok, tile_e, n_used


def _gather_kernel(slot_tok_ref, n_used_ref, h_ref, o_ref, zbuf, sem, zsem, *, tm):
    t = pl.program_id(0)

    def row_copy(r):
        tok = slot_tok_ref[t * tm + r]
        return pltpu.make_async_copy(h_ref.at[pl.ds(tok, 1)], o_ref.at[pl.ds(t * tm + r, 1)], sem)

    @pl.when(t < n_used_ref[0])
    def _():
        def start(r, c):
            row_copy(r).start()
            return c

        def wait(r, c):
            row_copy(r).wait()
            return c

        lax.fori_loop(0, tm, start, 0)
        lax.fori_loop(0, tm, wait, 0)

    @pl.when(t >= n_used_ref[0])
    def _():
        zbuf[...] = jnp.zeros(zbuf.shape, zbuf.dtype)
        fill = pltpu.make_async_copy(zbuf, o_ref.at[pl.ds(t * tm, tm)], zsem)
        fill.start()
        fill.wait()


def _gather_rows(h, slot_tok, n_used, tm):
    n_slots = slot_tok.shape[0]
    d = h.shape[1]
    return pl.pallas_call(
        functools.partial(_gather_kernel, tm=tm),
        grid_spec=pltpu.PrefetchScalarGridSpec(
            num_scalar_prefetch=2,
            grid=(n_slots // tm,),
            in_specs=[pl.BlockSpec(memory_space=pl.ANY)],
            out_specs=pl.BlockSpec(memory_space=pl.ANY),
            scratch_shapes=[pltpu.VMEM((tm, d), h.dtype), pltpu.SemaphoreType.DMA, pltpu.SemaphoreType.DMA],
        ),
        out_shape=jax.ShapeDtypeStruct((n_slots, d), h.dtype),
        compiler_params=_cparams("arbitrary"),
        name="gather_rows",
    )(slot_tok, n_used, h)


def _expert_changed(te_ref, i):
    prev = te_ref[jnp.maximum(i - 1, 0)]
    return jnp.logical_or(i == 0, te_ref[i] != prev)


def _expert_gu_kernel(te_ref, nu_ref, x_ref, wg_ref, wu_ref, bg_ref, bu_ref, o_ref, wg_scr, wu_scr):
    i = pl.program_id(1)

    @pl.when(i < nu_ref[0])
    def _():
        @pl.when(_expert_changed(te_ref, i))
        def _():
            wg_scr[...] = wg_ref[...].astype(BF16)
            wu_scr[...] = wu_ref[...].astype(BF16)

        x = x_ref[...].astype(BF16)
        g = _dot(x, wg_scr[...]) + bg_ref[...]
        u = _dot(x, wu_scr[...]) + bu_ref[...]
        g = jnp.minimum(g, SWIGLU_LIMIT)
        u = jnp.clip(u, -SWIGLU_LIMIT, SWIGLU_LIMIT)
        o_ref[...] = ((u + 1.0) * g * jax.nn.sigmoid(SWIGLU_ALPHA * g)).astype(BF16)

    @pl.when(i >= nu_ref[0])
    def _():
        o_ref[...] = jnp.zeros(o_ref.shape, o_ref.dtype)


def _expert_gu(xs, w_gu, b_gu, tile_e, n_used, tm):
    n_slots, d = xs.shape
    de = w_gu.shape[2] // 2
    tn = 1024
    nj = de // tn
    row = lambda j, i, te, nu: jnp.minimum(i, nu[0] - 1)
    exp = lambda j, i, te, nu: te[jnp.minimum(i, nu[0] - 1)]
    b3 = b_gu.reshape(w_gu.shape[0], 1, 2 * de)
    return pl.pallas_call(
        _expert_gu_kernel,
        grid_spec=pltpu.PrefetchScalarGridSpec(
            num_scalar_prefetch=2,
            grid=(nj, n_slots // tm),
            in_specs=[
                pl.BlockSpec((tm, d), lambda j, i, te, nu: (row(j, i, te, nu), 0)),
                pl.BlockSpec((None, d, tn), lambda j, i, te, nu: (exp(j, i, te, nu), 0, j)),
                pl.BlockSpec((None, d, tn), lambda j, i, te, nu: (exp(j, i, te, nu), 0, nj + j)),
                pl.BlockSpec((None, 1, tn), lambda j, i, te, nu: (exp(j, i, te, nu), 0, j)),
                pl.BlockSpec((None, 1, tn), lambda j, i, te, nu: (exp(j, i, te, nu), 0, nj + j)),
            ],
            out_specs=pl.BlockSpec((tm, tn), lambda j, i, te, nu: (i, j)),
            scratch_shapes=[pltpu.VMEM((d, tn), BF16), pltpu.VMEM((d, tn), BF16)],
        ),
        out_shape=jax.ShapeDtypeStruct((n_slots, de), BF16),
        compiler_params=_cparams("arbitrary", "arbitrary"),
        name="expert_gu",
    )(tile_e, n_used, xs, w_gu, w_gu, b3, b3)


def _expert_down_kernel(te_ref, nu_ref, a_ref, w_ref, b_ref, o_ref, w_scr):
    i = pl.program_id(1)

    @pl.when(i < nu_ref[0])
    def _():
        @pl.when(_expert_changed(te_ref, i))
        def _():
            w_scr[...] = w_ref[...].astype(BF16)

        o_ref[...] = _dot(a_ref[...], w_scr[...]) + b_ref[...]

    @pl.when(i >= nu_ref[0])
    def _():
        o_ref[...] = jnp.zeros(o_ref.shape, o_ref.dtype)


def _expert_down(act, w_down, b_down, tile_e, n_used, tm):
    n_slots, de = act.shape
    d = w_down.shape[2]
    tn = 1024
    nj = d // tn
    row = lambda j, i, te, nu: jnp.minimum(i, nu[0] - 1)
    exp = lambda j, i, te, nu: te[jnp.minimum(i, nu[0] - 1)]
    return pl.pallas_call(
        _expert_down_kernel,
        grid_spec=pltpu.PrefetchScalarGridSpec(
            num_scalar_prefetch=2,
            grid=(nj, n_slots // tm),
            in_specs=[
                pl.BlockSpec((tm, de), lambda j, i, te, nu: (row(j, i, te, nu), 0)),
                pl.BlockSpec((None, de, tn), lambda j, i, te, nu: (exp(j, i, te, nu), 0, j)),
                pl.BlockSpec((None, 1, tn), lambda j, i, te, nu: (exp(j, i, te, nu), 0, j)),
            ],
            out_specs=pl.BlockSpec((tm, tn), lambda j, i, te, nu: (i, j)),
            scratch_shapes=[pltpu.VMEM((de, tn), BF16)],
        ),
        out_shape=jax.ShapeDtypeStruct((n_slots, d), F32),
        compiler_params=_cparams("arbitrary", "arbitrary"),
        name="expert_down",
    )(tile_e, n_used, act, w_down, b_down.reshape(w_down.shape[0], 1, d))


def _combine_kernel(dest_ref, y_ref, gate_ref, x_ref, g2_ref, gfin_ref, o_ref, ybuf, sem, *, tt, tok0, final):
    g_ = pl.program_id(0)
    i = pl.program_id(1)
    base = (tok0 + (g_ * pl.num_programs(1) + i) * tt) * TOP_K

    def row_copy(r, k):
        slot = dest_ref[base + r * TOP_K + k]
        return pltpu.make_async_copy(y_ref.at[pl.ds(slot, 1)], ybuf.at[k, pl.ds(r, 1)], sem)

    def start(r, c):
        for k in range(TOP_K):
            row_copy(r, k).start()
        return c

    def wait(r, c):
        for k in range(TOP_K):
            row_copy(r, k).wait()
        return c

    lax.fori_loop(0, tt, start, 0)
    lax.fori_loop(0, tt, wait, 0)
    gates = gate_ref[...]
    moe = gates[:, 0:1] * ybuf[0]
    for k in range(1, TOP_K):
        moe = moe + gates[:, k:k + 1] * ybuf[k]
    x2 = x_ref[...] + g2_ref[...] * moe
    if final:
        x2 = x2 * lax.rsqrt(jnp.mean(x2 * x2, axis=-1, keepdims=True) + RMS_EPS) * gfin_ref[...]
    o_ref[...] = x2


def _combine(y, dest, gates, x1, mod, g_final, *, tt, tok0, final):
    G, R, D = x1.shape
    rs = mod.shape[1]
    rblk = 1 if rs == 1 else tt
    ridx = (lambda i: 0) if rs == 1 else (lambda i: i)
    kern = functools.partial(_combine_kernel, tt=tt, tok0=tok0, final=final)
    return pl.pallas_call(
        kern,
        grid_spec=pltpu.PrefetchScalarGridSpec(
            num_scalar_prefetch=1,
            grid=(G, R // tt),
            in_specs=[
                pl.BlockSpec(memory_space=pl.ANY),
                pl.BlockSpec((None, tt, LANES), lambda g_, i, d_: (g_, i, 0)),
                pl.BlockSpec((None, tt, D), lambda g_, i, d_: (g_, i, 0)),
                pl.BlockSpec((None, rblk, D), lambda g_, i, d_: (g_, ridx(i), 5)),
                pl.BlockSpec((1, D), lambda g_, i, d_: (0, 0)),
            ],
            out_specs=pl.BlockSpec((None, tt, D), lambda g_, i, d_: (g_, i, 0)),
            scratch_shapes=[pltpu.VMEM((TOP_K, tt, D), F32), pltpu.SemaphoreType.DMA],
        ),
        out_shape=jax.ShapeDtypeStruct((G, R, D), F32),
        compiler_params=_cparams("arbitrary", "arbitrary"),
        name="combine",
    )(dest, y, gates, x1, mod, g_final)


def _row_tile(rows, cap):
    return max(t for t in range(MOBA_BLOCK, cap + 1, MOBA_BLOCK) if rows % t == 0)


def _layer_weights(l, w_in, b_forget, w_br_moba, w_br_sb, w_br_fox, w_out, w_router, b_router):
    d = w_in.shape[1]
    w_l = w_in[l]
    w_main = jnp.concatenate([w_l[:, :3 * D_ATTN], w_l[:, 3 * D_ATTN + H_FOX:]], axis=1).astype(BF16)
    w_f = jnp.pad(w_l[:, 3 * D_ATTN:3 * D_ATTN + H_FOX], ((0, 0), (0, LANES - H_FOX)))
    b_f = jnp.pad(b_forget[l], (0, LANES - H_FOX)).reshape(1, LANES)
    w_r = jnp.pad(w_router[l], ((0, 0), (0, LANES - N_EXPERTS)))
    b_r = jnp.pad(b_router[l], (0, LANES - N_EXPERTS), constant_values=NEG_INF).reshape(1, LANES)
    return dict(w_main=w_main, w_f=w_f, b_f=b_f, w_r=w_r, b_r=b_r,
                w_m=w_br_moba[l].astype(BF16), w_s=w_br_sb[l].astype(BF16), w_x=w_br_fox[l].astype(BF16),
                w_o=w_out[l].astype(BF16))


def kernel(x_prompt, x_sample, c_prompt, c_sample, cache_k, cache_v, cache_log_f, page_table, w_mod, b_mod,
           g_attn, w_in, b_forget, w_br_moba, w_br_sb, w_br_fox, w_out, g_ffn, w_router, b_router, w_gu, b_gu,
           w_down, b_down, g_final):
    depth = w_mod.shape[0]
    B, T, D = x_prompt.shape
    nb, tq, _ = x_sample.shape
    n_pool = cache_k.shape[1]
    n_prompt = B * T
    n_samp = nb * tq
    tm = EXPERT_TILE

    ck = cache_k.reshape(depth, n_pool, PAGE, D)
    cv = cache_v.reshape(depth, n_pool, PAGE, D)
    clf = jnp.pad(jnp.swapaxes(cache_log_f, 2, 3), ((0, 0), (0, 0), (0, 8 - H_FOX), (0, 0)))
    c_all = jnp.pad(jnp.concatenate([c_prompt, c_sample], axis=0), ((0, 16 - B - nb), (0, 0)))
    wgu = w_gu.reshape(depth * N_EXPERTS, D, w_gu.shape[-1])
    bgu = b_gu.reshape(depth * N_EXPERTS, b_gu.shape[-1])
    wdn = w_down.reshape(depth * N_EXPERTS, w_down.shape[2], D)
    bdn = b_down.reshape(depth * N_EXPERTS, D)
    g_fin = g_final.reshape(1, D)

    xp = x_prompt
    xs = x_sample.reshape(1, n_samp, D)
    kp, vp, fp, ks, vs, fs = [], [], [], [], [], []
    for l in range(depth):
        lw = _layer_weights(l, w_in, b_forget, w_br_moba, w_br_sb, w_br_fox, w_out, w_router, b_router)
        g_a = g_attn[l].reshape(1, D)
        g_f = g_ffn[l].reshape(1, D)
        mod = _mod(c_all, w_mod, b_mod, l)
        mod_p = mod[:B].reshape(B, 1, 6 * D)
        mod_s = jnp.repeat(mod[B:B + nb], tq, axis=0).reshape(1, n_samp, 6 * D)

        proj_p, lf_p = _ln_proj(xp, mod_p, g_a, lw["w_main"], lw["w_f"], lw["b_f"], tr=_row_tile(T, 1024))
        fc, fr = _fcum(lf_p)
        x1p = _merge(_moba(proj_p), _sb(proj_p), _fox(proj_p, fc, fr), proj_p, xp, mod_p,
                     lw["w_m"], lw["w_s"], lw["w_x"], lw["w_o"], tr=256)
        proj_s, lf_s = _ln_proj(xs, mod_s, g_a, lw["w_main"], lw["w_f"], lw["b_f"], tr=n_samp)
        o_s = _decode_attn(proj_s, lf_s, ck, cv, clf, page_table, l).reshape(1, n_samp, D)
        n_m, n_s = H_MOBA * HEAD_DIM, H_SB * HEAD_DIM
        x1s = _merge(o_s[..., :n_m], o_s[..., n_m:n_m + n_s], o_s[..., n_m + n_s:], proj_s, xs, mod_s,
                     lw["w_m"], lw["w_s"], lw["w_x"], lw["w_o"], tr=n_samp)

        h2p, idx_p, gate_p = _norm_router(x1p, mod_p, g_f, lw["w_r"], lw["b_r"], tr=512)
        h2s, idx_s, gate_s = _norm_router(x1s, mod_s, g_f, lw["w_r"], lw["b_r"], tr=n_samp)
        h2 = jnp.concatenate([h2p.reshape(n_prompt, D), h2s.reshape(n_samp, D)], axis=0)
        top_idx = jnp.concatenate([idx_p.reshape(n_prompt, LANES)[:, :TOP_K],
                                   idx_s.reshape(n_samp, LANES)[:, :TOP_K]], axis=0)
        dest, slot_tok, tile_e, n_used = _route(top_idx, tm)
        rows = _gather_rows(h2, slot_tok, n_used, tm)
        act = _expert_gu(rows, wgu, bgu, tile_e + l * N_EXPERTS, n_used, tm)
        y = _expert_down(act, wdn, bdn, tile_e + l * N_EXPERTS, n_used, tm)
        final = l == depth - 1
        xp = _combine(y, dest, gate_p, x1p, mod_p, g_fin, tt=256, tok0=0, final=final)
        xs = _combine(y, dest, gate_s, x1s, mod_s, g_fin, tt=n_samp, tok0=n_prompt, final=final)

        kp.append(proj_p[1]); vp.append(proj_p[2]); fp.append(lf_p[..., :H_FOX])
        ks.append(proj_s[1]); vs.append(proj_s[2]); fs.append(lf_s[..., :H_FOX])

    hd = (N_HEADS, HEAD_DIM)
    return (xp, xs.reshape(nb, tq, D),
            jnp.stack(kp).reshape(depth, B, T, *hd), jnp.stack(vp).reshape(depth, B, T, *hd), jnp.stack(fp),
            jnp.stack(ks).reshape(depth, nb, tq, *hd), jnp.stack(vs).reshape(depth, nb, tq, *hd),
            jnp.stack(fs).reshape(depth, nb, tq, H_FOX))
```

```python
import functools

import jax
import jax.numpy as jnp
from jax import lax
from jax.experimental import pallas as pl
from jax.experimental.pallas import tpu as pltpu

F32 = jnp.float32
BF16 = jnp.bfloat16
I32 = jnp.int32

HEAD_DIM = 128
N_HEADS = 16
H_MOBA = 6
H_SB = 5
H_FOX = 5
D_ATTN = N_HEADS * HEAD_DIM
MOBA_BLOCK = 256
MOBA_TOPK = 3
N_EXPERTS = 32
TOP_K = 4
SWIGLU_ALPHA = 1.702
SWIGLU_LIMIT = 7.0
RMS_EPS = 1e-5
NEG_INF = -1e30
PAGE = 128
LANES = 128
VMEM_LIMIT_BYTES = 56 * 1024 * 1024
EXPERT_TILE = 256


def _cparams(*sem):
    return pltpu.CompilerParams(dimension_semantics=sem, vmem_limit_bytes=VMEM_LIMIT_BYTES)


def _dot(a, b):
    return jnp.dot(a, b, preferred_element_type=F32)


def _dot_nt(a, b):
    return lax.dot_general(a, b, (((1,), (1,)), ((), ())), preferred_element_type=F32)


def _split2(x):
    hi = x.astype(BF16)
    lo = (x - hi.astype(F32)).astype(BF16)
    return hi, lo


def _split3(x):
    hi = x.astype(BF16)
    r = x - hi.astype(F32)
    mid = r.astype(BF16)
    lo = (r - mid.astype(F32)).astype(BF16)
    return hi, mid, lo


def _dot_acc(a, b):
    a_hi, a_lo = _split2(a)
    b_hi, b_lo = _split2(b)
    return _dot(a_hi, b_hi) + _dot(a_lo, b_hi) + _dot(a_hi, b_lo)


def _dot_nt_acc(a, b):
    a_hi, a_lo = _split2(a)
    b_hi, b_lo = _split2(b)
    return _dot_nt(a_hi, b_hi) + _dot_nt(a_lo, b_hi) + _dot_nt(a_hi, b_lo)


def _dot_exact_rhs(a, b_exact):
    hi, mid, lo = _split3(a)
    return _dot(hi, b_exact) + _dot(mid, b_exact) + _dot(lo, b_exact)


def _dot_exact_lhs(a_exact, b):
    hi, mid, lo = _split3(b)
    return _dot(a_exact, hi) + _dot(a_exact, mid) + _dot(a_exact, lo)


def _softplus_neg_abs(z):
    return jnp.log1p(jnp.exp(-jnp.abs(z)))


def _log_sigmoid(z):
    return -(jnp.maximum(-z, 0.0) + _softplus_neg_abs(z))


def _rms_modulate(x, g, scale, shift):
    y = x * lax.rsqrt(jnp.mean(x * x, axis=-1, keepdims=True) + RMS_EPS)
    return (y * g) * (1.0 + scale) + shift


def _mod_kernel(c_ref, w_ref, b_ref, o_ref):
    c = c_ref[...]
    a = (c * jax.nn.sigmoid(c)).astype(BF16)
    o_ref[...] = _dot(a, w_ref[...].astype(BF16)) + b_ref[...]


def _mod(c, w, b, layer):
    m, d = c.shape
    depth, _, n = w.shape
    tn = 1024
    return pl.pallas_call(
        _mod_kernel,
        grid=(n // tn,),
        in_specs=[
            pl.BlockSpec((m, d), lambda j: (0, 0)),
            pl.BlockSpec((None, d, tn), lambda j: (layer, 0, j)),
            pl.BlockSpec((None, 1, tn), lambda j: (layer, 0, j)),
        ],
        out_specs=pl.BlockSpec((m, tn), lambda j: (0, j)),
        out_shape=jax.ShapeDtypeStruct((m, n), F32),
        compiler_params=_cparams("parallel"),
        name="mod",
    )(c, w, b.reshape(depth, 1, n))


def _ln_proj_kernel(x_ref, sh_ref, sc_ref, g_ref, w_ref, wf_ref, bf_ref, o_ref, lf_ref, h_scr, *, n_plain):
    j = pl.program_id(2)

    @pl.when(j == 0)
    def _():
        h = _rms_modulate(x_ref[...], g_ref[...], sc_ref[...], sh_ref[...])
        h_scr[...] = h.astype(BF16)
        lf_ref[...] = _log_sigmoid(_dot_acc(h, wf_ref[...]) + bf_ref[...])

    acc = _dot(h_scr[...], w_ref[...])

    @pl.when(j < n_plain)
    def _():
        o_ref[...] = acc

    @pl.when(j >= n_plain)
    def _():
        o_ref[...] = jax.nn.sigmoid(acc)


def _ln_proj(x, mod, g, w_main, w_f, b_f, *, tr):
    G, R, D = x.shape
    rs = mod.shape[1]
    tn = 1024
    per = D // tn
    n_col = w_main.shape[1] // tn
    rblk = 1 if rs == 1 else tr
    ridx = (lambda i: 0) if rs == 1 else (lambda i: i)
    kern = functools.partial(_ln_proj_kernel, n_plain=3 * per)
    return pl.pallas_call(
        kern,
        grid=(G, R // tr, n_col),
        in_specs=[
            pl.BlockSpec((None, tr, D), lambda g_, i, j: (g_, i, 0)),
            pl.BlockSpec((None, rblk, D), lambda g_, i, j: (g_, ridx(i), 0)),
            pl.BlockSpec((None, rblk, D), lambda g_, i, j: (g_, ridx(i), 1)),
            pl.BlockSpec((1, D), lambda g_, i, j: (0, 0)),
            pl.BlockSpec((D, tn), lambda g_, i, j: (0, j)),
            pl.BlockSpec((D, LANES), lambda g_, i, j: (0, 0)),
            pl.BlockSpec((1, LANES), lambda g_, i, j: (0, 0)),
        ],
        out_specs=[
            pl.BlockSpec((None, None, tr, tn), lambda g_, i, j: (j // per, g_, i, j % per)),
            pl.BlockSpec((None, tr, LANES), lambda g_, i, j: (g_, i, 0)),
        ],
        out_shape=[
            jax.ShapeDtypeStruct((n_col // per, G, R, D), F32),
            jax.ShapeDtypeStruct((G, R, LANES), F32),
        ],
        scratch_shapes=[pltpu.VMEM((tr, D), BF16)],
        compiler_params=_cparams("parallel", "parallel", "arbitrary"),
        name="ln_proj",
    )(x, mod, mod, g, w_main, w_f, b_f)


def _fcum_kernel(lf_ref, fc_ref, fr_ref):
    t = lf_ref.shape[0]
    r = lax.broadcasted_iota(I32, (t, t), 0)
    c = lax.broadcasted_iota(I32, (t, t), 1)
    tri = (r >= c).astype(BF16)
    f = _dot_exact_lhs(tri, lf_ref[...])
    fc_ref[...] = f
    fr_ref[...] = f.T[:8, :]


def _fcum(logf):
    B, T, _ = logf.shape
    return pl.pallas_call(
        _fcum_kernel,
        grid=(B,),
        in_specs=[pl.BlockSpec((None, T, LANES), lambda b: (b, 0, 0))],
        out_specs=[
            pl.BlockSpec((None, T, LANES), lambda b: (b, 0, 0)),
            pl.BlockSpec((None, 8, T), lambda b: (b, 0, 0)),
        ],
        out_shape=[jax.ShapeDtypeStruct((B, T, LANES), F32), jax.ShapeDtypeStruct((B, 8, T), F32)],
        compiler_params=_cparams("parallel"),
        name="fcum",
    )(logf)


def _head_specs(group, head0, n_heads, rows, row_idx):
    return [
        pl.BlockSpec((None, None, rows, HEAD_DIM), lambda b, i, h=h: (group, b, row_idx(i), head0 + h))
        for h in range(n_heads)
    ]


def _moba_kernel(*refs, n_heads, tq, n_kb):
    q_refs, k_refs, v_refs = refs[:n_heads], refs[n_heads:2 * n_heads], refs[2 * n_heads:3 * n_heads]
    o_ref = refs[3 * n_heads]
    i = pl.program_id(1)
    scale = HEAD_DIM ** -0.5
    n_sel = min(MOBA_TOPK, n_kb)
    lane = lax.broadcasted_iota(I32, (tq, LANES), 1)
    qpos = i * tq + lax.broadcasted_iota(I32, (tq, MOBA_BLOCK), 0)
    koff = lax.broadcasted_iota(I32, (tq, MOBA_BLOCK), 1)
    for h in range(n_heads):
        slope = 2.0 ** (-8.0 * (h + 1) / n_heads)
        q = q_refs[h][...]
        qb = q.astype(BF16)
        k_ref, v_ref = k_refs[h], v_refs[h]
        kmean = jnp.sum(k_ref[...].reshape(n_kb, MOBA_BLOCK, HEAD_DIM), axis=1) * (1.0 / MOBA_BLOCK)
        kmean = jnp.concatenate([kmean, jnp.zeros((LANES - n_kb, HEAD_DIM), F32)], axis=0)
        gate = jnp.where(lane < i, _dot_nt_acc(q, kmean), NEG_INF)
        sel = jnp.zeros((tq, LANES), jnp.bool_)
        for _ in range(n_sel):
            m = jnp.max(gate, axis=-1, keepdims=True)
            idx = jnp.min(jnp.where(gate == m, lane, LANES), axis=-1, keepdims=True)
            hit = lane == idx
            sel = jnp.logical_or(sel, hit)
            gate = jnp.where(hit, -jnp.inf, gate)
        selmask = jnp.where(jnp.logical_and(sel, lane < i), 1.0, 0.0)

        def body(n, carry, qb=qb, k_ref=k_ref, v_ref=v_ref, selmask=selmask, slope=slope):
            m_run, l_run, acc = carry
            start = pl.multiple_of(n * MOBA_BLOCK, MOBA_BLOCK)
            kn = k_ref[pl.ds(start, MOBA_BLOCK), :].astype(BF16)
            vn = v_ref[pl.ds(start, MOBA_BLOCK), :].astype(BF16)
            kpos = n * MOBA_BLOCK + koff
            s = _dot_nt(qb, kn) * scale - slope * (qpos - kpos).astype(F32)
            sel_n = jnp.sum(jnp.where(lane == n, selmask, 0.0), axis=-1, keepdims=True) > 0.5
            own = (koff * 0 + n) == i
            valid = jnp.logical_or(jnp.logical_and(own, kpos <= qpos),
                                   jnp.logical_and(jnp.logical_not(own), sel_n))
            s = jnp.where(valid, s, NEG_INF)
            m_new = jnp.maximum(m_run, jnp.max(s, axis=-1, keepdims=True))
            alpha = jnp.exp(m_run - m_new)
            p = jnp.where(valid, jnp.exp(s - m_new), 0.0)
            l_new = alpha * l_run + jnp.sum(p, axis=-1, keepdims=True)
            acc_new = alpha * acc + _dot(p.astype(BF16), vn)
            return m_new, l_new, acc_new

        init = (jnp.full((tq, 1), NEG_INF, F32), jnp.zeros((tq, 1), F32), jnp.zeros((tq, HEAD_DIM), F32))
        _, l_fin, acc = lax.fori_loop(0, i + 1, body, init)
        o_ref[:, h * HEAD_DIM:(h + 1) * HEAD_DIM] = acc / l_fin


def _moba(proj):
    _, B, T, _ = proj.shape
    tq = MOBA_BLOCK
    n_kb = T // MOBA_BLOCK
    kern = functools.partial(_moba_kernel, n_heads=H_MOBA, tq=tq, n_kb=n_kb)
    specs = (_head_specs(0, 0, H_MOBA, tq, lambda i: i) + _head_specs(1, 0, H_MOBA, T, lambda i: 0)
             + _head_specs(2, 0, H_MOBA, T, lambda i: 0))
    return pl.pallas_call(
        kern,
        grid=(B, T // tq),
        in_specs=specs,
        out_specs=pl.BlockSpec((None, tq, H_MOBA * HEAD_DIM), lambda b, i: (b, i, 0)),
        out_shape=jax.ShapeDtypeStruct((B, T, H_MOBA * HEAD_DIM), F32),
        compiler_params=_cparams("parallel", "arbitrary"),
        name="moba",
    )(*([proj] * (3 * H_MOBA)))


def _strict_upper(n):
    j = lax.broadcasted_iota(I32, (n, n), 0)
    s = lax.broadcasted_iota(I32, (n, n), 1)
    return (j > s).astype(BF16)


def _sb_kernel(*refs, n_heads, tq, tk):
    q_refs, k_refs, v_refs = refs[:n_heads], refs[n_heads:2 * n_heads], refs[2 * n_heads:3 * n_heads]
    o_ref = refs[3 * n_heads]
    i = pl.program_id(1)
    scale = HEAD_DIM ** -0.5
    qpos = i * tq + lax.broadcasted_iota(I32, (tq, tk), 0)
    koff = lax.broadcasted_iota(I32, (tq, tk), 1)
    after = _strict_upper(tk)
    for h in range(n_heads):
        qb = q_refs[h][...].astype(BF16)
        k_ref, v_ref = k_refs[h], v_refs[h]

        def body(t, carry, qb=qb, k_ref=k_ref, v_ref=v_ref):
            c_run, acc = carry
            n = i - t
            start = pl.multiple_of(n * tk, tk)
            kn = k_ref[pl.ds(start, tk), :].astype(BF16)
            vn = v_ref[pl.ds(start, tk), :].astype(BF16)
            z = _dot_nt(qb, kn) * scale
            strict = (n * tk + koff) < qpos
            sp = _softplus_neg_abs(z)
            log_beta = -(jnp.maximum(-z, 0.0) + sp)
            log_keep = jnp.where(strict, -(jnp.maximum(z, 0.0) + sp), 0.0)
            between = _dot_exact_rhs(log_keep, after) + c_run
            w = jnp.where(strict, jnp.exp(log_beta + between), 0.0)
            acc = acc + _dot(w.astype(BF16), vn)
            c_run = c_run + jnp.sum(log_keep, axis=-1, keepdims=True)
            return c_run, acc

        init = (jnp.zeros((tq, 1), F32), jnp.zeros((tq, HEAD_DIM), F32))
        _, acc = lax.fori_loop(0, i + 1, body, init)
        o_ref[:, h * HEAD_DIM:(h + 1) * HEAD_DIM] = acc


def _sb(proj):
    _, B, T, _ = proj.shape
    tq = tk = 256
    kern = functools.partial(_sb_kernel, n_heads=H_SB, tq=tq, tk=tk)
    specs = (_head_specs(0, H_MOBA, H_SB, tq, lambda i: i) + _head_specs(1, H_MOBA, H_SB, T, lambda i: 0)
             + _head_specs(2, H_MOBA, H_SB, T, lambda i: 0))
    return pl.pallas_call(
        kern,
        grid=(B, T // tq),
        in_specs=specs,
        out_specs=pl.BlockSpec((None, tq, H_SB * HEAD_DIM), lambda b, i: (b, i, 0)),
        out_shape=jax.ShapeDtypeStruct((B, T, H_SB * HEAD_DIM), F32),
        compiler_params=_cparams("parallel", "arbitrary"),
        name="sb",
    )(*([proj] * (3 * H_SB)))


def _fox_kernel(*refs, n_heads, tq, tk):
    q_refs, k_refs, v_refs = refs[:n_heads], refs[n_heads:2 * n_heads], refs[2 * n_heads:3 * n_heads]
    fc_ref, fr_ref, o_ref = refs[3 * n_heads:3 * n_heads + 3]
    i = pl.program_id(1)
    scale = HEAD_DIM ** -0.5
    qpos = i * tq + lax.broadcasted_iota(I32, (tq, tk), 0)
    koff = lax.broadcasted_iota(I32, (tq, tk), 1)
    for h in range(n_heads):
        qb = q_refs[h][...].astype(BF16)
        k_ref, v_ref = k_refs[h], v_refs[h]
        fq = fc_ref[:, h:h + 1]

        def body(n, carry, qb=qb, k_ref=k_ref, v_ref=v_ref, fq=fq, h=h):
            m_run, l_run, acc = carry
            start = pl.multiple_of(n * tk, tk)
            kn = k_ref[pl.ds(start, tk), :].astype(BF16)
            vn = v_ref[pl.ds(start, tk), :].astype(BF16)
            fk = fr_ref[h:h + 1, pl.ds(start, tk)]
            s = _dot_nt(qb, kn) * scale + (fq - fk)
            valid = (n * tk + koff) <= qpos
            s = jnp.where(valid, s, NEG_INF)
            m_new = jnp.maximum(m_run, jnp.max(s, axis=-1, keepdims=True))
            alpha = jnp.exp(m_run - m_new)
            p = jnp.where(valid, jnp.exp(s - m_new), 0.0)
            l_new = alpha * l_run + jnp.sum(p, axis=-1, keepdims=True)
            acc_new = alpha * acc + _dot(p.astype(BF16), vn)
            return m_new, l_new, acc_new

        init = (jnp.full((tq, 1), NEG_INF, F32), jnp.zeros((tq, 1), F32), jnp.zeros((tq, HEAD_DIM), F32))
        _, l_fin, acc = lax.fori_loop(0, i + 1, body, init)
        o_ref[:, h * HEAD_DIM:(h + 1) * HEAD_DIM] = acc / l_fin


def _fox(proj, fc, fr):
    _, B, T, _ = proj.shape
    tq = tk = 256
    h0 = H_MOBA + H_SB
    kern = functools.partial(_fox_kernel, n_heads=H_FOX, tq=tq, tk=tk)
    specs = (_head_specs(0, h0, H_FOX, tq, lambda i: i) + _head_specs(1, h0, H_FOX, T, lambda i: 0)
             + _head_specs(2, h0, H_FOX, T, lambda i: 0)
             + [pl.BlockSpec((None, tq, LANES), lambda b, i: (b, i, 0)),
                pl.BlockSpec((None, 8, T), lambda b, i: (b, 0, 0))])
    return pl.pallas_call(
        kern,
        grid=(B, T // tq),
        in_specs=specs,
        out_specs=pl.BlockSpec((None, tq, H_FOX * HEAD_DIM), lambda b, i: (b, i, 0)),
        out_shape=jax.ShapeDtypeStruct((B, T, H_FOX * HEAD_DIM), F32),
        compiler_params=_cparams("parallel", "arbitrary"),
        name="fox",
    )(*([proj] * (3 * H_FOX)), fc, fr)


HPG = 8
TOK_PER_BLK = LANES // HPG


def _lane_class_mats():
    j = lax.broadcasted_iota(I32, (LANES, LANES), 0)
    s = lax.broadcasted_iota(I32, (LANES, LANES), 1)
    same = (j & 7) == (s & 7)
    after = jnp.logical_and(same, (j >> 3) > (s >> 3)).astype(BF16)
    return after, same.astype(BF16)


def _scan_blocked(xb, rows, nblk, carry, after, same):
    within = _dot_exact_rhs(xb, after)
    tot = _dot_exact_rhs(xb, same)
    out = [None] * nblk
    run = carry
    for b in range(nblk - 1, -1, -1):
        out[b] = within[b * rows:(b + 1) * rows] + run
        run = run + tot[b * rows:(b + 1) * rows]
    return out, run


def _decode_kernel(pt_ref, q_ref, kn_ref, vn_ref, lfn_ref, lfnb_ref, kc_ref, vc_ref, lfc_ref, o_ref,
                   mb_m, mb_l, mb_g, mb_o, own_m, own_l, own_o, sb_acc0, sb_acc1, sb_c0, sb_c1,
                   fx_m, fx_l, fx_acc, fx_c, a_col, *, n_pages, tq):
    j = pl.program_id(1)
    page = n_pages - 1 - j
    t_past = n_pages * PAGE
    scale = HEAD_DIM ** -0.5
    n_rows = HPG * tq
    r_mb = H_MOBA * tq
    r_sb1 = (H_MOBA + H_SB - HPG) * tq
    after, same = _lane_class_mats()
    lane1 = lax.broadcasted_iota(I32, (tq, LANES), 1)

    def lane_blocks(x):
        return [x[:, b * LANES:(b + 1) * LANES] for b in range(x.shape[1] // LANES)]

    def process(k_ref, v_ref, lfblk, is_new):
        t_tok = k_ref.shape[0]
        nblk = t_tok // TOK_PER_BLK
        w = t_tok * HPG
        lane = lax.broadcasted_iota(I32, (n_rows, w), 1)
        row = lax.broadcasted_iota(I32, (n_rows, w), 0)
        tpos = lane >> 3
        r = row & 7
        mine = (lane & 7) == (row >> 3)
        if is_new:
            vis = jnp.logical_and(mine, tpos <= r)
            vis_strict = jnp.logical_and(mine, tpos < r)
            dist = (r - tpos).astype(F32)
        else:
            vis = vis_strict = mine
            dist = (t_past + r - (page * PAGE + tpos)).astype(F32)
        rowc = lax.broadcasted_iota(I32, (n_rows, 1), 0) >> 3
        slope = jnp.zeros((n_rows, 1), F32)
        for h in range(H_MOBA):
            slope = jnp.where(rowc == h, 2.0 ** (-8.0 * (h + 1) / H_MOBA), slope)

        s_g, v_g = [], []
        for g in range(2):
            kg = k_ref[:, g * HPG:(g + 1) * HPG, :].reshape(w, HEAD_DIM).astype(BF16)
            v_g.append(v_ref[:, g * HPG:(g + 1) * HPG, :].reshape(w, HEAD_DIM).astype(BF16))
            s_g.append(_dot_nt(q_ref[g].astype(BF16), kg) * scale)

        def sb_weights(z, visible, c_ref):
            rows = z.shape[0]
            sp = _softplus_neg_abs(z)
            log_beta = -(jnp.maximum(-z, 0.0) + sp)
            log_keep = jnp.where(visible, -(jnp.maximum(z, 0.0) + sp), 0.0)
            between, c_new = _scan_blocked(jnp.concatenate(lane_blocks(log_keep), axis=0), rows, nblk,
                                           c_ref[...], after, same)
            c_ref[...] = c_new
            ws = [jnp.where(vb, jnp.exp(lb + bt), 0.0)
                  for vb, lb, bt in zip(lane_blocks(visible), lane_blocks(log_beta), between)]
            return jnp.concatenate(ws, axis=1)

        s = jnp.where(vis[:r_mb], s_g[0][:r_mb] - slope[:r_mb] * dist[:r_mb], NEG_INF)
        m = jnp.max(s, axis=-1, keepdims=True)
        p_mb = jnp.exp(s - m)
        l = jnp.sum(p_mb, axis=-1, keepdims=True)
        w_sb0 = sb_weights(s_g[0][r_mb:], vis_strict[r_mb:], sb_c0)
        pv0 = _dot(jnp.concatenate([p_mb, w_sb0], axis=0).astype(BF16), v_g[0])
        sb_acc0[...] = sb_acc0[...] + pv0[r_mb:]
        if is_new:
            own_m[...] = m
            own_l[...] = l
            own_o[...] = pv0[:r_mb]
        else:
            ksum = jnp.sum(k_ref[:, 0:HPG, :], axis=0)
            ksum_rows = jnp.concatenate(
                [jnp.broadcast_to(ksum[h:h + 1, :], (tq, HEAD_DIM)) for h in range(H_MOBA)], axis=0)
            g_col = jnp.sum(q_ref[0][:r_mb] * ksum_rows, axis=-1, keepdims=True)
            here = lane1 == page
            for h in range(H_MOBA):
                rs = slice(h * tq, (h + 1) * tq)
                mb_m[h] = jnp.where(here, m[rs], mb_m[h])
                mb_l[h] = jnp.where(here, l[rs], mb_l[h])
                mb_g[h] = jnp.where(here, g_col[rs], mb_g[h])
                for rr in range(tq):
                    mb_o[h, rr, pl.ds(page, 1), :] = pv0[h * tq + rr:h * tq + rr + 1, :]

        w_sb1 = sb_weights(s_g[1][:r_sb1], vis_strict[:r_sb1], sb_c1)
        bias_blk, fc_new = _scan_blocked(lfblk, 1, nblk, fx_c[...], after, same)
        fx_c[...] = fc_new
        n_fx = n_rows - r_sb1
        bias = jnp.concatenate([jnp.broadcast_to(bb, (n_fx, LANES)) for bb in bias_blk], axis=1)
        s = jnp.where(vis[r_sb1:], s_g[1][r_sb1:] + bias + a_col[r_sb1:], NEG_INF)
        m_run = fx_m[...]
        m_new = jnp.maximum(m_run, jnp.max(s, axis=-1, keepdims=True))
        alpha = jnp.exp(m_run - m_new)
        p_fx = jnp.exp(s - m_new)
        fx_l[...] = alpha * fx_l[...] + jnp.sum(p_fx, axis=-1, keepdims=True)
        pv1 = _dot(jnp.concatenate([w_sb1, p_fx], axis=0).astype(BF16), v_g[1])
        sb_acc1[...] = sb_acc1[...] + pv1[:r_sb1]
        fx_acc[...] = alpha * fx_acc[...] + pv1[r_sb1:]
        fx_m[...] = m_new

    @pl.when(j == 0)
    def _():
        mb_m[...] = jnp.full(mb_m.shape, NEG_INF, F32)
        mb_l[...] = jnp.zeros(mb_l.shape, F32)
        mb_g[...] = jnp.full(mb_g.shape, NEG_INF, F32)
        mb_o[...] = jnp.zeros(mb_o.shape, F32)
        sb_acc0[...] = jnp.zeros(sb_acc0.shape, F32)
        sb_acc1[...] = jnp.zeros(sb_acc1.shape, F32)
        sb_c0[...] = jnp.zeros(sb_c0.shape, F32)
        sb_c1[...] = jnp.zeros(sb_c1.shape, F32)
        fx_m[...] = jnp.full(fx_m.shape, NEG_INF, F32)
        fx_l[...] = jnp.zeros(fx_l.shape, F32)
        fx_acc[...] = jnp.zeros(fx_acc.shape, F32)
        lfn = lfn_ref[...]
        row1 = lax.broadcasted_iota(I32, (tq, LANES), 0)
        a_new = jnp.zeros((tq, LANES), F32)
        for rr in range(tq):
            a_new = a_new + jnp.where(row1 >= rr, lfn[rr:rr + 1, :], 0.0)
        a_col[...] = jnp.concatenate([jnp.zeros((r_sb1, 1), F32)]
                                     + [a_new[:, hh:hh + 1] for hh in range(H_FOX)], axis=0)
        lfnb = lfnb_ref[...]
        fx_c[...] = -_dot_exact_rhs(lfnb, same)
        process(kn_ref, vn_ref, lfnb, True)

    process(kc_ref, vc_ref, lfc_ref[...], False)

    @pl.when(j == n_pages - 1)
    def _():
        pj = lax.broadcasted_iota(I32, (LANES, LANES), 0)
        ps = lax.broadcasted_iota(I32, (LANES, LANES), 1)
        pair = ((pj >> 1) == (ps >> 1)).astype(BF16)
        n_kb = n_pages * PAGE // MOBA_BLOCK + 1
        for h in range(H_MOBA):
            rs = slice(h * tq, (h + 1) * tq)
            gate = jnp.where(lane1 < n_pages, _dot_exact_rhs(mb_g[h], pair), NEG_INF)
            sel = jnp.zeros((tq, LANES), jnp.bool_)
            for _ in range(min(MOBA_TOPK, n_kb)):
                m = jnp.max(gate, axis=-1, keepdims=True)
                idx = jnp.min(jnp.where(gate == m, lane1, LANES), axis=-1, keepdims=True)
                hit = (lane1 >> 1) == (idx >> 1)
                sel = jnp.logical_or(sel, hit)
                gate = jnp.where(hit, -jnp.inf, gate)
            sel = jnp.logical_and(sel, lane1 < n_pages)
            m_pg = mb_m[h]
            m_own = own_m[rs]
            m_fin = jnp.maximum(jnp.max(jnp.where(sel, m_pg, NEG_INF), axis=-1, keepdims=True), m_own)
            wp = jnp.where(sel, jnp.exp(m_pg - m_fin), 0.0)
            w_own = jnp.exp(m_own - m_fin)
            l_fin = jnp.sum(wp * mb_l[h], axis=-1, keepdims=True) + w_own * own_l[rs]
            rows = [_dot_acc(wp[rr:rr + 1, :n_pages], mb_o[h, rr]) for rr in range(tq)]
            o_fin = jnp.concatenate(rows, axis=0) + w_own * own_o[rs]
            o_ref[0, rs, :] = o_fin / l_fin
        o_ref[0, r_mb:, :] = sb_acc0[...]
        o_ref[1, :r_sb1, :] = sb_acc1[...]
        o_ref[1, r_sb1:, :] = fx_acc[...] / fx_l[...]


def _decode_attn(proj_s, logf_s, cache_k, cache_v, cache_lfb, page_table, layer):
    nb, n_pages = page_table.shape
    d = proj_s.shape[-1]
    tq = proj_s.shape[2] // nb
    assert tq == HPG and N_HEADS == 2 * HPG and n_pages % 2 == 0 and n_pages <= LANES
    slot0 = H_MOBA + H_SB - HPG
    n_rows = HPG * tq
    qg = proj_s[0, 0].reshape(nb, tq, 2, HPG, HEAD_DIM).transpose(0, 2, 3, 1, 4).reshape(nb, 2, n_rows, HEAD_DIM)
    pad_tok = ((0, 0), (0, TOK_PER_BLK - tq), (0, 0), (0, 0))
    knew = jnp.pad(proj_s[1, 0].reshape(nb, tq, N_HEADS, HEAD_DIM), pad_tok)
    vnew = jnp.pad(proj_s[2, 0].reshape(nb, tq, N_HEADS, HEAD_DIM), pad_tok)
    lfnb = jnp.pad(logf_s[0, :, :H_FOX].reshape(nb, tq, H_FOX),
                   ((0, 0), (0, TOK_PER_BLK - tq), (slot0, 0))).reshape(nb, 1, LANES)
    kern = functools.partial(_decode_kernel, n_pages=n_pages, tq=tq)
    cache_idx = lambda b, j, pt: (layer, pt[b, n_pages - 1 - j], 0, 0, 0)
    lf_idx = lambda b, j, pt: (layer, pt[b, n_pages - 1 - j], 0, 0)
    r_mb = H_MOBA * tq
    r_sb1 = slot0 * tq
    vm = pltpu.VMEM
    out = pl.pallas_call(
        kern,
        grid_spec=pltpu.PrefetchScalarGridSpec(
            num_scalar_prefetch=1,
            grid=(nb, n_pages),
            in_specs=[
                pl.BlockSpec((None, 2, n_rows, HEAD_DIM), lambda b, j, pt: (b, 0, 0, 0)),
                pl.BlockSpec((None, TOK_PER_BLK, N_HEADS, HEAD_DIM), lambda b, j, pt: (b, 0, 0, 0)),
                pl.BlockSpec((None, TOK_PER_BLK, N_HEADS, HEAD_DIM), lambda b, j, pt: (b, 0, 0, 0)),
                pl.BlockSpec((None, tq, LANES), lambda b, j, pt: (0, b, 0)),
                pl.BlockSpec((None, 1, LANES), lambda b, j, pt: (b, 0, 0)),
                pl.BlockSpec((None, None, PAGE, N_HEADS, HEAD_DIM), cache_idx),
                pl.BlockSpec((None, None, PAGE, N_HEADS, HEAD_DIM), cache_idx),
                pl.BlockSpec((None, None, PAGE // TOK_PER_BLK, LANES), lf_idx),
            ],
            out_specs=pl.BlockSpec((None, 2, n_rows, HEAD_DIM), lambda b, j, pt: (b, 0, 0, 0)),
            scratch_shapes=[
                vm((H_MOBA, tq, LANES), F32), vm((H_MOBA, tq, LANES), F32), vm((H_MOBA, tq, LANES), F32),
                vm((H_MOBA, tq, n_pages, HEAD_DIM), F32),
                vm((r_mb, 1), F32), vm((r_mb, 1), F32), vm((r_mb, HEAD_DIM), F32),
                vm((n_rows - r_mb, HEAD_DIM), F32), vm((r_sb1, HEAD_DIM), F32),
                vm((n_rows - r_mb, LANES), F32), vm((r_sb1, LANES), F32),
                vm((n_rows - r_sb1, 1), F32), vm((n_rows - r_sb1, 1), F32), vm((n_rows - r_sb1, HEAD_DIM), F32),
                vm((1, LANES), F32), vm((n_rows, 1), F32),
            ],
        ),
        out_shape=jax.ShapeDtypeStruct((nb, 2, n_rows, HEAD_DIM), F32),
        compiler_params=_cparams("parallel", "arbitrary"),
        name="decode_attn",
    )(page_table, qg, knew, vnew, logf_s, lfnb, cache_k, cache_v, cache_lfb)
    return out.reshape(nb, 2, HPG, tq, HEAD_DIM).transpose(0, 3, 1, 2, 4).reshape(nb * tq, d)


def _merge_kernel(om_ref, os_ref, of_ref, gm_ref, gs_ref, gf_ref, x_ref, g1_ref,
                  wm_ref, ws_ref, wf_ref, wo_ref, o_ref, mrg_scr):
    j = pl.program_id(2)

    @pl.when(j == 0)
    def _():
        merged = (gm_ref[...] * _dot(om_ref[...].astype(BF16), wm_ref[...])
                  + gs_ref[...] * _dot(os_ref[...].astype(BF16), ws_ref[...])
                  + gf_ref[...] * _dot(of_ref[...].astype(BF16), wf_ref[...]))
        mrg_scr[...] = merged.astype(BF16)

    o_ref[...] = x_ref[...] + g1_ref[...] * _dot(mrg_scr[...], wo_ref[...])


def _merge(o_moba, o_sb, o_fox, proj, x, mod, w_m, w_s, w_f, w_o, *, tr):
    G, R, D = x.shape
    rs = mod.shape[1]
    tn = 1024
    per = D // tn
    rblk = 1 if rs == 1 else tr
    ridx = (lambda i: 0) if rs == 1 else (lambda i: i)
    full = lambda a: pl.BlockSpec(a.shape, lambda g_, i, j: (0, 0))
    return pl.pallas_call(
        _merge_kernel,
        grid=(G, R // tr, per),
        in_specs=[
            pl.BlockSpec((None, tr, o_moba.shape[-1]), lambda g_, i, j: (g_, i, 0)),
            pl.BlockSpec((None, tr, o_sb.shape[-1]), lambda g_, i, j: (g_, i, 0)),
            pl.BlockSpec((None, tr, o_fox.shape[-1]), lambda g_, i, j: (g_, i, 0)),
            pl.BlockSpec((None, None, tr, D), lambda g_, i, j: (3, g_, i, 0)),
            pl.BlockSpec((None, None, tr, D), lambda g_, i, j: (4, g_, i, 0)),
            pl.BlockSpec((None, None, tr, D), lambda g_, i, j: (5, g_, i, 0)),
            pl.BlockSpec((None, tr, tn), lambda g_, i, j: (g_, i, j)),
            pl.BlockSpec((None, rblk, tn), lambda g_, i, j: (g_, ridx(i), 2 * per + j)),
            full(w_m), full(w_s), full(w_f),
            pl.BlockSpec((D, tn), lambda g_, i, j: (0, j)),
        ],
        out_specs=pl.BlockSpec((None, tr, tn), lambda g_, i, j: (g_, i, j)),
        out_shape=jax.ShapeDtypeStruct((G, R, D), F32),
        scratch_shapes=[pltpu.VMEM((tr, D), BF16)],
        compiler_params=_cparams("parallel", "parallel", "arbitrary"),
        name="merge",
    )(o_moba, o_sb, o_fox, proj, proj, proj, x, mod, w_m, w_s, w_f, w_o)


def _norm_router_kernel(x_ref, sh_ref, sc_ref, g_ref, wr_ref, br_ref, h_ref, idx_ref, gate_ref):
    h = _rms_modulate(x_ref[...], g_ref[...], sc_ref[...], sh_ref[...])
    h_ref[...] = h
    tr = h.shape[0]
    lane = lax.broadcasted_iota(I32, (tr, LANES), 1)
    logits = _dot_acc(h, wr_ref[...]) + br_ref[...]
    vals, idxs = [], []
    for _ in range(TOP_K):
        m = jnp.max(logits, axis=-1, keepdims=True)
        idx = jnp.min(jnp.where(logits == m, lane, LANES), axis=-1, keepdims=True)
        vals.append(m)
        idxs.append(idx)
        logits = jnp.where(lane == idx, -jnp.inf, logits)
    es = [jnp.exp(v - vals[0]) for v in vals]
    denom = es[0] + es[1] + es[2] + es[3]
    idx_out = jnp.zeros((tr, LANES), I32)
    gate_out = jnp.zeros((tr, LANES), F32)
    for k in range(TOP_K):
        idx_out = jnp.where(lane == k, idxs[k], idx_out)
        gate_out = jnp.where(lane == k, es[k] / denom, gate_out)
    idx_ref[...] = idx_out
    gate_ref[...] = gate_out


def _norm_router(x, mod, g, w_r, b_r, *, tr):
    G, R, D = x.shape
    rs = mod.shape[1]
    rblk = 1 if rs == 1 else tr
    ridx = (lambda i: 0) if rs == 1 else (lambda i: i)
    return pl.pallas_call(
        _norm_router_kernel,
        grid=(G, R // tr),
        in_specs=[
            pl.BlockSpec((None, tr, D), lambda g_, i: (g_, i, 0)),
            pl.BlockSpec((None, rblk, D), lambda g_, i: (g_, ridx(i), 3)),
            pl.BlockSpec((None, rblk, D), lambda g_, i: (g_, ridx(i), 4)),
            pl.BlockSpec((1, D), lambda g_, i: (0, 0)),
            pl.BlockSpec((D, LANES), lambda g_, i: (0, 0)),
            pl.BlockSpec((1, LANES), lambda g_, i: (0, 0)),
        ],
        out_specs=[
            pl.BlockSpec((None, tr, D), lambda g_, i: (g_, i, 0)),
            pl.BlockSpec((None, tr, LANES), lambda g_, i: (g_, i, 0)),
            pl.BlockSpec((None, tr, LANES), lambda g_, i: (g_, i, 0)),
        ],
        out_shape=[
            jax.ShapeDtypeStruct((G, R, D), F32),
            jax.ShapeDtypeStruct((G, R, LANES), I32),
            jax.ShapeDtypeStruct((G, R, LANES), F32),
        ],
        compiler_params=_cparams("parallel", "parallel"),
        name="norm_router",
    )(x, mod, mod, g, w_r, b_r)


def _route(top_idx, tm):
    n_tok = top_idx.shape[0]
    n_assign = n_tok * TOP_K
    n_tiles = -(-(n_assign + N_EXPERTS * (tm - 1)) // tm)
    flat_e = top_idx.reshape(-1)
    onehot = (flat_e[:, None] == jnp.arange(N_EXPERTS, dtype=I32)[None, :]).astype(I32)
    sizes = jnp.sum(onehot, axis=0)
    rank = jnp.sum(jnp.cumsum(onehot, axis=0) * onehot, axis=1) - 1
    padded = (sizes + tm - 1) // tm * tm
    pad_end = jnp.cumsum(padded)
    pad_start = pad_end - padded
    dest = (pad_start[flat_e] + rank).astype(I32)
    slot_tok = jnp.zeros((n_tiles * tm,), I32).at[dest].set(jnp.arange(n_assign, dtype=I32) // TOP_K)
    tile_e = jnp.minimum(jnp.searchsorted(pad_end, jnp.arange(n_tiles, dtype=I32) * tm, side="right"),
                         N_EXPERTS - 1).astype(I32)
    n_used = (pad_end[-1] // tm).astype(I32).reshape(1)
    return dest, slot_tok, tile_e, n_used


def _gather_kernel(slot_tok_ref, n_used_ref, h_ref, o_ref, buf, sem, *, tm):
    t = pl.program_id(0)

    def row_copy(r):
        tok = slot_tok_ref[t * tm + r]
        return pltpu.make_async_copy(h_ref.at[pl.ds(tok, 1)], buf.at[pl.ds(r, 1)], sem)

    @pl.when(t < n_used_ref[0])
    def _():
        def start(r, c):
            row_copy(r).start()
            return c

        def wait(r, c):
            row_copy(r).wait()
            return c

        lax.fori_loop(0, tm, start, 0)
        lax.fori_loop(0, tm, wait, 0)
        o_ref[...] = buf[...].astype(o_ref.dtype)

    @pl.when(t >= n_used_ref[0])
    def _():
        o_ref[...] = jnp.zeros(o_ref.shape, o_ref.dtype)


def _gather_rows(h, slot_tok, n_used, tm):
    n_slots = slot_tok.shape[0]
    d = h.shape[1]
    return pl.pallas_call(
        functools.partial(_gather_kernel, tm=tm),
        grid_spec=pltpu.PrefetchScalarGridSpec(
            num_scalar_prefetch=2,
            grid=(n_slots // tm,),
            in_specs=[pl.BlockSpec(memory_space=pl.ANY)],
            out_specs=pl.BlockSpec((tm, d), lambda t, st, nu: (t, 0)),
            scratch_shapes=[pltpu.VMEM((tm, d), h.dtype), pltpu.SemaphoreType.DMA],
        ),
        out_shape=jax.ShapeDtypeStruct((n_slots, d), BF16),
        compiler_params=_cparams("arbitrary"),
        name="gather_rows",
    )(slot_tok, n_used, h)


def _expert_changed(te_ref, i):
    prev = te_ref[jnp.maximum(i - 1, 0)]
    return jnp.logical_or(i == 0, te_ref[i] != prev)


def _expert_gu_kernel(te_ref, nu_ref, x_ref, wg_ref, wu_ref, bg_ref, bu_ref, o_ref, wg_scr, wu_scr):
    i = pl.program_id(1)

    @pl.when(i < nu_ref[0])
    def _():
        @pl.when(_expert_changed(te_ref, i))
        def _():
            wg_scr[...] = wg_ref[...].astype(BF16)
            wu_scr[...] = wu_ref[...].astype(BF16)

        x = x_ref[...]
        g = _dot(x, wg_scr[...]) + bg_ref[...]
        u = _dot(x, wu_scr[...]) + bu_ref[...]
        g = jnp.minimum(g, SWIGLU_LIMIT)
        u = jnp.clip(u, -SWIGLU_LIMIT, SWIGLU_LIMIT)
        o_ref[...] = ((u + 1.0) * g * jax.nn.sigmoid(SWIGLU_ALPHA * g)).astype(BF16)

    @pl.when(i >= nu_ref[0])
    def _():
        o_ref[...] = jnp.zeros(o_ref.shape, o_ref.dtype)


def _expert_gu(xs, w_gu, b_gu, tile_e, n_used, tm):
    n_slots, d = xs.shape
    de = w_gu.shape[2] // 2
    tn = 1024
    nj = de // tn
    row = lambda j, i, te, nu: jnp.minimum(i, nu[0] - 1)
    exp = lambda j, i, te, nu: te[jnp.minimum(i, nu[0] - 1)]
    b3 = b_gu.reshape(w_gu.shape[0], 1, 2 * de)
    return pl.pallas_call(
        _expert_gu_kernel,
        grid_spec=pltpu.PrefetchScalarGridSpec(
            num_scalar_prefetch=2,
            grid=(nj, n_slots // tm),
            in_specs=[
                pl.BlockSpec((tm, d), lambda j, i, te, nu: (row(j, i, te, nu), 0)),
                pl.BlockSpec((None, d, tn), lambda j, i, te, nu: (exp(j, i, te, nu), 0, j)),
                pl.BlockSpec((None, d, tn), lambda j, i, te, nu: (exp(j, i, te, nu), 0, nj + j)),
                pl.BlockSpec((None, 1, tn), lambda j, i, te, nu: (exp(j, i, te, nu), 0, j)),
                pl.BlockSpec((None, 1, tn), lambda j, i, te, nu: (exp(j, i, te, nu), 0, nj + j)),
            ],
            out_specs=pl.BlockSpec((tm, tn), lambda j, i, te, nu: (i, j)),
            scratch_shapes=[pltpu.VMEM((d, tn), BF16), pltpu.VMEM((d, tn), BF16)],
        ),
        out_shape=jax.ShapeDtypeStruct((n_slots, de), BF16),
        compiler_params=_cparams("arbitrary", "arbitrary"),
        name="expert_gu",
    )(tile_e, n_used, xs, w_gu, w_gu, b3, b3)


def _expert_down_kernel(te_ref, nu_ref, a_ref, w_ref, b_ref, o_ref, w_scr):
    i = pl.program_id(1)

    @pl.when(i < nu_ref[0])
    def _():
        @pl.when(_expert_changed(te_ref, i))
        def _():
            w_scr[...] = w_ref[...].astype(BF16)

        o_ref[...] = _dot(a_ref[...], w_scr[...]) + b_ref[...]

    @pl.when(i >= nu_ref[0])
    def _():
        o_ref[...] = jnp.zeros(o_ref.shape, o_ref.dtype)


def _expert_down(act, w_down, b_down, tile_e, n_used, tm):
    n_slots, de = act.shape
    d = w_down.shape[2]
    tn = 1024
    nj = d // tn
    row = lambda j, i, te, nu: jnp.minimum(i, nu[0] - 1)
    exp = lambda j, i, te, nu: te[jnp.minimum(i, nu[0] - 1)]
    return pl.pallas_call(
        _expert_down_kernel,
        grid_spec=pltpu.PrefetchScalarGridSpec(
            num_scalar_prefetch=2,
            grid=(nj, n_slots // tm),
            in_specs=[
                pl.BlockSpec((tm, de), lambda j, i, te, nu: (row(j, i, te, nu), 0)),
                pl.BlockSpec((None, de, tn), lambda j, i, te, nu: (exp(j, i, te, nu), 0, j)),
                pl.BlockSpec((None, 1, tn), lambda j, i, te, nu: (exp(j, i, te, nu), 0, j)),
            ],
            out_specs=pl.BlockSpec((tm, tn), lambda j, i, te, nu: (i, j)),
            scratch_shapes=[pltpu.VMEM((de, tn), BF16)],
        ),
        out_shape=jax.ShapeDtypeStruct((n_slots, d), F32),
        compiler_params=_cparams("arbitrary", "arbitrary"),
        name="expert_down",
    )(tile_e, n_used, act, w_down, b_down.reshape(w_down.shape[0], 1, d))


def _combine_kernel(dest_ref, y_ref, gate_ref, x_ref, g2_ref, gfin_ref, o_ref, ybuf, sem, *, tt, tok0, final):
    g_ = pl.program_id(0)
    i = pl.program_id(1)
    base = (tok0 + (g_ * pl.num_programs(1) + i) * tt) * TOP_K

    def row_copy(r, k):
        slot = dest_ref[base + r * TOP_K + k]
        return pltpu.make_async_copy(y_ref.at[pl.ds(slot, 1)], ybuf.at[k, pl.ds(r, 1)], sem)

    def start(r, c):
        for k in range(TOP_K):
            row_copy(r, k).start()
        return c

    def wait(r, c):
        for k in range(TOP_K):
            row_copy(r, k).wait()
        return c

    lax.fori_loop(0, tt, start, 0)
    lax.fori_loop(0, tt, wait, 0)
    gates = gate_ref[...]
    moe = gates[:, 0:1] * ybuf[0]
    for k in range(1, TOP_K):
        moe = moe + gates[:, k:k + 1] * ybuf[k]
    x2 = x_ref[...] + g2_ref[...] * moe
    if final:
        x2 = x2 * lax.rsqrt(jnp.mean(x2 * x2, axis=-1, keepdims=True) + RMS_EPS) * gfin_ref[...]
    o_ref[...] = x2


def _combine(y, dest, gates, x1, mod, g_final, *, tt, tok0, final):
    G, R, D = x1.shape
    rs = mod.shape[1]
    rblk = 1 if rs == 1 else tt
    ridx = (lambda i: 0) if rs == 1 else (lambda i: i)
    kern = functools.partial(_combine_kernel, tt=tt, tok0=tok0, final=final)
    return pl.pallas_call(
        kern,
        grid_spec=pltpu.PrefetchScalarGridSpec(
            num_scalar_prefetch=1,
            grid=(G, R // tt),
            in_specs=[
                pl.BlockSpec(memory_space=pl.ANY),
                pl.BlockSpec((None, tt, LANES), lambda g_, i, d_: (g_, i, 0)),
                pl.BlockSpec((None, tt, D), lambda g_, i, d_: (g_, i, 0)),
                pl.BlockSpec((None, rblk, D), lambda g_, i, d_: (g_, ridx(i), 5)),
                pl.BlockSpec((1, D), lambda g_, i, d_: (0, 0)),
            ],
            out_specs=pl.BlockSpec((None, tt, D), lambda g_, i, d_: (g_, i, 0)),
            scratch_shapes=[pltpu.VMEM((TOP_K, tt, D), F32), pltpu.SemaphoreType.DMA],
        ),
        out_shape=jax.ShapeDtypeStruct((G, R, D), F32),
        compiler_params=_cparams("arbitrary", "arbitrary"),
        name="combine",
    )(dest, y, gates, x1, mod, g_final)


def _row_tile(rows, cap):
    return max(t for t in range(MOBA_BLOCK, cap + 1, MOBA_BLOCK) if rows % t == 0)


def _layer_weights(l, w_in, b_forget, w_br_moba, w_br_sb, w_br_fox, w_out, w_router, b_router):
    d = w_in.shape[1]
    w_l = w_in[l]
    w_main = jnp.concatenate([w_l[:, :3 * D_ATTN], w_l[:, 3 * D_ATTN + H_FOX:]], axis=1).astype(BF16)
    w_f = jnp.pad(w_l[:, 3 * D_ATTN:3 * D_ATTN + H_FOX], ((0, 0), (0, LANES - H_FOX)))
    b_f = jnp.pad(b_forget[l], (0, LANES - H_FOX)).reshape(1, LANES)
    w_r = jnp.pad(w_router[l], ((0, 0), (0, LANES - N_EXPERTS)))
    b_r = jnp.pad(b_router[l], (0, LANES - N_EXPERTS), constant_values=NEG_INF).reshape(1, LANES)
    return dict(w_main=w_main, w_f=w_f, b_f=b_f, w_r=w_r, b_r=b_r,
                w_m=w_br_moba[l].astype(BF16), w_s=w_br_sb[l].astype(BF16), w_x=w_br_fox[l].astype(BF16),
                w_o=w_out[l].astype(BF16))


def kernel(x_prompt, x_sample, c_prompt, c_sample, cache_k, cache_v, cache_log_f, page_table, w_mod, b_mod,
           g_attn, w_in, b_forget, w_br_moba, w_br_sb, w_br_fox, w_out, g_ffn, w_router, b_router, w_gu, b_gu,
           w_down, b_down, g_final):
    depth = w_mod.shape[0]
    B, T, D = x_prompt.shape
    nb, tq, _ = x_sample.shape
    n_pool = cache_k.shape[1]
    n_prompt = B * T
    n_samp = nb * tq
    tm = EXPERT_TILE

    clf = jnp.pad(cache_log_f, ((0, 0), (0, 0), (0, 0), (HPG - H_FOX, 0))).reshape(
        depth, n_pool, PAGE // TOK_PER_BLK, LANES)
    c_all = jnp.pad(jnp.concatenate([c_prompt, c_sample], axis=0), ((0, 16 - B - nb), (0, 0)))
    wgu = w_gu.reshape(depth * N_EXPERTS, D, w_gu.shape[-1])
    bgu = b_gu.reshape(depth * N_EXPERTS, b_gu.shape[-1])
    wdn = w_down.reshape(depth * N_EXPERTS, w_down.shape[2], D)
    bdn = b_down.reshape(depth * N_EXPERTS, D)
    g_fin = g_final.reshape(1, D)

    xp = x_prompt
    xs = x_sample.reshape(1, n_samp, D)
    kp, vp, fp, ks, vs, fs = [], [], [], [], [], []
    for l in range(depth):
        lw = _layer_weights(l, w_in, b_forget, w_br_moba, w_br_sb, w_br_fox, w_out, w_router, b_router)
        g_a = g_attn[l].reshape(1, D)
        g_f = g_ffn[l].reshape(1, D)
        mod = _mod(c_all, w_mod, b_mod, l)
        mod_p = mod[:B].reshape(B, 1, 6 * D)
        mod_s = jnp.repeat(mod[B:B + nb], tq, axis=0).reshape(1, n_samp, 6 * D)

        proj_p, lf_p = _ln_proj(xp, mod_p, g_a, lw["w_main"], lw["w_f"], lw["b_f"], tr=_row_tile(T, 1024))
        fc, fr = _fcum(lf_p)
        x1p = _merge(_moba(proj_p), _sb(proj_p), _fox(proj_p, fc, fr), proj_p, xp, mod_p,
                     lw["w_m"], lw["w_s"], lw["w_x"], lw["w_o"], tr=256)
        proj_s, lf_s = _ln_proj(xs, mod_s, g_a, lw["w_main"], lw["w_f"], lw["b_f"], tr=n_samp)
        o_s = _decode_attn(proj_s, lf_s, cache_k, cache_v, clf, page_table, l).reshape(1, n_samp, D)
        n_m, n_s = H_MOBA * HEAD_DIM, H_SB * HEAD_DIM
        x1s = _merge(o_s[..., :n_m], o_s[..., n_m:n_m + n_s], o_s[..., n_m + n_s:], proj_s, xs, mod_s,
                     lw["w_m"], lw["w_s"], lw["w_x"], lw["w_o"], tr=n_samp)

        h2p, idx_p, gate_p = _norm_router(x1p, mod_p, g_f, lw["w_r"], lw["b_r"], tr=512)
        h2s, idx_s, gate_s = _norm_router(x1s, mod_s, g_f, lw["w_r"], lw["b_r"], tr=n_samp)
        h2 = jnp.concatenate([h2p.reshape(n_prompt, D), h2s.reshape(n_samp, D)], axis=0)
        top_idx = jnp.concatenate([idx_p.reshape(n_prompt, LANES)[:, :TOP_K],
                                   idx_s.reshape(n_samp, LANES)[:, :TOP_K]], axis=0)
        dest, slot_tok, tile_e, n_used = _route(top_idx, tm)
        rows = _gather_rows(h2, slot_tok, n_used, tm)
        act = _expert_gu(rows, wgu, bgu, tile_e + l * N_EXPERTS, n_used, tm)
        y = _expert_down(act, wdn, bdn, tile_e + l * N_EXPERTS, n_used, tm)
        final = l == depth - 1
        xp = _combine(y, dest, gate_p, x1p, mod_p, g_fin, tt=256, tok0=0, final=final)
        xs = _combine(y, dest, gate_s, x1s, mod_s, g_fin, tt=n_samp, tok0=n_prompt, final=final)

        kp.append(proj_p[1]); vp.append(proj_p[2]); fp.append(lf_p[..., :H_FOX])
        ks.append(proj_s[1]); vs.append(proj_s[2]); fs.append(lf_s[..., :H_FOX])

    hd = (N_HEADS, HEAD_DIM)
    return (xp, xs.reshape(nb, tq, D),
            jnp.stack(kp).reshape(depth, B, T, *hd), jnp.stack(vp).reshape(depth, B, T, *hd), jnp.stack(fp),
            jnp.stack(ks).reshape(depth, nb, tq, *hd), jnp.stack(vs).reshape(depth, nb, tq, *hd),
            jnp.stack(fs).reshape(depth, nb, tq, H_FOX))
```

```python
import functools

import jax
import jax.numpy as jnp
from jax import lax
from jax.experimental import pallas as pl
from jax.experimental.pallas import tpu as pltpu

F32 = jnp.float32
BF16 = jnp.bfloat16
I32 = jnp.int32

HEAD_DIM = 128
N_HEADS = 16
H_MOBA = 6
H_SB = 5
H_FOX = 5
D_ATTN = N_HEADS * HEAD_DIM
MOBA_BLOCK = 256
MOBA_TOPK = 3
N_EXPERTS = 32
TOP_K = 4
SWIGLU_ALPHA = 1.702
SWIGLU_LIMIT = 7.0
RMS_EPS = 1e-5
NEG_INF = -1e30
PAGE = 128
LANES = 128
VMEM_LIMIT_BYTES = 56 * 1024 * 1024
EXPERT_TILE = 256


def _cparams(*sem):
    return pltpu.CompilerParams(dimension_semantics=sem, vmem_limit_bytes=VMEM_LIMIT_BYTES)


def _dot(a, b):
    return jnp.dot(a, b, preferred_element_type=F32)


def _dot_nt(a, b):
    return lax.dot_general(a, b, (((1,), (1,)), ((), ())), preferred_element_type=F32)


def _split2(x):
    hi = x.astype(BF16)
    lo = (x - hi.astype(F32)).astype(BF16)
    return hi, lo


def _split3(x):
    hi = x.astype(BF16)
    r = x - hi.astype(F32)
    mid = r.astype(BF16)
    lo = (r - mid.astype(F32)).astype(BF16)
    return hi, mid, lo


def _dot_acc(a, b):
    a_hi, a_lo = _split2(a)
    b_hi, b_lo = _split2(b)
    return _dot(a_hi, b_hi) + _dot(a_lo, b_hi) + _dot(a_hi, b_lo)


def _dot_nt_acc(a, b):
    a_hi, a_lo = _split2(a)
    b_hi, b_lo = _split2(b)
    return _dot_nt(a_hi, b_hi) + _dot_nt(a_lo, b_hi) + _dot_nt(a_hi, b_lo)


def _dot_exact_rhs(a, b_exact):
    hi, mid, lo = _split3(a)
    return _dot(hi, b_exact) + _dot(mid, b_exact) + _dot(lo, b_exact)


def _dot_exact_lhs(a_exact, b):
    hi, mid, lo = _split3(b)
    return _dot(a_exact, hi) + _dot(a_exact, mid) + _dot(a_exact, lo)


def _softplus_neg_abs(z):
    return jnp.log1p(jnp.exp(-jnp.abs(z)))


def _log_sigmoid(z):
    return -(jnp.maximum(-z, 0.0) + _softplus_neg_abs(z))


def _rms_modulate(x, g, scale, shift):
    y = x * lax.rsqrt(jnp.mean(x * x, axis=-1, keepdims=True) + RMS_EPS)
    return (y * g) * (1.0 + scale) + shift


def _mod_kernel(c_ref, w_ref, b_ref, o_ref):
    c = c_ref[...]
    a = (c * jax.nn.sigmoid(c)).astype(BF16)
    o_ref[...] = _dot(a, w_ref[...].astype(BF16)) + b_ref[...]


def _mod(c, w, b, layer):
    m, d = c.shape
    depth, _, n = w.shape
    tn = 1024
    return pl.pallas_call(
        _mod_kernel,
        grid=(n // tn,),
        in_specs=[
            pl.BlockSpec((m, d), lambda j: (0, 0)),
            pl.BlockSpec((None, d, tn), lambda j: (layer, 0, j)),
            pl.BlockSpec((None, 1, tn), lambda j: (layer, 0, j)),
        ],
        out_specs=pl.BlockSpec((m, tn), lambda j: (0, j)),
        out_shape=jax.ShapeDtypeStruct((m, n), F32),
        compiler_params=_cparams("parallel"),
        name="mod",
    )(c, w, b.reshape(depth, 1, n))


def _ln_proj_kernel(x_ref, sh_ref, sc_ref, g_ref, w_ref, wf_ref, bf_ref, kprev_ref, vprev_ref,
                    qg_ref, k_ref, v_ref, lf_ref, h_scr, *, per):
    del kprev_ref, vprev_ref
    j = pl.program_id(2)

    @pl.when(j == 0)
    def _():
        h = _rms_modulate(x_ref[...], g_ref[...], sc_ref[...], sh_ref[...])
        h_scr[...] = h.astype(BF16)
        lf_ref[...] = _log_sigmoid(_dot_acc(h, wf_ref[...]) + bf_ref[...])

    acc = _dot(h_scr[...], w_ref[...])

    @pl.when(j < per)
    def _():
        qg_ref[...] = acc

    @pl.when(jnp.logical_and(j >= per, j < 2 * per))
    def _():
        k_ref[...] = acc

    @pl.when(jnp.logical_and(j >= 2 * per, j < 3 * per))
    def _():
        v_ref[...] = acc

    @pl.when(j >= 3 * per)
    def _():
        qg_ref[...] = jax.nn.sigmoid(acc)


def _ln_proj(x, mod, g, w_main, w_f, b_f, k_all, v_all, layer, *, tr):
    G, R, D = x.shape
    rs = mod.shape[1]
    tn = 512
    per = D // tn
    n_col = w_main.shape[1] // tn
    rblk = 1 if rs == 1 else tr
    ridx = (lambda i: 0) if rs == 1 else (lambda i: i)
    kern = functools.partial(_ln_proj_kernel, per=per)
    clamp = lambda j, lo: jnp.clip(j - lo, 0, per - 1)

    def qg_idx(g_, i, j):
        grp = jnp.where(j < 3 * per, 0, j // per - 2)
        col = jnp.where(j < per, j, jnp.where(j < 3 * per, per - 1, j % per))
        return (grp, g_, i, col)

    return pl.pallas_call(
        kern,
        grid=(G, R // tr, n_col),
        in_specs=[
            pl.BlockSpec((None, tr, D), lambda g_, i, j: (g_, i, 0)),
            pl.BlockSpec((None, rblk, D), lambda g_, i, j: (g_, ridx(i), 0)),
            pl.BlockSpec((None, rblk, D), lambda g_, i, j: (g_, ridx(i), 1)),
            pl.BlockSpec((1, D), lambda g_, i, j: (0, 0)),
            pl.BlockSpec((D, tn), lambda g_, i, j: (0, j)),
            pl.BlockSpec((D, LANES), lambda g_, i, j: (0, 0)),
            pl.BlockSpec((1, LANES), lambda g_, i, j: (0, 0)),
            pl.BlockSpec(memory_space=pl.ANY),
            pl.BlockSpec(memory_space=pl.ANY),
        ],
        out_specs=[
            pl.BlockSpec((None, None, tr, tn), qg_idx),
            pl.BlockSpec((None, None, tr, tn), lambda g_, i, j: (layer, g_, i, clamp(j, per))),
            pl.BlockSpec((None, None, tr, tn), lambda g_, i, j: (layer, g_, i, clamp(j, 2 * per))),
            pl.BlockSpec((None, tr, LANES), lambda g_, i, j: (g_, i, 0)),
        ],
        out_shape=[
            jax.ShapeDtypeStruct((n_col // per - 2, G, R, D), F32),
            jax.ShapeDtypeStruct(k_all.shape, F32),
            jax.ShapeDtypeStruct(v_all.shape, F32),
            jax.ShapeDtypeStruct((G, R, LANES), F32),
        ],
        input_output_aliases={7: 1, 8: 2},
        scratch_shapes=[pltpu.VMEM((tr, D), BF16)],
        compiler_params=_cparams("parallel", "parallel", "arbitrary"),
        name="ln_proj",
    )(x, mod, mod, g, w_main, w_f, b_f, k_all, v_all)


def _fcum_kernel(lf_ref, fc_ref, fr_ref):
    t = lf_ref.shape[0]
    r = lax.broadcasted_iota(I32, (t, t), 0)
    c = lax.broadcasted_iota(I32, (t, t), 1)
    tri = (r >= c).astype(BF16)
    f = _dot_exact_lhs(tri, lf_ref[...])
    fc_ref[...] = f
    fr_ref[...] = f.T[:8, :]


def _fcum(logf):
    B, T, _ = logf.shape
    return pl.pallas_call(
        _fcum_kernel,
        grid=(B,),
        in_specs=[pl.BlockSpec((None, T, LANES), lambda b: (b, 0, 0))],
        out_specs=[
            pl.BlockSpec((None, T, LANES), lambda b: (b, 0, 0)),
            pl.BlockSpec((None, 8, T), lambda b: (b, 0, 0)),
        ],
        out_shape=[jax.ShapeDtypeStruct((B, T, LANES), F32), jax.ShapeDtypeStruct((B, 8, T), F32)],
        compiler_params=_cparams("parallel"),
        name="fcum",
    )(logf)


def _head_specs(lead, head0, n_heads, rows, row_idx):
    return [
        pl.BlockSpec((None, None, rows, HEAD_DIM), lambda b, i, h=h: (lead, b, row_idx(i), head0 + h))
        for h in range(n_heads)
    ]


def _moba_kernel(*refs, n_heads, tq, n_kb):
    q_refs, k_refs, v_refs = refs[:n_heads], refs[n_heads:2 * n_heads], refs[2 * n_heads:3 * n_heads]
    o_ref, m_scr, l_scr, acc_scr, sel_scr = refs[3 * n_heads:]
    i = pl.program_id(1)
    scale = HEAD_DIM ** -0.5
    n_sel = min(MOBA_TOPK, n_kb)
    lane = lax.broadcasted_iota(I32, (tq, LANES), 1)
    qpos = i * tq + lax.broadcasted_iota(I32, (tq, MOBA_BLOCK), 0)
    koff = lax.broadcasted_iota(I32, (tq, MOBA_BLOCK), 1)
    for h in range(n_heads):
        q = q_refs[h][...]
        kmean = jnp.sum(k_refs[h][...].reshape(n_kb, MOBA_BLOCK, HEAD_DIM), axis=1) * (1.0 / MOBA_BLOCK)
        kmean = jnp.concatenate([kmean, jnp.zeros((LANES - n_kb, HEAD_DIM), F32)], axis=0)
        gate = jnp.where(lane < i, _dot_nt_acc(q, kmean), NEG_INF)
        sel = jnp.zeros((tq, LANES), jnp.bool_)
        for _ in range(n_sel):
            m = jnp.max(gate, axis=-1, keepdims=True)
            idx = jnp.min(jnp.where(gate == m, lane, LANES), axis=-1, keepdims=True)
            hit = lane == idx
            sel = jnp.logical_or(sel, hit)
            gate = jnp.where(hit, -jnp.inf, gate)
        sel_scr[h] = jnp.where(jnp.logical_and(sel, lane < i), 1.0, 0.0)
        m_scr[h] = jnp.full((tq, 1), NEG_INF, F32)
        l_scr[h] = jnp.zeros((tq, 1), F32)
        acc_scr[h] = jnp.zeros((tq, HEAD_DIM), F32)

    def body(n, carry):
        start = pl.multiple_of(n * MOBA_BLOCK, MOBA_BLOCK)
        kpos = n * MOBA_BLOCK + koff
        dist = (qpos - kpos).astype(F32)
        own = (koff * 0 + n) == i
        own_visible = jnp.logical_and(own, kpos <= qpos)
        not_own = jnp.logical_not(own)
        for h in range(n_heads):
            slope = 2.0 ** (-8.0 * (h + 1) / n_heads)
            qb = q_refs[h][...].astype(BF16)
            kn = k_refs[h][pl.ds(start, MOBA_BLOCK), :].astype(BF16)
            vn = v_refs[h][pl.ds(start, MOBA_BLOCK), :].astype(BF16)
            s = _dot_nt(qb, kn) * scale - slope * dist
            sel_n = jnp.sum(jnp.where(lane == n, sel_scr[h], 0.0), axis=-1, keepdims=True) > 0.5
            valid = jnp.logical_or(own_visible, jnp.logical_and(not_own, sel_n))
            s = jnp.where(valid, s, NEG_INF)
            m_run = m_scr[h]
            m_new = jnp.maximum(m_run, jnp.max(s, axis=-1, keepdims=True))
            alpha = jnp.exp(m_run - m_new)
            p = jnp.where(valid, jnp.exp(s - m_new), 0.0)
            l_scr[h] = alpha * l_scr[h] + jnp.sum(p, axis=-1, keepdims=True)
            acc_scr[h] = alpha * acc_scr[h] + _dot(p.astype(BF16), vn)
            m_scr[h] = m_new
        return carry

    lax.fori_loop(0, i + 1, body, 0)
    for h in range(n_heads):
        o_ref[:, h * HEAD_DIM:(h + 1) * HEAD_DIM] = acc_scr[h] / l_scr[h]


def _mixer_scratch(n_heads, tq, n_col, n_wide):
    return ([pltpu.VMEM((n_heads, tq, 1), F32)] * n_col
            + [pltpu.VMEM((n_heads, tq, HEAD_DIM), F32)] * n_wide)


def _moba(qg, k_all, v_all, layer):
    _, B, T, _ = qg.shape
    tq = MOBA_BLOCK
    n_kb = T // MOBA_BLOCK
    kern = functools.partial(_moba_kernel, n_heads=H_MOBA, tq=tq, n_kb=n_kb)
    specs = (_head_specs(0, 0, H_MOBA, tq, lambda i: i) + _head_specs(layer, 0, H_MOBA, T, lambda i: 0)
             + _head_specs(layer, 0, H_MOBA, T, lambda i: 0))
    return pl.pallas_call(
        kern,
        grid=(B, T // tq),
        in_specs=specs,
        out_specs=pl.BlockSpec((None, tq, H_MOBA * HEAD_DIM), lambda b, i: (b, i, 0)),
        out_shape=jax.ShapeDtypeStruct((B, T, H_MOBA * HEAD_DIM), F32),
        scratch_shapes=_mixer_scratch(H_MOBA, tq, 2, 2),
        compiler_params=_cparams("parallel", "arbitrary"),
        name="moba",
    )(*([qg] * H_MOBA + [k_all] * H_MOBA + [v_all] * H_MOBA))


def _strict_upper(n):
    j = lax.broadcasted_iota(I32, (n, n), 0)
    s = lax.broadcasted_iota(I32, (n, n), 1)
    return (j > s).astype(BF16)


def _sb_kernel(*refs, n_heads, tq, tk):
    q_refs, k_refs, v_refs = refs[:n_heads], refs[n_heads:2 * n_heads], refs[2 * n_heads:3 * n_heads]
    o_ref, c_scr, acc_scr = refs[3 * n_heads:]
    i = pl.program_id(1)
    scale = HEAD_DIM ** -0.5
    qpos = i * tq + lax.broadcasted_iota(I32, (tq, tk), 0)
    koff = lax.broadcasted_iota(I32, (tq, tk), 1)
    after = _strict_upper(tk)
    c_scr[...] = jnp.zeros(c_scr.shape, F32)
    acc_scr[...] = jnp.zeros(acc_scr.shape, F32)

    def body(t, carry):
        n = i - t
        start = pl.multiple_of(n * tk, tk)
        strict = (n * tk + koff) < qpos
        for h in range(n_heads):
            qb = q_refs[h][...].astype(BF16)
            kn = k_refs[h][pl.ds(start, tk), :].astype(BF16)
            vn = v_refs[h][pl.ds(start, tk), :].astype(BF16)
            z = _dot_nt(qb, kn) * scale
            sp = _softplus_neg_abs(z)
            log_beta = -(jnp.maximum(-z, 0.0) + sp)
            log_keep = jnp.where(strict, -(jnp.maximum(z, 0.0) + sp), 0.0)
            c_run = c_scr[h]
            between = _dot_exact_rhs(log_keep, after) + c_run
            w = jnp.where(strict, jnp.exp(log_beta + between), 0.0)
            acc_scr[h] = acc_scr[h] + _dot(w.astype(BF16), vn)
            c_scr[h] = c_run + jnp.sum(log_keep, axis=-1, keepdims=True)
        return carry

    lax.fori_loop(0, i + 1, body, 0)
    for h in range(n_heads):
        o_ref[:, h * HEAD_DIM:(h + 1) * HEAD_DIM] = acc_scr[h]


def _sb(qg, k_all, v_all, layer):
    _, B, T, _ = qg.shape
    tq = tk = 256
    kern = functools.partial(_sb_kernel, n_heads=H_SB, tq=tq, tk=tk)
    specs = (_head_specs(0, H_MOBA, H_SB, tq, lambda i: i) + _head_specs(layer, H_MOBA, H_SB, T, lambda i: 0)
             + _head_specs(layer, H_MOBA, H_SB, T, lambda i: 0))
    return pl.pallas_call(
        kern,
        grid=(B, T // tq),
        in_specs=specs,
        out_specs=pl.BlockSpec((None, tq, H_SB * HEAD_DIM), lambda b, i: (b, i, 0)),
        out_shape=jax.ShapeDtypeStruct((B, T, H_SB * HEAD_DIM), F32),
        scratch_shapes=_mixer_scratch(H_SB, tq, 1, 1),
        compiler_params=_cparams("parallel", "arbitrary"),
        name="sb",
    )(*([qg] * H_SB + [k_all] * H_SB + [v_all] * H_SB))


def _fox_kernel(*refs, n_heads, tq, tk):
    q_refs, k_refs, v_refs = refs[:n_heads], refs[n_heads:2 * n_heads], refs[2 * n_heads:3 * n_heads]
    fc_ref, fr_ref, o_ref, m_scr, l_scr, acc_scr = refs[3 * n_heads:]
    i = pl.program_id(1)
    scale = HEAD_DIM ** -0.5
    qpos = i * tq + lax.broadcasted_iota(I32, (tq, tk), 0)
    koff = lax.broadcasted_iota(I32, (tq, tk), 1)
    m_scr[...] = jnp.full(m_scr.shape, NEG_INF, F32)
    l_scr[...] = jnp.zeros(l_scr.shape, F32)
    acc_scr[...] = jnp.zeros(acc_scr.shape, F32)

    def body(n, carry):
        start = pl.multiple_of(n * tk, tk)
        valid = (n * tk + koff) <= qpos
        for h in range(n_heads):
            qb = q_refs[h][...].astype(BF16)
            kn = k_refs[h][pl.ds(start, tk), :].astype(BF16)
            vn = v_refs[h][pl.ds(start, tk), :].astype(BF16)
            fq = fc_ref[:, h:h + 1]
            fk = fr_ref[h:h + 1, pl.ds(start, tk)]
            s = jnp.where(valid, _dot_nt(qb, kn) * scale + (fq - fk), NEG_INF)
            m_run = m_scr[h]
            m_new = jnp.maximum(m_run, jnp.max(s, axis=-1, keepdims=True))
            alpha = jnp.exp(m_run - m_new)
            p = jnp.where(valid, jnp.exp(s - m_new), 0.0)
            l_scr[h] = alpha * l_scr[h] + jnp.sum(p, axis=-1, keepdims=True)
            acc_scr[h] = alpha * acc_scr[h] + _dot(p.astype(BF16), vn)
            m_scr[h] = m_new
        return carry

    lax.fori_loop(0, i + 1, body, 0)
    for h in range(n_heads):
        o_ref[:, h * HEAD_DIM:(h + 1) * HEAD_DIM] = acc_scr[h] / l_scr[h]


def _fox(qg, k_all, v_all, fc, fr, layer):
    _, B, T, _ = qg.shape
    tq = tk = 256
    h0 = H_MOBA + H_SB
    kern = functools.partial(_fox_kernel, n_heads=H_FOX, tq=tq, tk=tk)
    specs = (_head_specs(0, h0, H_FOX, tq, lambda i: i) + _head_specs(layer, h0, H_FOX, T, lambda i: 0)
             + _head_specs(layer, h0, H_FOX, T, lambda i: 0)
             + [pl.BlockSpec((None, tq, LANES), lambda b, i: (b, i, 0)),
                pl.BlockSpec((None, 8, T), lambda b, i: (b, 0, 0))])
    return pl.pallas_call(
        kern,
        grid=(B, T // tq),
        in_specs=specs,
        out_specs=pl.BlockSpec((None, tq, H_FOX * HEAD_DIM), lambda b, i: (b, i, 0)),
        out_shape=jax.ShapeDtypeStruct((B, T, H_FOX * HEAD_DIM), F32),
        scratch_shapes=_mixer_scratch(H_FOX, tq, 2, 1),
        compiler_params=_cparams("parallel", "arbitrary"),
        name="fox",
    )(*([qg] * H_FOX + [k_all] * H_FOX + [v_all] * H_FOX), fc, fr)


HPG = 8
TOK_PER_BLK = LANES // HPG


def _lane_class_mats():
    j = lax.broadcasted_iota(I32, (LANES, LANES), 0)
    s = lax.broadcasted_iota(I32, (LANES, LANES), 1)
    same = (j & 7) == (s & 7)
    after = jnp.logical_and(same, (j >> 3) > (s >> 3)).astype(BF16)
    return after, same.astype(BF16)


def _scan_blocked(xb, rows, nblk, carry, after, same):
    within = _dot_exact_rhs(xb, after)
    tot = _dot_exact_rhs(xb, same)
    out = [None] * nblk
    run = carry
    for b in range(nblk - 1, -1, -1):
        out[b] = within[b * rows:(b + 1) * rows] + run
        run = run + tot[b * rows:(b + 1) * rows]
    return out, run


def _decode_kernel(pt_ref, q_ref, kn_ref, vn_ref, lfn_ref, lfnb_ref, kc_ref, vc_ref, lfc_ref, o_ref,
                   mb_m, mb_l, mb_g, mb_o, own_m, own_l, own_o, sb_acc0, sb_acc1, sb_c0, sb_c1,
                   fx_m, fx_l, fx_acc, fx_c, a_col, *, n_pages, tq):
    j = pl.program_id(1)
    page = n_pages - 1 - j
    t_past = n_pages * PAGE
    scale = HEAD_DIM ** -0.5
    n_rows = HPG * tq
    r_mb = H_MOBA * tq
    r_sb1 = (H_MOBA + H_SB - HPG) * tq
    after, same = _lane_class_mats()
    lane1 = lax.broadcasted_iota(I32, (tq, LANES), 1)

    def lane_blocks(x):
        return [x[:, b * LANES:(b + 1) * LANES] for b in range(x.shape[1] // LANES)]

    def process(k_ref, v_ref, lfblk, is_new):
        t_tok = k_ref.shape[0]
        nblk = t_tok // TOK_PER_BLK
        w = t_tok * HPG
        lane = lax.broadcasted_iota(I32, (n_rows, w), 1)
        row = lax.broadcasted_iota(I32, (n_rows, w), 0)
        tpos = lane >> 3
        r = row & 7
        mine = (lane & 7) == (row >> 3)
        if is_new:
            vis = jnp.logical_and(mine, tpos <= r)
            vis_strict = jnp.logical_and(mine, tpos < r)
            dist = (r - tpos).astype(F32)
        else:
            vis = vis_strict = mine
            dist = (t_past + r - (page * PAGE + tpos)).astype(F32)
        rowc = lax.broadcasted_iota(I32, (n_rows, 1), 0) >> 3
        slope = jnp.zeros((n_rows, 1), F32)
        for h in range(H_MOBA):
            slope = jnp.where(rowc == h, 2.0 ** (-8.0 * (h + 1) / H_MOBA), slope)

        s_g, v_g = [], []
        for g in range(2):
            kg = k_ref[:, g * HPG:(g + 1) * HPG, :].reshape(w, HEAD_DIM).astype(BF16)
            v_g.append(v_ref[:, g * HPG:(g + 1) * HPG, :].reshape(w, HEAD_DIM).astype(BF16))
            s_g.append(_dot_nt(q_ref[g].astype(BF16), kg) * scale)

        def sb_weights(z, visible, c_ref):
            rows = z.shape[0]
            sp = _softplus_neg_abs(z)
            log_beta = -(jnp.maximum(-z, 0.0) + sp)
            log_keep = jnp.where(visible, -(jnp.maximum(z, 0.0) + sp), 0.0)
            between, c_new = _scan_blocked(jnp.concatenate(lane_blocks(log_keep), axis=0), rows, nblk,
                                           c_ref[...], after, same)
            c_ref[...] = c_new
            ws = [jnp.where(vb, jnp.exp(lb + bt), 0.0)
                  for vb, lb, bt in zip(lane_blocks(visible), lane_blocks(log_beta), between)]
            return jnp.concatenate(ws, axis=1)

        s = jnp.where(vis[:r_mb], s_g[0][:r_mb] - slope[:r_mb] * dist[:r_mb], NEG_INF)
        m = jnp.max(s, axis=-1, keepdims=True)
        p_mb = jnp.exp(s - m)
        l = jnp.sum(p_mb, axis=-1, keepdims=True)
        w_sb0 = sb_weights(s_g[0][r_mb:], vis_strict[r_mb:], sb_c0)
        pv0 = _dot(jnp.concatenate([p_mb, w_sb0], axis=0).astype(BF16), v_g[0])
        sb_acc0[...] = sb_acc0[...] + pv0[r_mb:]
        if is_new:
            own_m[...] = m
            own_l[...] = l
            own_o[...] = pv0[:r_mb]
        else:
            ksum = jnp.sum(k_ref[:, 0:HPG, :], axis=0)
            ksum_rows = jnp.concatenate(
                [jnp.broadcast_to(ksum[h:h + 1, :], (tq, HEAD_DIM)) for h in range(H_MOBA)], axis=0)
            g_col = jnp.sum(q_ref[0][:r_mb] * ksum_rows, axis=-1, keepdims=True)
            here = lane1 == page
            for h in range(H_MOBA):
                rs = slice(h * tq, (h + 1) * tq)
                mb_m[h] = jnp.where(here, m[rs], mb_m[h])
                mb_l[h] = jnp.where(here, l[rs], mb_l[h])
                mb_g[h] = jnp.where(here, g_col[rs], mb_g[h])
                for rr in range(tq):
                    mb_o[h, rr, pl.ds(page, 1), :] = pv0[h * tq + rr:h * tq + rr + 1, :]

        w_sb1 = sb_weights(s_g[1][:r_sb1], vis_strict[:r_sb1], sb_c1)
        bias_blk, fc_new = _scan_blocked(lfblk, 1, nblk, fx_c[...], after, same)
        fx_c[...] = fc_new
        n_fx = n_rows - r_sb1
        bias = jnp.concatenate([jnp.broadcast_to(bb, (n_fx, LANES)) for bb in bias_blk], axis=1)
        s = jnp.where(vis[r_sb1:], s_g[1][r_sb1:] + bias + a_col[r_sb1:], NEG_INF)
        m_run = fx_m[...]
        m_new = jnp.maximum(m_run, jnp.max(s, axis=-1, keepdims=True))
        alpha = jnp.exp(m_run - m_new)
        p_fx = jnp.exp(s - m_new)
        fx_l[...] = alpha * fx_l[...] + jnp.sum(p_fx, axis=-1, keepdims=True)
        pv1 = _dot(jnp.concatenate([w_sb1, p_fx], axis=0).astype(BF16), v_g[1])
        sb_acc1[...] = sb_acc1[...] + pv1[:r_sb1]
        fx_acc[...] = alpha * fx_acc[...] + pv1[r_sb1:]
        fx_m[...] = m_new

    @pl.when(j == 0)
    def _():
        mb_m[...] = jnp.full(mb_m.shape, NEG_INF, F32)
        mb_l[...] = jnp.zeros(mb_l.shape, F32)
        mb_g[...] = jnp.full(mb_g.shape, NEG_INF, F32)
        mb_o[...] = jnp.zeros(mb_o.shape, F32)
        sb_acc0[...] = jnp.zeros(sb_acc0.shape, F32)
        sb_acc1[...] = jnp.zeros(sb_acc1.shape, F32)
        sb_c0[...] = jnp.zeros(sb_c0.shape, F32)
        sb_c1[...] = jnp.zeros(sb_c1.shape, F32)
        fx_m[...] = jnp.full(fx_m.shape, NEG_INF, F32)
        fx_l[...] = jnp.zeros(fx_l.shape, F32)
        fx_acc[...] = jnp.zeros(fx_acc.shape, F32)
        lfn = lfn_ref[...]
        row1 = lax.broadcasted_iota(I32, (tq, LANES), 0)
        a_new = jnp.zeros((tq, LANES), F32)
        for rr in range(tq):
            a_new = a_new + jnp.where(row1 >= rr, lfn[rr:rr + 1, :], 0.0)
        a_col[...] = jnp.concatenate([jnp.zeros((r_sb1, 1), F32)]
                                     + [a_new[:, hh:hh + 1] for hh in range(H_FOX)], axis=0)
        lfnb = lfnb_ref[...]
        fx_c[...] = -_dot_exact_rhs(lfnb, same)
        process(kn_ref, vn_ref, lfnb, True)

    process(kc_ref, vc_ref, lfc_ref[...], False)

    @pl.when(j == n_pages - 1)
    def _():
        pj = lax.broadcasted_iota(I32, (LANES, LANES), 0)
        ps = lax.broadcasted_iota(I32, (LANES, LANES), 1)
        pair = ((pj >> 1) == (ps >> 1)).astype(BF16)
        n_kb = n_pages * PAGE // MOBA_BLOCK + 1
        for h in range(H_MOBA):
            rs = slice(h * tq, (h + 1) * tq)
            gate = jnp.where(lane1 < n_pages, _dot_exact_rhs(mb_g[h], pair), NEG_INF)
            sel = jnp.zeros((tq, LANES), jnp.bool_)
            for _ in range(min(MOBA_TOPK, n_kb)):
                m = jnp.max(gate, axis=-1, keepdims=True)
                idx = jnp.min(jnp.where(gate == m, lane1, LANES), axis=-1, keepdims=True)
                hit = (lane1 >> 1) == (idx >> 1)
                sel = jnp.logical_or(sel, hit)
                gate = jnp.where(hit, -jnp.inf, gate)
            sel = jnp.logical_and(sel, lane1 < n_pages)
            m_pg = mb_m[h]
            m_own = own_m[rs]
            m_fin = jnp.maximum(jnp.max(jnp.where(sel, m_pg, NEG_INF), axis=-1, keepdims=True), m_own)
            wp = jnp.where(sel, jnp.exp(m_pg - m_fin), 0.0)
            w_own = jnp.exp(m_own - m_fin)
            l_fin = jnp.sum(wp * mb_l[h], axis=-1, keepdims=True) + w_own * own_l[rs]
            rows = [_dot_acc(wp[rr:rr + 1, :n_pages], mb_o[h, rr]) for rr in range(tq)]
            o_fin = jnp.concatenate(rows, axis=0) + w_own * own_o[rs]
            o_ref[0, rs, :] = o_fin / l_fin
        o_ref[0, r_mb:, :] = sb_acc0[...]
        o_ref[1, :r_sb1, :] = sb_acc1[...]
        o_ref[1, r_sb1:, :] = fx_acc[...] / fx_l[...]


def _decode_attn(q_s, k_s, v_s, logf_s, cache_k, cache_v, cache_lfb, page_table, layer):
    nb, n_pages = page_table.shape
    d = q_s.shape[-1]
    tq = q_s.shape[0] // nb
    assert tq == HPG and N_HEADS == 2 * HPG and n_pages % 2 == 0 and n_pages <= LANES
    slot0 = H_MOBA + H_SB - HPG
    n_rows = HPG * tq
    qg = q_s.reshape(nb, tq, 2, HPG, HEAD_DIM).transpose(0, 2, 3, 1, 4).reshape(nb, 2, n_rows, HEAD_DIM)
    pad_tok = ((0, 0), (0, TOK_PER_BLK - tq), (0, 0), (0, 0))
    knew = jnp.pad(k_s.reshape(nb, tq, N_HEADS, HEAD_DIM), pad_tok)
    vnew = jnp.pad(v_s.reshape(nb, tq, N_HEADS, HEAD_DIM), pad_tok)
    lfnb = jnp.pad(logf_s[0, :, :H_FOX].reshape(nb, tq, H_FOX),
                   ((0, 0), (0, TOK_PER_BLK - tq), (slot0, 0))).reshape(nb, 1, LANES)
    kern = functools.partial(_decode_kernel, n_pages=n_pages, tq=tq)
    cache_idx = lambda b, j, pt: (layer, pt[b, n_pages - 1 - j], 0, 0, 0)
    lf_idx = lambda b, j, pt: (layer, pt[b, n_pages - 1 - j], 0, 0)
    r_mb = H_MOBA * tq
    r_sb1 = slot0 * tq
    vm = pltpu.VMEM
    out = pl.pallas_call(
        kern,
        grid_spec=pltpu.PrefetchScalarGridSpec(
            num_scalar_prefetch=1,
            grid=(nb, n_pages),
            in_specs=[
                pl.BlockSpec((None, 2, n_rows, HEAD_DIM), lambda b, j, pt: (b, 0, 0, 0)),
                pl.BlockSpec((None, TOK_PER_BLK, N_HEADS, HEAD_DIM), lambda b, j, pt: (b, 0, 0, 0)),
                pl.BlockSpec((None, TOK_PER_BLK, N_HEADS, HEAD_DIM), lambda b, j, pt: (b, 0, 0, 0)),
                pl.BlockSpec((None, tq, LANES), lambda b, j, pt: (0, b, 0)),
                pl.BlockSpec((None, 1, LANES), lambda b, j, pt: (b, 0, 0)),
                pl.BlockSpec((None, None, PAGE, N_HEADS, HEAD_DIM), cache_idx),
                pl.BlockSpec((None, None, PAGE, N_HEADS, HEAD_DIM), cache_idx),
                pl.BlockSpec((None, None, PAGE // TOK_PER_BLK, LANES), lf_idx),
            ],
            out_specs=pl.BlockSpec((None, 2, n_rows, HEAD_DIM), lambda b, j, pt: (b, 0, 0, 0)),
            scratch_shapes=[
                vm((H_MOBA, tq, LANES), F32), vm((H_MOBA, tq, LANES), F32), vm((H_MOBA, tq, LANES), F32),
                vm((H_MOBA, tq, n_pages, HEAD_DIM), F32),
                vm((r_mb, 1), F32), vm((r_mb, 1), F32), vm((r_mb, HEAD_DIM), F32),
                vm((n_rows - r_mb, HEAD_DIM), F32), vm((r_sb1, HEAD_DIM), F32),
                vm((n_rows - r_mb, LANES), F32), vm((r_sb1, LANES), F32),
                vm((n_rows - r_sb1, 1), F32), vm((n_rows - r_sb1, 1), F32), vm((n_rows - r_sb1, HEAD_DIM), F32),
                vm((1, LANES), F32), vm((n_rows, 1), F32),
            ],
        ),
        out_shape=jax.ShapeDtypeStruct((nb, 2, n_rows, HEAD_DIM), F32),
        compiler_params=_cparams("parallel", "arbitrary"),
        name="decode_attn",
    )(page_table, qg, knew, vnew, logf_s, lfnb, cache_k, cache_v, cache_lfb)
    return out.reshape(nb, 2, HPG, tq, HEAD_DIM).transpose(0, 3, 1, 2, 4).reshape(nb * tq, d)


def _merge_kernel(om_ref, os_ref, of_ref, gm_ref, gs_ref, gf_ref, x_ref, g1_ref,
                  wm_ref, ws_ref, wf_ref, wo_ref, o_ref, mrg_scr):
    j = pl.program_id(2)

    @pl.when(j == 0)
    def _():
        merged = (gm_ref[...] * _dot(om_ref[...].astype(BF16), wm_ref[...])
                  + gs_ref[...] * _dot(os_ref[...].astype(BF16), ws_ref[...])
                  + gf_ref[...] * _dot(of_ref[...].astype(BF16), wf_ref[...]))
        mrg_scr[...] = merged.astype(BF16)

    o_ref[...] = x_ref[...] + g1_ref[...] * _dot(mrg_scr[...], wo_ref[...])


def _merge(o_moba, o_sb, o_fox, proj, x, mod, w_m, w_s, w_f, w_o, *, tr):
    G, R, D = x.shape
    rs = mod.shape[1]
    tn = 1024
    per = D // tn
    rblk = 1 if rs == 1 else tr
    ridx = (lambda i: 0) if rs == 1 else (lambda i: i)
    full = lambda a: pl.BlockSpec(a.shape, lambda g_, i, j: (0, 0))
    return pl.pallas_call(
        _merge_kernel,
        grid=(G, R // tr, per),
        in_specs=[
            pl.BlockSpec((None, tr, o_moba.shape[-1]), lambda g_, i, j: (g_, i, 0)),
            pl.BlockSpec((None, tr, o_sb.shape[-1]), lambda g_, i, j: (g_, i, 0)),
            pl.BlockSpec((None, tr, o_fox.shape[-1]), lambda g_, i, j: (g_, i, 0)),
            pl.BlockSpec((None, None, tr, D), lambda g_, i, j: (1, g_, i, 0)),
            pl.BlockSpec((None, None, tr, D), lambda g_, i, j: (2, g_, i, 0)),
            pl.BlockSpec((None, None, tr, D), lambda g_, i, j: (3, g_, i, 0)),
            pl.BlockSpec((None, tr, tn), lambda g_, i, j: (g_, i, j)),
            pl.BlockSpec((None, rblk, tn), lambda g_, i, j: (g_, ridx(i), 2 * per + j)),
            full(w_m), full(w_s), full(w_f),
            pl.BlockSpec((D, tn), lambda g_, i, j: (0, j)),
        ],
        out_specs=pl.BlockSpec((None, tr, tn), lambda g_, i, j: (g_, i, j)),
        out_shape=jax.ShapeDtypeStruct((G, R, D), F32),
        scratch_shapes=[pltpu.VMEM((tr, D), BF16)],
        compiler_params=_cparams("parallel", "parallel", "arbitrary"),
        name="merge",
    )(o_moba, o_sb, o_fox, proj, proj, proj, x, mod, w_m, w_s, w_f, w_o)


def _norm_router_kernel(x_ref, sh_ref, sc_ref, g_ref, wr_ref, br_ref, h_ref, idx_ref, gate_ref):
    h = _rms_modulate(x_ref[...], g_ref[...], sc_ref[...], sh_ref[...])
    h_ref[...] = h
    tr = h.shape[0]
    lane = lax.broadcasted_iota(I32, (tr, LANES), 1)
    logits = _dot_acc(h, wr_ref[...]) + br_ref[...]
    vals, idxs = [], []
    for _ in range(TOP_K):
        m = jnp.max(logits, axis=-1, keepdims=True)
        idx = jnp.min(jnp.where(logits == m, lane, LANES), axis=-1, keepdims=True)
        vals.append(m)
        idxs.append(idx)
        logits = jnp.where(lane == idx, -jnp.inf, logits)
    es = [jnp.exp(v - vals[0]) for v in vals]
    denom = es[0] + es[1] + es[2] + es[3]
    idx_out = jnp.zeros((tr, LANES), I32)
    gate_out = jnp.zeros((tr, LANES), F32)
    for k in range(TOP_K):
        idx_out = jnp.where(lane == k, idxs[k], idx_out)
        gate_out = jnp.where(lane == k, es[k] / denom, gate_out)
    idx_ref[...] = idx_out
    gate_ref[...] = gate_out


def _norm_router(x, mod, g, w_r, b_r, *, tr):
    G, R, D = x.shape
    rs = mod.shape[1]
    rblk = 1 if rs == 1 else tr
    ridx = (lambda i: 0) if rs == 1 else (lambda i: i)
    return pl.pallas_call(
        _norm_router_kernel,
        grid=(G, R // tr),
        in_specs=[
            pl.BlockSpec((None, tr, D), lambda g_, i: (g_, i, 0)),
            pl.BlockSpec((None, rblk, D), lambda g_, i: (g_, ridx(i), 3)),
            pl.BlockSpec((None, rblk, D), lambda g_, i: (g_, ridx(i), 4)),
            pl.BlockSpec((1, D), lambda g_, i: (0, 0)),
            pl.BlockSpec((D, LANES), lambda g_, i: (0, 0)),
            pl.BlockSpec((1, LANES), lambda g_, i: (0, 0)),
        ],
        out_specs=[
            pl.BlockSpec((None, tr, D), lambda g_, i: (g_, i, 0)),
            pl.BlockSpec((None, tr, LANES), lambda g_, i: (g_, i, 0)),
            pl.BlockSpec((None, tr, LANES), lambda g_, i: (g_, i, 0)),
        ],
        out_shape=[
            jax.ShapeDtypeStruct((G, R, D), F32),
            jax.ShapeDtypeStruct((G, R, LANES), I32),
            jax.ShapeDtypeStruct((G, R, LANES), F32),
        ],
        compiler_params=_cparams("parallel", "parallel"),
        name="norm_router",
    )(x, mod, mod, g, w_r, b_r)


def _route(top_idx, tm):
    n_tok = top_idx.shape[0]
    n_assign = n_tok * TOP_K
    n_tiles = -(-(n_assign + N_EXPERTS * (tm - 1)) // tm)
    flat_e = top_idx.reshape(-1)
    onehot = (flat_e[:, None] == jnp.arange(N_EXPERTS, dtype=I32)[None, :]).astype(I32)
    sizes = jnp.sum(onehot, axis=0)
    rank = jnp.sum(jnp.cumsum(onehot, axis=0) * onehot, axis=1) - 1
    padded = (sizes + tm - 1) // tm * tm
    pad_end = jnp.cumsum(padded)
    pad_start = pad_end - padded
    dest = (pad_start[flat_e] + rank).astype(I32)
    slot_tok = jnp.zeros((n_tiles * tm,), I32).at[dest].set(jnp.arange(n_assign, dtype=I32) // TOP_K)
    tile_start = jnp.arange(n_tiles, dtype=I32) * tm
    tile_e = jnp.minimum(jnp.sum((pad_end[None, :] <= tile_start[:, None]).astype(I32), axis=1), N_EXPERTS - 1)
    n_used = (pad_end[-1] // tm).astype(I32).reshape(1)
    return dest, slot_tok, tile_e, n_used


def _gather_kernel(slot_tok_ref, n_used_ref, h_ref, o_ref, buf, sem, *, tm):
    t = pl.program_id(0)

    def row_copy(r):
        tok = slot_tok_ref[t * tm + r]
        return pltpu.make_async_copy(h_ref.at[pl.ds(tok, 1)], buf.at[pl.ds(r, 1)], sem)

    @pl.when(t < n_used_ref[0])
    def _():
        def start(r, c):
            row_copy(r).start()
            return c

        def wait(r, c):
            row_copy(r).wait()
            return c

        lax.fori_loop(0, tm, start, 0, unroll=8)
        lax.fori_loop(0, tm, wait, 0, unroll=8)
        o_ref[...] = buf[...].astype(o_ref.dtype)

    @pl.when(t >= n_used_ref[0])
    def _():
        o_ref[...] = jnp.zeros(o_ref.shape, o_ref.dtype)


def _gather_rows(h, slot_tok, n_used, tm):
    n_slots = slot_tok.shape[0]
    d = h.shape[1]
    return pl.pallas_call(
        functools.partial(_gather_kernel, tm=tm),
        grid_spec=pltpu.PrefetchScalarGridSpec(
            num_scalar_prefetch=2,
            grid=(n_slots // tm,),
            in_specs=[pl.BlockSpec(memory_space=pl.ANY)],
            out_specs=pl.BlockSpec((tm, d), lambda t, st, nu: (t, 0)),
            scratch_shapes=[pltpu.VMEM((tm, d), h.dtype), pltpu.SemaphoreType.DMA],
        ),
        out_shape=jax.ShapeDtypeStruct((n_slots, d), BF16),
        compiler_params=_cparams("arbitrary"),
        name="gather_rows",
    )(slot_tok, n_used, h)


def _expert_changed(te_ref, i):
    prev = te_ref[jnp.maximum(i - 1, 0)]
    return jnp.logical_or(i == 0, te_ref[i] != prev)


def _expert_gu_kernel(te_ref, nu_ref, x_ref, wg_ref, wu_ref, bg_ref, bu_ref, o_ref, wg_scr, wu_scr):
    i = pl.program_id(1)

    @pl.when(i < nu_ref[0])
    def _():
        @pl.when(_expert_changed(te_ref, i))
        def _():
            wg_scr[...] = wg_ref[...].astype(BF16)
            wu_scr[...] = wu_ref[...].astype(BF16)

        x = x_ref[...]
        g = _dot(x, wg_scr[...]) + bg_ref[...]
        u = _dot(x, wu_scr[...]) + bu_ref[...]
        g = jnp.minimum(g, SWIGLU_LIMIT)
        u = jnp.clip(u, -SWIGLU_LIMIT, SWIGLU_LIMIT)
        o_ref[...] = ((u + 1.0) * g * jax.nn.sigmoid(SWIGLU_ALPHA * g)).astype(BF16)

    @pl.when(i >= nu_ref[0])
    def _():
        o_ref[...] = jnp.zeros(o_ref.shape, o_ref.dtype)


def _expert_gu(xs, w_gu, b_gu, tile_e, n_used, tm):
    n_slots, d = xs.shape
    de = w_gu.shape[2] // 2
    tn = 1024
    nj = de // tn
    row = lambda j, i, te, nu: jnp.minimum(i, nu[0] - 1)
    exp = lambda j, i, te, nu: te[jnp.minimum(i, nu[0] - 1)]
    b3 = b_gu.reshape(w_gu.shape[0], 1, 2 * de)
    return pl.pallas_call(
        _expert_gu_kernel,
        grid_spec=pltpu.PrefetchScalarGridSpec(
            num_scalar_prefetch=2,
            grid=(nj, n_slots // tm),
            in_specs=[
                pl.BlockSpec((tm, d), lambda j, i, te, nu: (row(j, i, te, nu), 0)),
                pl.BlockSpec((None, d, tn), lambda j, i, te, nu: (exp(j, i, te, nu), 0, j)),
                pl.BlockSpec((None, d, tn), lambda j, i, te, nu: (exp(j, i, te, nu), 0, nj + j)),
                pl.BlockSpec((None, 1, tn), lambda j, i, te, nu: (exp(j, i, te, nu), 0, j)),
                pl.BlockSpec((None, 1, tn), lambda j, i, te, nu: (exp(j, i, te, nu), 0, nj + j)),
            ],
            out_specs=pl.BlockSpec((tm, tn), lambda j, i, te, nu: (i, j)),
            scratch_shapes=[pltpu.VMEM((d, tn), BF16), pltpu.VMEM((d, tn), BF16)],
        ),
        out_shape=jax.ShapeDtypeStruct((n_slots, de), BF16),
        compiler_params=_cparams("arbitrary", "arbitrary"),
        name="expert_gu",
    )(tile_e, n_used, xs, w_gu, w_gu, b3, b3)


def _expert_down_kernel(te_ref, nu_ref, a_ref, w_ref, b_ref, o_ref, w_scr):
    i = pl.program_id(1)

    @pl.when(i < nu_ref[0])
    def _():
        @pl.when(_expert_changed(te_ref, i))
        def _():
            w_scr[...] = w_ref[...].astype(BF16)

        o_ref[...] = _dot(a_ref[...], w_scr[...]) + b_ref[...]

    @pl.when(i >= nu_ref[0])
    def _():
        o_ref[...] = jnp.zeros(o_ref.shape, o_ref.dtype)


def _expert_down(act, w_down, b_down, tile_e, n_used, tm):
    n_slots, de = act.shape
    d = w_down.shape[2]
    tn = 1024
    nj = d // tn
    row = lambda j, i, te, nu: jnp.minimum(i, nu[0] - 1)
    exp = lambda j, i, te, nu: te[jnp.minimum(i, nu[0] - 1)]
    return pl.pallas_call(
        _expert_down_kernel,
        grid_spec=pltpu.PrefetchScalarGridSpec(
            num_scalar_prefetch=2,
            grid=(nj, n_slots // tm),
            in_specs=[
                pl.BlockSpec((tm, de), lambda j, i, te, nu: (row(j, i, te, nu), 0)),
                pl.BlockSpec((None, de, tn), lambda j, i, te, nu: (exp(j, i, te, nu), 0, j)),
                pl.BlockSpec((None, 1, tn), lambda j, i, te, nu: (exp(j, i, te, nu), 0, j)),
            ],
            out_specs=pl.BlockSpec((tm, tn), lambda j, i, te, nu: (i, j)),
            scratch_shapes=[pltpu.VMEM((de, tn), BF16)],
        ),
        out_shape=jax.ShapeDtypeStruct((n_slots, d), F32),
        compiler_params=_cparams("arbitrary", "arbitrary"),
        name="expert_down",
    )(tile_e, n_used, act, w_down, b_down.reshape(w_down.shape[0], 1, d))


def _combine_kernel(dest_ref, y_ref, gate_ref, x_ref, g2_ref, gfin_ref, o_ref, ybuf, sem, *, tt, tok0, final):
    g_ = pl.program_id(0)
    i = pl.program_id(1)
    base = (tok0 + (g_ * pl.num_programs(1) + i) * tt) * TOP_K

    def row_copy(r, k):
        slot = dest_ref[base + r * TOP_K + k]
        return pltpu.make_async_copy(y_ref.at[pl.ds(slot, 1)], ybuf.at[k, pl.ds(r, 1)], sem)

    def start(r, c):
        for k in range(TOP_K):
            row_copy(r, k).start()
        return c

    def wait(r, c):
        for k in range(TOP_K):
            row_copy(r, k).wait()
        return c

    lax.fori_loop(0, tt, start, 0)
    lax.fori_loop(0, tt, wait, 0)
    gates = gate_ref[...]
    moe = gates[:, 0:1] * ybuf[0]
    for k in range(1, TOP_K):
        moe = moe + gates[:, k:k + 1] * ybuf[k]
    x2 = x_ref[...] + g2_ref[...] * moe
    if final:
        x2 = x2 * lax.rsqrt(jnp.mean(x2 * x2, axis=-1, keepdims=True) + RMS_EPS) * gfin_ref[...]
    o_ref[...] = x2


def _combine(y, dest, gates, x1, mod, g_final, *, tt, tok0, final):
    G, R, D = x1.shape
    rs = mod.shape[1]
    rblk = 1 if rs == 1 else tt
    ridx = (lambda i: 0) if rs == 1 else (lambda i: i)
    kern = functools.partial(_combine_kernel, tt=tt, tok0=tok0, final=final)
    return pl.pallas_call(
        kern,
        grid_spec=pltpu.PrefetchScalarGridSpec(
            num_scalar_prefetch=1,
            grid=(G, R // tt),
            in_specs=[
                pl.BlockSpec(memory_space=pl.ANY),
                pl.BlockSpec((None, tt, LANES), lambda g_, i, d_: (g_, i, 0)),
                pl.BlockSpec((None, tt, D), lambda g_, i, d_: (g_, i, 0)),
                pl.BlockSpec((None, rblk, D), lambda g_, i, d_: (g_, ridx(i), 5)),
                pl.BlockSpec((1, D), lambda g_, i, d_: (0, 0)),
            ],
            out_specs=pl.BlockSpec((None, tt, D), lambda g_, i, d_: (g_, i, 0)),
            scratch_shapes=[pltpu.VMEM((TOP_K, tt, D), F32), pltpu.SemaphoreType.DMA],
        ),
        out_shape=jax.ShapeDtypeStruct((G, R, D), F32),
        compiler_params=_cparams("arbitrary", "arbitrary"),
        name="combine",
    )(dest, y, gates, x1, mod, g_final)


def _row_tile(rows, cap):
    return max(t for t in range(MOBA_BLOCK, cap + 1, MOBA_BLOCK) if rows % t == 0)


def _layer_weights(l, w_in, b_forget, w_br_moba, w_br_sb, w_br_fox, w_out, w_router, b_router):
    d = w_in.shape[1]
    w_l = w_in[l]
    w_main = jnp.concatenate([w_l[:, :3 * D_ATTN], w_l[:, 3 * D_ATTN + H_FOX:]], axis=1).astype(BF16)
    w_f = jnp.pad(w_l[:, 3 * D_ATTN:3 * D_ATTN + H_FOX], ((0, 0), (0, LANES - H_FOX)))
    b_f = jnp.pad(b_forget[l], (0, LANES - H_FOX)).reshape(1, LANES)
    w_r = jnp.pad(w_router[l], ((0, 0), (0, LANES - N_EXPERTS)))
    b_r = jnp.pad(b_router[l], (0, LANES - N_EXPERTS), constant_values=NEG_INF).reshape(1, LANES)
    return dict(w_main=w_main, w_f=w_f, b_f=b_f, w_r=w_r, b_r=b_r,
                w_m=w_br_moba[l].astype(BF16), w_s=w_br_sb[l].astype(BF16), w_x=w_br_fox[l].astype(BF16),
                w_o=w_out[l].astype(BF16))


def kernel(x_prompt, x_sample, c_prompt, c_sample, cache_k, cache_v, cache_log_f, page_table, w_mod, b_mod,
           g_attn, w_in, b_forget, w_br_moba, w_br_sb, w_br_fox, w_out, g_ffn, w_router, b_router, w_gu, b_gu,
           w_down, b_down, g_final):
    depth = w_mod.shape[0]
    B, T, D = x_prompt.shape
    nb, tq, _ = x_sample.shape
    n_pool = cache_k.shape[1]
    n_prompt = B * T
    n_samp = nb * tq
    tm = EXPERT_TILE

    clf = jnp.pad(cache_log_f, ((0, 0), (0, 0), (0, 0), (HPG - H_FOX, 0))).reshape(
        depth, n_pool, PAGE // TOK_PER_BLK, LANES)
    c_all = jnp.pad(jnp.concatenate([c_prompt, c_sample], axis=0), ((0, 16 - B - nb), (0, 0)))
    wgu = w_gu.reshape(depth * N_EXPERTS, D, w_gu.shape[-1])
    bgu = b_gu.reshape(depth * N_EXPERTS, b_gu.shape[-1])
    wdn = w_down.reshape(depth * N_EXPERTS, w_down.shape[2], D)
    bdn = b_down.reshape(depth * N_EXPERTS, D)
    g_fin = g_final.reshape(1, D)

    xp = x_prompt
    xs = x_sample.reshape(1, n_samp, D)
    kp = jnp.zeros((depth, B, T, D), F32)
    vp = jnp.zeros((depth, B, T, D), F32)
    ks = jnp.zeros((depth, 1, n_samp, D), F32)
    vs = jnp.zeros((depth, 1, n_samp, D), F32)
    fp, fs = [], []
    for l in range(depth):
        lw = _layer_weights(l, w_in, b_forget, w_br_moba, w_br_sb, w_br_fox, w_out, w_router, b_router)
        g_a = g_attn[l].reshape(1, D)
        g_f = g_ffn[l].reshape(1, D)
        mod = _mod(c_all, w_mod, b_mod, l)
        mod_p = mod[:B].reshape(B, 1, 6 * D)
        mod_s = jnp.repeat(mod[B:B + nb], tq, axis=0).reshape(1, n_samp, 6 * D)

        qg_p, kp, vp, lf_p = _ln_proj(xp, mod_p, g_a, lw["w_main"], lw["w_f"], lw["b_f"], kp, vp, l,
                                      tr=_row_tile(T, 1024))
        fc, fr = _fcum(lf_p)
        x1p = _merge(_moba(qg_p, kp, vp, l), _sb(qg_p, kp, vp, l), _fox(qg_p, kp, vp, fc, fr, l), qg_p, xp, mod_p,
                     lw["w_m"], lw["w_s"], lw["w_x"], lw["w_o"], tr=256)
        qg_s, ks, vs, lf_s = _ln_proj(xs, mod_s, g_a, lw["w_main"], lw["w_f"], lw["b_f"], ks, vs, l, tr=n_samp)
        o_s = _decode_attn(qg_s[0, 0], ks[l, 0], vs[l, 0], lf_s, cache_k, cache_v, clf, page_table, l)
        o_s = o_s.reshape(1, n_samp, D)
        n_m, n_s = H_MOBA * HEAD_DIM, H_SB * HEAD_DIM
        x1s = _merge(o_s[..., :n_m], o_s[..., n_m:n_m + n_s], o_s[..., n_m + n_s:], qg_s, xs, mod_s,
                     lw["w_m"], lw["w_s"], lw["w_x"], lw["w_o"], tr=n_samp)

        h2p, idx_p, gate_p = _norm_router(x1p, mod_p, g_f, lw["w_r"], lw["b_r"], tr=512)
        h2s, idx_s, gate_s = _norm_router(x1s, mod_s, g_f, lw["w_r"], lw["b_r"], tr=n_samp)
        h2 = jnp.concatenate([h2p.reshape(n_prompt, D), h2s.reshape(n_samp, D)], axis=0)
        top_idx = jnp.concatenate([idx_p.reshape(n_prompt, LANES)[:, :TOP_K],
                                   idx_s.reshape(n_samp, LANES)[:, :TOP_K]], axis=0)
        dest, slot_tok, tile_e, n_used = _route(top_idx, tm)
        rows = _gather_rows(h2, slot_tok, n_used, tm)
        act = _expert_gu(rows, wgu, bgu, tile_e + l * N_EXPERTS, n_used, tm)
        y = _expert_down(act, wdn, bdn, tile_e + l * N_EXPERTS, n_used, tm)
        final = l == depth - 1
        xp = _combine(y, dest, gate_p, x1p, mod_p, g_fin, tt=256, tok0=0, final=final)
        xs = _combine(y, dest, gate_s, x1s, mod_s, g_fin, tt=n_samp, tok0=n_prompt, final=final)

        fp.append(lf_p[..., :H_FOX])
        fs.append(lf_s[..., :H_FOX])

    hd = (N_HEADS, HEAD_DIM)
    return (xp, xs.reshape(nb, tq, D),
            kp.reshape(depth, B, T, *hd), vp.reshape(depth, B, T, *hd), jnp.stack(fp),
            ks.reshape(depth, nb, tq, *hd), vs.reshape(depth, nb, tq, *hd),
            jnp.stack(fs).reshape(depth, nb, tq, H_FOX))
```

```python
import functools

import jax
import jax.numpy as jnp
from jax import lax
from jax.experimental import pallas as pl
from jax.experimental.pallas import tpu as pltpu

F32 = jnp.float32
BF16 = jnp.bfloat16
I32 = jnp.int32

HEAD_DIM = 128
N_HEADS = 16
H_MOBA = 6
H_SB = 5
H_FOX = 5
D_ATTN = N_HEADS * HEAD_DIM
MOBA_BLOCK = 256
MOBA_TOPK = 3
N_EXPERTS = 32
TOP_K = 4
SWIGLU_ALPHA = 1.702
SWIGLU_LIMIT = 7.0
RMS_EPS = 1e-5
NEG_INF = -1e30
PAGE = 128
LANES = 128
VMEM_LIMIT_BYTES = 56 * 1024 * 1024
EXPERT_TILE = 256
MIXER_TQ = 128
MIXER_TK = 256


def _cparams(*sem):
    return pltpu.CompilerParams(dimension_semantics=sem, vmem_limit_bytes=VMEM_LIMIT_BYTES)


def _dot(a, b):
    return jnp.dot(a, b, preferred_element_type=F32)


def _dot_nt(a, b):
    return lax.dot_general(a, b, (((1,), (1,)), ((), ())), preferred_element_type=F32)


def _split2(x):
    hi = x.astype(BF16)
    lo = (x - hi.astype(F32)).astype(BF16)
    return hi, lo


def _split3(x):
    hi = x.astype(BF16)
    r = x - hi.astype(F32)
    mid = r.astype(BF16)
    lo = (r - mid.astype(F32)).astype(BF16)
    return hi, mid, lo


def _dot_acc(a, b):
    a_hi, a_lo = _split2(a)
    b_hi, b_lo = _split2(b)
    return _dot(a_hi, b_hi) + _dot(a_lo, b_hi) + _dot(a_hi, b_lo)


def _dot_nt_acc(a, b):
    a_hi, a_lo = _split2(a)
    b_hi, b_lo = _split2(b)
    return _dot_nt(a_hi, b_hi) + _dot_nt(a_lo, b_hi) + _dot_nt(a_hi, b_lo)


def _dot_exact_rhs(a, b_exact):
    hi, mid, lo = _split3(a)
    return _dot(hi, b_exact) + _dot(mid, b_exact) + _dot(lo, b_exact)


def _dot_exact_lhs(a_exact, b):
    hi, mid, lo = _split3(b)
    return _dot(a_exact, hi) + _dot(a_exact, mid) + _dot(a_exact, lo)


def _softplus_neg_abs(z):
    return jnp.log1p(jnp.exp(-jnp.abs(z)))


def _log_sigmoid(z):
    return -(jnp.maximum(-z, 0.0) + _softplus_neg_abs(z))


def _rms_modulate(x, g, scale, shift):
    y = x * lax.rsqrt(jnp.mean(x * x, axis=-1, keepdims=True) + RMS_EPS)
    return (y * g) * (1.0 + scale) + shift


def _mod_kernel(c_ref, w_ref, b_ref, o_ref):
    c = c_ref[...]
    o_ref[...] = _dot_acc(c * jax.nn.sigmoid(c), w_ref[...]) + b_ref[...]


def _mod(c, w, b, layer):
    m, d = c.shape
    depth, _, n = w.shape
    tn = 1024
    return pl.pallas_call(
        _mod_kernel,
        grid=(n // tn,),
        in_specs=[
            pl.BlockSpec((m, d), lambda j: (0, 0)),
            pl.BlockSpec((None, d, tn), lambda j: (layer, 0, j)),
            pl.BlockSpec((None, 1, tn), lambda j: (layer, 0, j)),
        ],
        out_specs=pl.BlockSpec((m, tn), lambda j: (0, j)),
        out_shape=jax.ShapeDtypeStruct((m, n), F32),
        compiler_params=_cparams("parallel"),
        name="mod",
    )(c, w, b.reshape(depth, 1, n))


def _ln_proj_kernel(x_ref, sh_ref, sc_ref, g_ref, w_ref, wf_ref, bf_ref, kprev_ref, vprev_ref,
                    qg_ref, k_ref, v_ref, lf_ref, h_scr, *, per):
    del kprev_ref, vprev_ref
    j = pl.program_id(2)

    @pl.when(j == 0)
    def _():
        h = _rms_modulate(x_ref[...], g_ref[...], sc_ref[...], sh_ref[...])
        h_scr[...] = h.astype(BF16)
        lf_ref[...] = _log_sigmoid(_dot_acc(h, wf_ref[...]) + bf_ref[...])

    acc = _dot(h_scr[...], w_ref[...])

    @pl.when(j < per)
    def _():
        qg_ref[...] = acc

    @pl.when(jnp.logical_and(j >= per, j < 2 * per))
    def _():
        k_ref[...] = acc

    @pl.when(jnp.logical_and(j >= 2 * per, j < 3 * per))
    def _():
        v_ref[...] = acc

    @pl.when(j >= 3 * per)
    def _():
        qg_ref[...] = jax.nn.sigmoid(acc)


def _ln_proj(x, mod, g, w_main, w_f, b_f, k_all, v_all, layer, *, tr):
    G, R, D = x.shape
    rs = mod.shape[1]
    tn = 512
    per = D // tn
    n_col = w_main.shape[1] // tn
    rblk = 1 if rs == 1 else tr
    ridx = (lambda i: 0) if rs == 1 else (lambda i: i)
    kern = functools.partial(_ln_proj_kernel, per=per)
    clamp = lambda j, lo: jnp.clip(j - lo, 0, per - 1)

    def qg_idx(g_, i, j):
        grp = jnp.where(j < 3 * per, 0, j // per - 2)
        col = jnp.where(j < per, j, jnp.where(j < 3 * per, per - 1, j % per))
        return (grp, g_, i, col)

    return pl.pallas_call(
        kern,
        grid=(G, R // tr, n_col),
        in_specs=[
            pl.BlockSpec((None, tr, D), lambda g_, i, j: (g_, i, 0)),
            pl.BlockSpec((None, rblk, D), lambda g_, i, j: (g_, ridx(i), 0)),
            pl.BlockSpec((None, rblk, D), lambda g_, i, j: (g_, ridx(i), 1)),
            pl.BlockSpec((1, D), lambda g_, i, j: (0, 0)),
            pl.BlockSpec((D, tn), lambda g_, i, j: (0, j)),
            pl.BlockSpec((D, LANES), lambda g_, i, j: (0, 0)),
            pl.BlockSpec((1, LANES), lambda g_, i, j: (0, 0)),
            pl.BlockSpec(memory_space=pl.ANY),
            pl.BlockSpec(memory_space=pl.ANY),
        ],
        out_specs=[
            pl.BlockSpec((None, None, tr, tn), qg_idx),
            pl.BlockSpec((None, None, tr, tn), lambda g_, i, j: (layer, g_, i, clamp(j, per))),
            pl.BlockSpec((None, None, tr, tn), lambda g_, i, j: (layer, g_, i, clamp(j, 2 * per))),
            pl.BlockSpec((None, tr, LANES), lambda g_, i, j: (g_, i, 0)),
        ],
        out_shape=[
            jax.ShapeDtypeStruct((n_col // per - 2, G, R, D), F32),
            jax.ShapeDtypeStruct(k_all.shape, F32),
            jax.ShapeDtypeStruct(v_all.shape, F32),
            jax.ShapeDtypeStruct((G, R, LANES), F32),
        ],
        input_output_aliases={7: 1, 8: 2},
        scratch_shapes=[pltpu.VMEM((tr, D), BF16)],
        compiler_params=_cparams("parallel", "parallel", "arbitrary"),
        name="ln_proj",
    )(x, mod, mod, g, w_main, w_f, b_f, k_all, v_all)


def _fcum_kernel(lf_ref, fc_ref, fr_ref):
    t = lf_ref.shape[0]
    r = lax.broadcasted_iota(I32, (t, t), 0)
    c = lax.broadcasted_iota(I32, (t, t), 1)
    tri = (r >= c).astype(BF16)
    f = _dot_exact_lhs(tri, lf_ref[...])
    fc_ref[...] = f
    fr_ref[...] = f.T[:8, :]


def _fcum(logf):
    B, T, _ = logf.shape
    return pl.pallas_call(
        _fcum_kernel,
        grid=(B,),
        in_specs=[pl.BlockSpec((None, T, LANES), lambda b: (b, 0, 0))],
        out_specs=[
            pl.BlockSpec((None, T, LANES), lambda b: (b, 0, 0)),
            pl.BlockSpec((None, 8, T), lambda b: (b, 0, 0)),
        ],
        out_shape=[jax.ShapeDtypeStruct((B, T, LANES), F32), jax.ShapeDtypeStruct((B, 8, T), F32)],
        compiler_params=_cparams("parallel"),
        name="fcum",
    )(logf)


def _head_specs(lead, head0, n_heads, rows, row_idx):
    return [
        pl.BlockSpec((None, None, rows, HEAD_DIM), lambda b, i, h=h: (lead, b, row_idx(i), head0 + h))
        for h in range(n_heads)
    ]


def _moba_kernel(*refs, n_heads, tq, n_kb):
    q_refs, k_refs, v_refs = refs[:n_heads], refs[n_heads:2 * n_heads], refs[2 * n_heads:3 * n_heads]
    o_ref, m_scr, l_scr, acc_scr, sel_scr = refs[3 * n_heads:]
    i = pl.program_id(1)
    own_blk = (i * tq) // MOBA_BLOCK
    scale = HEAD_DIM ** -0.5
    n_sel = min(MOBA_TOPK, n_kb)
    lane = lax.broadcasted_iota(I32, (tq, LANES), 1)
    qpos = i * tq + lax.broadcasted_iota(I32, (tq, MOBA_BLOCK), 0)
    koff = lax.broadcasted_iota(I32, (tq, MOBA_BLOCK), 1)
    for h in range(n_heads):
        q = q_refs[h][...]
        kmean = jnp.sum(k_refs[h][...].reshape(n_kb, MOBA_BLOCK, HEAD_DIM), axis=1) * (1.0 / MOBA_BLOCK)
        kmean = jnp.concatenate([kmean, jnp.zeros((LANES - n_kb, HEAD_DIM), F32)], axis=0)
        gate = jnp.where(lane < own_blk, _dot_nt_acc(q, kmean), NEG_INF)
        sel = jnp.zeros((tq, LANES), jnp.bool_)
        for _ in range(n_sel):
            m = jnp.max(gate, axis=-1, keepdims=True)
            idx = jnp.min(jnp.where(gate == m, lane, LANES), axis=-1, keepdims=True)
            hit = lane == idx
            sel = jnp.logical_or(sel, hit)
            gate = jnp.where(hit, -jnp.inf, gate)
        sel_scr[h] = jnp.where(jnp.logical_and(sel, lane < own_blk), 1.0, 0.0)
        m_scr[h] = jnp.full((tq, 1), NEG_INF, F32)
        l_scr[h] = jnp.zeros((tq, 1), F32)
        acc_scr[h] = jnp.zeros((tq, HEAD_DIM), F32)

    def body(n, carry):
        start = pl.multiple_of(n * MOBA_BLOCK, MOBA_BLOCK)
        kpos = n * MOBA_BLOCK + koff
        dist = (qpos - kpos).astype(F32)
        own = (koff * 0 + n) == own_blk
        own_visible = jnp.logical_and(own, kpos <= qpos)
        not_own = jnp.logical_not(own)
        for h in range(n_heads):
            slope = 2.0 ** (-8.0 * (h + 1) / n_heads)
            qb = q_refs[h][...].astype(BF16)
            kn = k_refs[h][pl.ds(start, MOBA_BLOCK), :].astype(BF16)
            vn = v_refs[h][pl.ds(start, MOBA_BLOCK), :].astype(BF16)
            s = _dot_nt(qb, kn) * scale - slope * dist
            sel_n = jnp.sum(jnp.where(lane == n, sel_scr[h], 0.0), axis=-1, keepdims=True) > 0.5
            valid = jnp.logical_or(own_visible, jnp.logical_and(not_own, sel_n))
            s = jnp.where(valid, s, NEG_INF)
            m_run = m_scr[h]
            m_new = jnp.maximum(m_run, jnp.max(s, axis=-1, keepdims=True))
            alpha = jnp.exp(m_run - m_new)
            p = jnp.where(valid, jnp.exp(s - m_new), 0.0)
            l_scr[h] = alpha * l_scr[h] + jnp.sum(p, axis=-1, keepdims=True)
            acc_scr[h] = alpha * acc_scr[h] + _dot(p.astype(BF16), vn)
            m_scr[h] = m_new
        return carry

    lax.fori_loop(0, own_blk + 1, body, 0)
    for h in range(n_heads):
        o_ref[:, h * HEAD_DIM:(h + 1) * HEAD_DIM] = acc_scr[h] / l_scr[h]


def _mixer_scratch(n_heads, tq, n_col, n_wide):
    return ([pltpu.VMEM((n_heads, tq, 1), F32)] * n_col
            + [pltpu.VMEM((n_heads, tq, HEAD_DIM), F32)] * n_wide)


def _moba(qg, k_all, v_all, layer):
    _, B, T, _ = qg.shape
    tq = MIXER_TQ
    n_kb = T // MOBA_BLOCK
    kern = functools.partial(_moba_kernel, n_heads=H_MOBA, tq=tq, n_kb=n_kb)
    specs = (_head_specs(0, 0, H_MOBA, tq, lambda i: i) + _head_specs(layer, 0, H_MOBA, T, lambda i: 0)
             + _head_specs(layer, 0, H_MOBA, T, lambda i: 0))
    return pl.pallas_call(
        kern,
        grid=(B, T // tq),
        in_specs=specs,
        out_specs=pl.BlockSpec((None, tq, H_MOBA * HEAD_DIM), lambda b, i: (b, i, 0)),
        out_shape=jax.ShapeDtypeStruct((B, T, H_MOBA * HEAD_DIM), F32),
        scratch_shapes=_mixer_scratch(H_MOBA, tq, 2, 2),
        compiler_params=_cparams("parallel", "arbitrary"),
        name="moba",
    )(*([qg] * H_MOBA + [k_all] * H_MOBA + [v_all] * H_MOBA))


def _strict_upper(n):
    j = lax.broadcasted_iota(I32, (n, n), 0)
    s = lax.broadcasted_iota(I32, (n, n), 1)
    return (j > s).astype(BF16)


def _sb_kernel(*refs, n_heads, tq, tk):
    q_refs, k_refs, v_refs = refs[:n_heads], refs[n_heads:2 * n_heads], refs[2 * n_heads:3 * n_heads]
    o_ref, c_scr, acc_scr = refs[3 * n_heads:]
    i = pl.program_id(1)
    scale = HEAD_DIM ** -0.5
    qpos = i * tq + lax.broadcasted_iota(I32, (tq, tk), 0)
    koff = lax.broadcasted_iota(I32, (tq, tk), 1)
    after = _strict_upper(tk)
    c_scr[...] = jnp.zeros(c_scr.shape, F32)
    acc_scr[...] = jnp.zeros(acc_scr.shape, F32)

    last = (i * tq) // tk

    def body(t, carry):
        n = last - t
        start = pl.multiple_of(n * tk, tk)
        strict = (n * tk + koff) < qpos
        for h in range(n_heads):
            qb = q_refs[h][...].astype(BF16)
            kn = k_refs[h][pl.ds(start, tk), :].astype(BF16)
            vn = v_refs[h][pl.ds(start, tk), :].astype(BF16)
            z = _dot_nt(qb, kn) * scale
            sp = _softplus_neg_abs(z)
            log_beta = -(jnp.maximum(-z, 0.0) + sp)
            log_keep = jnp.where(strict, -(jnp.maximum(z, 0.0) + sp), 0.0)
            c_run = c_scr[h]
            between = _dot_exact_rhs(log_keep, after) + c_run
            w = jnp.where(strict, jnp.exp(log_beta + between), 0.0)
            acc_scr[h] = acc_scr[h] + _dot(w.astype(BF16), vn)
            c_scr[h] = c_run + jnp.sum(log_keep, axis=-1, keepdims=True)
        return carry

    lax.fori_loop(0, last + 1, body, 0)
    for h in range(n_heads):
        o_ref[:, h * HEAD_DIM:(h + 1) * HEAD_DIM] = acc_scr[h]


def _sb(qg, k_all, v_all, layer):
    _, B, T, _ = qg.shape
    tq, tk = MIXER_TQ, MIXER_TK
    kern = functools.partial(_sb_kernel, n_heads=H_SB, tq=tq, tk=tk)
    specs = (_head_specs(0, H_MOBA, H_SB, tq, lambda i: i) + _head_specs(layer, H_MOBA, H_SB, T, lambda i: 0)
             + _head_specs(layer, H_MOBA, H_SB, T, lambda i: 0))
    return pl.pallas_call(
        kern,
        grid=(B, T // tq),
        in_specs=specs,
        out_specs=pl.BlockSpec((None, tq, H_SB * HEAD_DIM), lambda b, i: (b, i, 0)),
        out_shape=jax.ShapeDtypeStruct((B, T, H_SB * HEAD_DIM), F32),
        scratch_shapes=_mixer_scratch(H_SB, tq, 1, 1),
        compiler_params=_cparams("parallel", "arbitrary"),
        name="sb",
    )(*([qg] * H_SB + [k_all] * H_SB + [v_all] * H_SB))


def _fox_kernel(*refs, n_heads, tq, tk):
    q_refs, k_refs, v_refs = refs[:n_heads], refs[n_heads:2 * n_heads], refs[2 * n_heads:3 * n_heads]
    fc_ref, fr_ref, o_ref, m_scr, l_scr, acc_scr = refs[3 * n_heads:]
    i = pl.program_id(1)
    scale = HEAD_DIM ** -0.5
    qpos = i * tq + lax.broadcasted_iota(I32, (tq, tk), 0)
    koff = lax.broadcasted_iota(I32, (tq, tk), 1)
    m_scr[...] = jnp.full(m_scr.shape, NEG_INF, F32)
    l_scr[...] = jnp.zeros(l_scr.shape, F32)
    acc_scr[...] = jnp.zeros(acc_scr.shape, F32)

    def body(n, carry):
        start = pl.multiple_of(n * tk, tk)
        valid = (n * tk + koff) <= qpos
        for h in range(n_heads):
            qb = q_refs[h][...].astype(BF16)
            kn = k_refs[h][pl.ds(start, tk), :].astype(BF16)
            vn = v_refs[h][pl.ds(start, tk), :].astype(BF16)
            fq = fc_ref[:, h:h + 1]
            fk = fr_ref[h:h + 1, pl.ds(start, tk)]
            s = jnp.where(valid, _dot_nt(qb, kn) * scale + (fq - fk), NEG_INF)
            m_run = m_scr[h]
            m_new = jnp.maximum(m_run, jnp.max(s, axis=-1, keepdims=True))
            alpha = jnp.exp(m_run - m_new)
            p = jnp.where(valid, jnp.exp(s - m_new), 0.0)
            l_scr[h] = alpha * l_scr[h] + jnp.sum(p, axis=-1, keepdims=True)
            acc_scr[h] = alpha * acc_scr[h] + _dot(p.astype(BF16), vn)
            m_scr[h] = m_new
        return carry

    lax.fori_loop(0, (i * tq) // tk + 1, body, 0)
    for h in range(n_heads):
        o_ref[:, h * HEAD_DIM:(h + 1) * HEAD_DIM] = acc_scr[h] / l_scr[h]


def _fox(qg, k_all, v_all, fc, fr, layer):
    _, B, T, _ = qg.shape
    tq, tk = MIXER_TQ, MIXER_TK
    h0 = H_MOBA + H_SB
    kern = functools.partial(_fox_kernel, n_heads=H_FOX, tq=tq, tk=tk)
    specs = (_head_specs(0, h0, H_FOX, tq, lambda i: i) + _head_specs(layer, h0, H_FOX, T, lambda i: 0)
             + _head_specs(layer, h0, H_FOX, T, lambda i: 0)
             + [pl.BlockSpec((None, tq, LANES), lambda b, i: (b, i, 0)),
                pl.BlockSpec((None, 8, T), lambda b, i: (b, 0, 0))])
    return pl.pallas_call(
        kern,
        grid=(B, T // tq),
        in_specs=specs,
        out_specs=pl.BlockSpec((None, tq, H_FOX * HEAD_DIM), lambda b, i: (b, i, 0)),
        out_shape=jax.ShapeDtypeStruct((B, T, H_FOX * HEAD_DIM), F32),
        scratch_shapes=_mixer_scratch(H_FOX, tq, 2, 1),
        compiler_params=_cparams("parallel", "arbitrary"),
        name="fox",
    )(*([qg] * H_FOX + [k_all] * H_FOX + [v_all] * H_FOX), fc, fr)


HPG = 8
TOK_PER_BLK = LANES // HPG


def _lane_class_mats():
    j = lax.broadcasted_iota(I32, (LANES, LANES), 0)
    s = lax.broadcasted_iota(I32, (LANES, LANES), 1)
    same = (j & 7) == (s & 7)
    after = jnp.logical_and(same, (j >> 3) > (s >> 3)).astype(BF16)
    return after, same.astype(BF16)


def _scan_blocked(xb, rows, nblk, carry, after, same):
    within = _dot_exact_rhs(xb, after)
    tot = _dot_exact_rhs(xb, same)
    out = [None] * nblk
    run = carry
    for b in range(nblk - 1, -1, -1):
        out[b] = within[b * rows:(b + 1) * rows] + run
        run = run + tot[b * rows:(b + 1) * rows]
    return out, run


def _decode_kernel(pt_ref, q_ref, kn_ref, vn_ref, lfn_ref, lfnb_ref, kb_ref, vb_ref, lfb_ref,
                   ka_ref, va_ref, lfa_ref, o_ref,
                   mb_m, mb_l, mb_g, mb_o, own_m, own_l, own_o, sb_acc0, sb_acc1, sb_c0, sb_c1,
                   fx_m, fx_l, fx_acc, fx_c, a_col, *, n_pages, tq):
    j = pl.program_id(1)
    page_b = n_pages - 1 - 2 * j
    t_past = n_pages * PAGE
    scale = HEAD_DIM ** -0.5
    n_rows = HPG * tq
    r_mb = H_MOBA * tq
    r_sb1 = (H_MOBA + H_SB - HPG) * tq
    after, same = _lane_class_mats()
    lane1 = lax.broadcasted_iota(I32, (tq, LANES), 1)

    def lane_blocks(x):
        return [x[:, b * LANES:(b + 1) * LANES] for b in range(x.shape[1] // LANES)]

    def process(k_ref, v_ref, lfblk, page, is_new):
        t_tok = k_ref.shape[0]
        nblk = t_tok // TOK_PER_BLK
        w = t_tok * HPG
        lane = lax.broadcasted_iota(I32, (n_rows, w), 1)
        row = lax.broadcasted_iota(I32, (n_rows, w), 0)
        tpos = lane >> 3
        r = row & 7
        mine = (lane & 7) == (row >> 3)
        if is_new:
            vis = jnp.logical_and(mine, tpos <= r)
            vis_strict = jnp.logical_and(mine, tpos < r)
            dist = (r - tpos).astype(F32)
        else:
            vis = vis_strict = mine
            dist = (t_past + r - (page * PAGE + tpos)).astype(F32)
        rowc = lax.broadcasted_iota(I32, (n_rows, 1), 0) >> 3
        slope = jnp.zeros((n_rows, 1), F32)
        for h in range(H_MOBA):
            slope = jnp.where(rowc == h, 2.0 ** (-8.0 * (h + 1) / H_MOBA), slope)

        s_g, v_g = [], []
        for g in range(2):
            kg = k_ref[:, g * HPG:(g + 1) * HPG, :].reshape(w, HEAD_DIM).astype(BF16)
            v_g.append(v_ref[:, g * HPG:(g + 1) * HPG, :].reshape(w, HEAD_DIM).astype(BF16))
            s_g.append(_dot_nt(q_ref[g].astype(BF16), kg) * scale)

        def sb_weights(z, visible, c_ref):
            rows = z.shape[0]
            sp = _softplus_neg_abs(z)
            log_beta = -(jnp.maximum(-z, 0.0) + sp)
            log_keep = jnp.where(visible, -(jnp.maximum(z, 0.0) + sp), 0.0)
            between, c_new = _scan_blocked(jnp.concatenate(lane_blocks(log_keep), axis=0), rows, nblk,
                                           c_ref[...], after, same)
            c_ref[...] = c_new
            ws = [jnp.where(vb, jnp.exp(lb + bt), 0.0)
                  for vb, lb, bt in zip(lane_blocks(visible), lane_blocks(log_beta), between)]
            return jnp.concatenate(ws, axis=1)

        s = jnp.where(vis[:r_mb], s_g[0][:r_mb] - slope[:r_mb] * dist[:r_mb], NEG_INF)
        m = jnp.max(s, axis=-1, keepdims=True)
        p_mb = jnp.exp(s - m)
        l = jnp.sum(p_mb, axis=-1, keepdims=True)
        w_sb0 = sb_weights(s_g[0][r_mb:], vis_strict[r_mb:], sb_c0)
        pv0 = _dot(jnp.concatenate([p_mb, w_sb0], axis=0).astype(BF16), v_g[0])
        sb_acc0[...] = sb_acc0[...] + pv0[r_mb:]
        if is_new:
            own_m[...] = m
            own_l[...] = l
            own_o[...] = pv0[:r_mb]
        else:
            ksum = jnp.sum(k_ref[:, 0:HPG, :], axis=0)
            ksum_rows = jnp.concatenate(
                [jnp.broadcast_to(ksum[h:h + 1, :], (tq, HEAD_DIM)) for h in range(H_MOBA)], axis=0)
            g_col = jnp.sum(q_ref[0][:r_mb] * ksum_rows, axis=-1, keepdims=True)
            here = lane1 == page
            for h in range(H_MOBA):
                rs = slice(h * tq, (h + 1) * tq)
                mb_m[h] = jnp.where(here, m[rs], mb_m[h])
                mb_l[h] = jnp.where(here, l[rs], mb_l[h])
                mb_g[h] = jnp.where(here, g_col[rs], mb_g[h])
                for rr in range(tq):
                    mb_o[h, rr, pl.ds(page, 1), :] = pv0[h * tq + rr:h * tq + rr + 1, :]

        w_sb1 = sb_weights(s_g[1][:r_sb1], vis_strict[:r_sb1], sb_c1)
        bias_blk, fc_new = _scan_blocked(lfblk, 1, nblk, fx_c[...], after, same)
        fx_c[...] = fc_new
        n_fx = n_rows - r_sb1
        bias = jnp.concatenate([jnp.broadcast_to(bb, (n_fx, LANES)) for bb in bias_blk], axis=1)
        s = jnp.where(vis[r_sb1:], s_g[1][r_sb1:] + bias + a_col[r_sb1:], NEG_INF)
        m_run = fx_m[...]
        m_new = jnp.maximum(m_run, jnp.max(s, axis=-1, keepdims=True))
        alpha = jnp.exp(m_run - m_new)
        p_fx = jnp.exp(s - m_new)
        fx_l[...] = alpha * fx_l[...] + jnp.sum(p_fx, axis=-1, keepdims=True)
        pv1 = _dot(jnp.concatenate([w_sb1, p_fx], axis=0).astype(BF16), v_g[1])
        sb_acc1[...] = sb_acc1[...] + pv1[:r_sb1]
        fx_acc[...] = alpha * fx_acc[...] + pv1[r_sb1:]
        fx_m[...] = m_new

    @pl.when(j == 0)
    def _():
        mb_m[...] = jnp.full(mb_m.shape, NEG_INF, F32)
        mb_l[...] = jnp.zeros(mb_l.shape, F32)
        mb_g[...] = jnp.full(mb_g.shape, NEG_INF, F32)
        mb_o[...] = jnp.zeros(mb_o.shape, F32)
        sb_acc0[...] = jnp.zeros(sb_acc0.shape, F32)
        sb_acc1[...] = jnp.zeros(sb_acc1.shape, F32)
        sb_c0[...] = jnp.zeros(sb_c0.shape, F32)
        sb_c1[...] = jnp.zeros(sb_c1.shape, F32)
        fx_m[...] = jnp.full(fx_m.shape, NEG_INF, F32)
        fx_l[...] = jnp.zeros(fx_l.shape, F32)
        fx_acc[...] = jnp.zeros(fx_acc.shape, F32)
        lfn = lfn_ref[...]
        row1 = lax.broadcasted_iota(I32, (tq, LANES), 0)
        a_new = jnp.zeros((tq, LANES), F32)
        for rr in range(tq):
            a_new = a_new + jnp.where(row1 >= rr, lfn[rr:rr + 1, :], 0.0)
        a_col[...] = jnp.concatenate([jnp.zeros((r_sb1, 1), F32)]
                                     + [a_new[:, hh:hh + 1] for hh in range(H_FOX)], axis=0)
        lfnb = lfnb_ref[...]
        fx_c[...] = -_dot_exact_rhs(lfnb, same)
        process(kn_ref, vn_ref, lfnb, None, True)

    process(kb_ref, vb_ref, lfb_ref[...], page_b, False)
    process(ka_ref, va_ref, lfa_ref[...], page_b - 1, False)

    @pl.when(j == n_pages // 2 - 1)
    def _():
        pj = lax.broadcasted_iota(I32, (LANES, LANES), 0)
        ps = lax.broadcasted_iota(I32, (LANES, LANES), 1)
        pair = ((pj >> 1) == (ps >> 1)).astype(BF16)
        n_kb = n_pages * PAGE // MOBA_BLOCK + 1
        for h in range(H_MOBA):
            rs = slice(h * tq, (h + 1) * tq)
            gate = jnp.where(lane1 < n_pages, _dot_exact_rhs(mb_g[h], pair), NEG_INF)
            sel = jnp.zeros((tq, LANES), jnp.bool_)
            for _ in range(min(MOBA_TOPK, n_kb)):
                m = jnp.max(gate, axis=-1, keepdims=True)
                idx = jnp.min(jnp.where(gate == m, lane1, LANES), axis=-1, keepdims=True)
                hit = (lane1 >> 1) == (idx >> 1)
                sel = jnp.logical_or(sel, hit)
                gate = jnp.where(hit, -jnp.inf, gate)
            sel = jnp.logical_and(sel, lane1 < n_pages)
            m_pg = mb_m[h]
            m_own = own_m[rs]
            m_fin = jnp.maximum(jnp.max(jnp.where(sel, m_pg, NEG_INF), axis=-1, keepdims=True), m_own)
            wp = jnp.where(sel, jnp.exp(m_pg - m_fin), 0.0)
            w_own = jnp.exp(m_own - m_fin)
            l_fin = jnp.sum(wp * mb_l[h], axis=-1, keepdims=True) + w_own * own_l[rs]
            rows = [_dot_acc(wp[rr:rr + 1, :n_pages], mb_o[h, rr]) for rr in range(tq)]
            o_fin = jnp.concatenate(rows, axis=0) + w_own * own_o[rs]
            o_ref[0, rs, :] = o_fin / l_fin
        o_ref[0, r_mb:, :] = sb_acc0[...]
        o_ref[1, :r_sb1, :] = sb_acc1[...]
        o_ref[1, r_sb1:, :] = fx_acc[...] / fx_l[...]


def _decode_attn(q_s, k_s, v_s, logf_s, cache_k, cache_v, cache_lfb, page_table, layer):
    nb, n_pages = page_table.shape
    d = q_s.shape[-1]
    tq = q_s.shape[0] // nb
    assert tq == HPG and N_HEADS == 2 * HPG and n_pages % 2 == 0 and n_pages <= LANES
    slot0 = H_MOBA + H_SB - HPG
    n_rows = HPG * tq
    qg = q_s.reshape(nb, tq, 2, HPG, HEAD_DIM).transpose(0, 2, 3, 1, 4).reshape(nb, 2, n_rows, HEAD_DIM)
    pad_tok = ((0, 0), (0, TOK_PER_BLK - tq), (0, 0), (0, 0))
    knew = jnp.pad(k_s.reshape(nb, tq, N_HEADS, HEAD_DIM), pad_tok)
    vnew = jnp.pad(v_s.reshape(nb, tq, N_HEADS, HEAD_DIM), pad_tok)
    lfnb = jnp.pad(logf_s[0, :, :H_FOX].reshape(nb, tq, H_FOX),
                   ((0, 0), (0, TOK_PER_BLK - tq), (slot0, 0))).reshape(nb, 1, LANES)
    kern = functools.partial(_decode_kernel, n_pages=n_pages, tq=tq)
    page_spec = lambda back: pl.BlockSpec(
        (None, None, PAGE, N_HEADS, HEAD_DIM), lambda b, j, pt: (layer, pt[b, n_pages - 1 - 2 * j - back], 0, 0, 0))
    lf_spec = lambda back: pl.BlockSpec(
        (None, None, PAGE // TOK_PER_BLK, LANES), lambda b, j, pt: (layer, pt[b, n_pages - 1 - 2 * j - back], 0, 0))
    r_mb = H_MOBA * tq
    r_sb1 = slot0 * tq
    vm = pltpu.VMEM
    out = pl.pallas_call(
        kern,
        grid_spec=pltpu.PrefetchScalarGridSpec(
            num_scalar_prefetch=1,
            grid=(nb, n_pages // 2),
            in_specs=[
                pl.BlockSpec((None, 2, n_rows, HEAD_DIM), lambda b, j, pt: (b, 0, 0, 0)),
                pl.BlockSpec((None, TOK_PER_BLK, N_HEADS, HEAD_DIM), lambda b, j, pt: (b, 0, 0, 0)),
                pl.BlockSpec((None, TOK_PER_BLK, N_HEADS, HEAD_DIM), lambda b, j, pt: (b, 0, 0, 0)),
                pl.BlockSpec((None, tq, LANES), lambda b, j, pt: (0, b, 0)),
                pl.BlockSpec((None, 1, LANES), lambda b, j, pt: (b, 0, 0)),
                page_spec(0), page_spec(0), lf_spec(0),
                page_spec(1), page_spec(1), lf_spec(1),
            ],
            out_specs=pl.BlockSpec((None, 2, n_rows, HEAD_DIM), lambda b, j, pt: (b, 0, 0, 0)),
            scratch_shapes=[
                vm((H_MOBA, tq, LANES), F32), vm((H_MOBA, tq, LANES), F32), vm((H_MOBA, tq, LANES), F32),
                vm((H_MOBA, tq, n_pages, HEAD_DIM), F32),
                vm((r_mb, 1), F32), vm((r_mb, 1), F32), vm((r_mb, HEAD_DIM), F32),
                vm((n_rows - r_mb, HEAD_DIM), F32), vm((r_sb1, HEAD_DIM), F32),
                vm((n_rows - r_mb, LANES), F32), vm((r_sb1, LANES), F32),
                vm((n_rows - r_sb1, 1), F32), vm((n_rows - r_sb1, 1), F32), vm((n_rows - r_sb1, HEAD_DIM), F32),
                vm((1, LANES), F32), vm((n_rows, 1), F32),
            ],
        ),
        out_shape=jax.ShapeDtypeStruct((nb, 2, n_rows, HEAD_DIM), F32),
        compiler_params=_cparams("parallel", "arbitrary"),
        name="decode_attn",
    )(page_table, qg, knew, vnew, logf_s, lfnb, cache_k, cache_v, cache_lfb, cache_k, cache_v, cache_lfb)
    return out.reshape(nb, 2, HPG, tq, HEAD_DIM).transpose(0, 3, 1, 2, 4).reshape(nb * tq, d)


def _merge_kernel(om_ref, os_ref, of_ref, gm_ref, gs_ref, gf_ref, x_ref, g1_ref,
                  wm_ref, ws_ref, wf_ref, wo_ref, o_ref, mrg_scr):
    j = pl.program_id(2)

    @pl.when(j == 0)
    def _():
        merged = (gm_ref[...] * _dot(om_ref[...].astype(BF16), wm_ref[...])
                  + gs_ref[...] * _dot(os_ref[...].astype(BF16), ws_ref[...])
                  + gf_ref[...] * _dot(of_ref[...].astype(BF16), wf_ref[...]))
        mrg_scr[...] = merged.astype(BF16)

    o_ref[...] = x_ref[...] + g1_ref[...] * _dot(mrg_scr[...], wo_ref[...])


def _merge(o_moba, o_sb, o_fox, proj, x, mod, w_m, w_s, w_f, w_o, *, tr):
    G, R, D = x.shape
    rs = mod.shape[1]
    tn = 1024
    per = D // tn
    rblk = 1 if rs == 1 else tr
    ridx = (lambda i: 0) if rs == 1 else (lambda i: i)
    full = lambda a: pl.BlockSpec(a.shape, lambda g_, i, j: (0, 0))
    return pl.pallas_call(
        _merge_kernel,
        grid=(G, R // tr, per),
        in_specs=[
            pl.BlockSpec((None, tr, o_moba.shape[-1]), lambda g_, i, j: (g_, i, 0)),
            pl.BlockSpec((None, tr, o_sb.shape[-1]), lambda g_, i, j: (g_, i, 0)),
            pl.BlockSpec((None, tr, o_fox.shape[-1]), lambda g_, i, j: (g_, i, 0)),
            pl.BlockSpec((None, None, tr, D), lambda g_, i, j: (1, g_, i, 0)),
            pl.BlockSpec((None, None, tr, D), lambda g_, i, j: (2, g_, i, 0)),
            pl.BlockSpec((None, None, tr, D), lambda g_, i, j: (3, g_, i, 0)),
            pl.BlockSpec((None, tr, tn), lambda g_, i, j: (g_, i, j)),
            pl.BlockSpec((None, rblk, tn), lambda g_, i, j: (g_, ridx(i), 2 * per + j)),
            full(w_m), full(w_s), full(w_f),
            pl.BlockSpec((D, tn), lambda g_, i, j: (0, j)),
        ],
        out_specs=pl.BlockSpec((None, tr, tn), lambda g_, i, j: (g_, i, j)),
        out_shape=jax.ShapeDtypeStruct((G, R, D), F32),
        scratch_shapes=[pltpu.VMEM((tr, D), BF16)],
        compiler_params=_cparams("parallel", "parallel", "arbitrary"),
        name="merge",
    )(o_moba, o_sb, o_fox, proj, proj, proj, x, mod, w_m, w_s, w_f, w_o)


def _norm_router_kernel(x_ref, sh_ref, sc_ref, g_ref, wr_ref, br_ref, h_ref, idx_ref, gate_ref):
    h = _rms_modulate(x_ref[...], g_ref[...], sc_ref[...], sh_ref[...])
    h_ref[...] = h
    tr = h.shape[0]
    lane = lax.broadcasted_iota(I32, (tr, LANES), 1)
    logits = _dot_acc(h, wr_ref[...]) + br_ref[...]
    vals, idxs = [], []
    for _ in range(TOP_K):
        m = jnp.max(logits, axis=-1, keepdims=True)
        idx = jnp.min(jnp.where(logits == m, lane, LANES), axis=-1, keepdims=True)
        vals.append(m)
        idxs.append(idx)
        logits = jnp.where(lane == idx, -jnp.inf, logits)
    es = [jnp.exp(v - vals[0]) for v in vals]
    denom = es[0] + es[1] + es[2] + es[3]
    idx_out = jnp.zeros((tr, LANES), I32)
    gate_out = jnp.zeros((tr, LANES), F32)
    for k in range(TOP_K):
        idx_out = jnp.where(lane == k, idxs[k], idx_out)
        gate_out = jnp.where(lane == k, es[k] / denom, gate_out)
    idx_ref[...] = idx_out
    gate_ref[...] = gate_out


def _norm_router(x, mod, g, w_r, b_r, *, tr):
    G, R, D = x.shape
    rs = mod.shape[1]
    rblk = 1 if rs == 1 else tr
    ridx = (lambda i: 0) if rs == 1 else (lambda i: i)
    return pl.pallas_call(
        _norm_router_kernel,
        grid=(G, R // tr),
        in_specs=[
            pl.BlockSpec((None, tr, D), lambda g_, i: (g_, i, 0)),
            pl.BlockSpec((None, rblk, D), lambda g_, i: (g_, ridx(i), 3)),
            pl.BlockSpec((None, rblk, D), lambda g_, i: (g_, ridx(i), 4)),
            pl.BlockSpec((1, D), lambda g_, i: (0, 0)),
            pl.BlockSpec((D, LANES), lambda g_, i: (0, 0)),
            pl.BlockSpec((1, LANES), lambda g_, i: (0, 0)),
        ],
        out_specs=[
            pl.BlockSpec((None, tr, D), lambda g_, i: (g_, i, 0)),
            pl.BlockSpec((None, tr, LANES), lambda g_, i: (g_, i, 0)),
            pl.BlockSpec((None, tr, LANES), lambda g_, i: (g_, i, 0)),
        ],
        out_shape=[
            jax.ShapeDtypeStruct((G, R, D), F32),
            jax.ShapeDtypeStruct((G, R, LANES), I32),
            jax.ShapeDtypeStruct((G, R, LANES), F32),
        ],
        compiler_params=_cparams("parallel", "parallel"),
        name="norm_router",
    )(x, mod, mod, g, w_r, b_r)


def _route(top_idx, tm):
    n_tok = top_idx.shape[0]
    n_assign = n_tok * TOP_K
    n_tiles = -(-(n_assign + N_EXPERTS * (tm - 1)) // tm)
    flat_e = top_idx.reshape(-1)
    onehot = (flat_e[:, None] == jnp.arange(N_EXPERTS, dtype=I32)[None, :]).astype(I32)
    sizes = jnp.sum(onehot, axis=0)
    rank = jnp.sum(jnp.cumsum(onehot, axis=0) * onehot, axis=1) - 1
    padded = (sizes + tm - 1) // tm * tm
    pad_end = jnp.cumsum(padded)
    pad_start = pad_end - padded
    dest = (pad_start[flat_e] + rank).astype(I32)
    slot_tok = jnp.zeros((n_tiles * tm,), I32).at[dest].set(jnp.arange(n_assign, dtype=I32) // TOP_K)
    tile_start = jnp.arange(n_tiles, dtype=I32) * tm
    tile_e = jnp.minimum(jnp.sum((pad_end[None, :] <= tile_start[:, None]).astype(I32), axis=1), N_EXPERTS - 1)
    n_used = (pad_end[-1] // tm).astype(I32)
    tile_i = jnp.arange(n_tiles, dtype=I32)
    prev_e = jnp.concatenate([jnp.full((1,), -1, I32), tile_e[:-1]])
    first = jnp.logical_and(tile_i < n_used, tile_e != prev_e)
    grp = jnp.cumsum(first.astype(I32)) - 1
    next_start = pad_end[tile_e] // tm
    next_e = jnp.where(next_start < n_used, tile_e[jnp.minimum(next_start, n_tiles - 1)], -1).astype(I32)
    meta = jnp.stack([n_used, jnp.sum(first.astype(I32))]).astype(I32)
    return dest, slot_tok, tile_e, next_e, grp.astype(I32), meta


def _gather_kernel(slot_tok_ref, n_used_ref, h_ref, o_ref, buf, sem, *, tm):
    t = pl.program_id(0)
    n_used = n_used_ref[0]

    def row_copy(tile, r):
        slot = tile % 2
        tok = slot_tok_ref[tile * tm + r]
        return pltpu.make_async_copy(h_ref.at[pl.ds(tok, 1)], buf.at[slot, pl.ds(r, 1)], sem.at[slot])

    def start_tile(tile):
        def start(r, c):
            row_copy(tile, r).start()
            return c
        lax.fori_loop(0, tm, start, 0, unroll=8)

    def wait_tile(tile):
        def wait(r, c):
            row_copy(tile, r).wait()
            return c
        lax.fori_loop(0, tm, wait, 0, unroll=8)

    @pl.when(jnp.logical_and(t == 0, n_used > 0))
    def _():
        start_tile(t)

    @pl.when(t + 1 < n_used)
    def _():
        start_tile(t + 1)

    @pl.when(t < n_used)
    def _():
        wait_tile(t)
        o_ref[...] = buf[t % 2].astype(o_ref.dtype)

    @pl.when(t >= n_used)
    def _():
        o_ref[...] = jnp.zeros(o_ref.shape, o_ref.dtype)


def _gather_rows(h, slot_tok, n_used, tm):
    n_slots = slot_tok.shape[0]
    d = h.shape[1]
    return pl.pallas_call(
        functools.partial(_gather_kernel, tm=tm),
        grid_spec=pltpu.PrefetchScalarGridSpec(
            num_scalar_prefetch=2,
            grid=(n_slots // tm,),
            in_specs=[pl.BlockSpec(memory_space=pl.ANY)],
            out_specs=pl.BlockSpec((tm, d), lambda t, st, nu: (t, 0)),
            scratch_shapes=[pltpu.VMEM((2, tm, d), h.dtype), pltpu.SemaphoreType.DMA((2,))],
        ),
        out_shape=jax.ShapeDtypeStruct((n_slots, d), BF16),
        compiler_params=_cparams("arbitrary"),
        name="gather_rows",
    )(slot_tok, n_used, h)


def _stream_expert_weights(te_ref, nx_ref, grp_ref, meta_ref, copies, cast):
    j = pl.program_id(0)
    i = pl.program_id(1)
    first = jnp.logical_and(i < meta_ref[0],
                            jnp.logical_or(i == 0, te_ref[i] != te_ref[jnp.maximum(i - 1, 0)]))

    @pl.when(first)
    def _():
        run = j * meta_ref[1] + grp_ref[i]
        slot = run % 2

        @pl.when(run == 0)
        def _():
            for c in copies(te_ref[i], j, slot):
                c.start()

        for c in copies(te_ref[i], j, slot):
            c.wait()
        cast(slot)
        nxt = nx_ref[i]

        @pl.when(nxt >= 0)
        def _():
            for c in copies(nxt, j, 1 - slot):
                c.start()

        @pl.when(jnp.logical_and(nxt < 0, j + 1 < pl.num_programs(0)))
        def _():
            for c in copies(te_ref[0], j + 1, 1 - slot):
                c.start()


def _expert_gu_kernel(te_ref, nx_ref, grp_ref, meta_ref, x_ref, w_ref, bg_ref, bu_ref, o_ref,
                      wbuf, wg_scr, wu_scr, sem, *, tn, nj):
    i = pl.program_id(1)

    def copies(e, col, slot):
        return [pltpu.make_async_copy(w_ref.at[e, :, pl.ds(pl.multiple_of((half * nj + col) * tn, tn), tn)],
                                      wbuf.at[slot, half], sem.at[slot]) for half in range(2)]

    def cast(slot):
        wg_scr[...] = wbuf[slot, 0].astype(BF16)
        wu_scr[...] = wbuf[slot, 1].astype(BF16)

    _stream_expert_weights(te_ref, nx_ref, grp_ref, meta_ref, copies, cast)

    @pl.when(i < meta_ref[0])
    def _():
        x = x_ref[...]
        g = _dot(x, wg_scr[...]) + bg_ref[...]
        u = _dot(x, wu_scr[...]) + bu_ref[...]
        g = jnp.minimum(g, SWIGLU_LIMIT)
        u = jnp.clip(u, -SWIGLU_LIMIT, SWIGLU_LIMIT)
        o_ref[...] = ((u + 1.0) * g * jax.nn.sigmoid(SWIGLU_ALPHA * g)).astype(BF16)

    @pl.when(i >= meta_ref[0])
    def _():
        o_ref[...] = jnp.zeros(o_ref.shape, o_ref.dtype)


def _expert_gu(xs, w_gu, b_gu, sched, tm):
    n_slots, d = xs.shape
    de = w_gu.shape[2] // 2
    tn = 1024
    nj = de // tn
    row = lambda j, i, te, nx, gr, mt: jnp.minimum(i, mt[0] - 1)
    exp = lambda j, i, te, nx, gr, mt: te[jnp.minimum(i, mt[0] - 1)]
    b3 = b_gu.reshape(w_gu.shape[0], 1, 2 * de)
    return pl.pallas_call(
        functools.partial(_expert_gu_kernel, tn=tn, nj=nj),
        grid_spec=pltpu.PrefetchScalarGridSpec(
            num_scalar_prefetch=4,
            grid=(nj, n_slots // tm),
            in_specs=[
                pl.BlockSpec((tm, d), lambda j, i, *s: (row(j, i, *s), 0)),
                pl.BlockSpec(memory_space=pl.ANY),
                pl.BlockSpec((None, 1, tn), lambda j, i, *s: (exp(j, i, *s), 0, j)),
                pl.BlockSpec((None, 1, tn), lambda j, i, *s: (exp(j, i, *s), 0, nj + j)),
            ],
            out_specs=pl.BlockSpec((tm, tn), lambda j, i, *s: (i, j)),
            scratch_shapes=[pltpu.VMEM((2, 2, d, tn), F32), pltpu.VMEM((d, tn), BF16), pltpu.VMEM((d, tn), BF16),
                            pltpu.SemaphoreType.DMA((2,))],
        ),
        out_shape=jax.ShapeDtypeStruct((n_slots, de), BF16),
        compiler_params=_cparams("arbitrary", "arbitrary"),
        name="expert_gu",
    )(*sched, xs, w_gu, b3, b3)


def _expert_down_kernel(te_ref, nx_ref, grp_ref, meta_ref, a_ref, w_ref, b_ref, o_ref, wbuf, w_scr, sem, *, tn):
    i = pl.program_id(1)

    def copies(e, col, slot):
        return [pltpu.make_async_copy(w_ref.at[e, :, pl.ds(pl.multiple_of(col * tn, tn), tn)],
                                      wbuf.at[slot], sem.at[slot])]

    def cast(slot):
        w_scr[...] = wbuf[slot].astype(BF16)

    _stream_expert_weights(te_ref, nx_ref, grp_ref, meta_ref, copies, cast)

    @pl.when(i < meta_ref[0])
    def _():
        o_ref[...] = _dot(a_ref[...], w_scr[...]) + b_ref[...]

    @pl.when(i >= meta_ref[0])
    def _():
        o_ref[...] = jnp.zeros(o_ref.shape, o_ref.dtype)


def _expert_down(act, w_down, b_down, sched, tm):
    n_slots, de = act.shape
    d = w_down.shape[2]
    tn = 1024
    nj = d // tn
    row = lambda j, i, te, nx, gr, mt: jnp.minimum(i, mt[0] - 1)
    exp = lambda j, i, te, nx, gr, mt: te[jnp.minimum(i, mt[0] - 1)]
    return pl.pallas_call(
        functools.partial(_expert_down_kernel, tn=tn),
        grid_spec=pltpu.PrefetchScalarGridSpec(
            num_scalar_prefetch=4,
            grid=(nj, n_slots // tm),
            in_specs=[
                pl.BlockSpec((tm, de), lambda j, i, *s: (row(j, i, *s), 0)),
                pl.BlockSpec(memory_space=pl.ANY),
                pl.BlockSpec((None, 1, tn), lambda j, i, *s: (exp(j, i, *s), 0, j)),
            ],
            out_specs=pl.BlockSpec((tm, tn), lambda j, i, *s: (i, j)),
            scratch_shapes=[pltpu.VMEM((2, de, tn), F32), pltpu.VMEM((de, tn), BF16),
                            pltpu.SemaphoreType.DMA((2,))],
        ),
        out_shape=jax.ShapeDtypeStruct((n_slots, d), F32),
        compiler_params=_cparams("arbitrary", "arbitrary"),
        name="expert_down",
    )(*sched, act, w_down, b_down.reshape(w_down.shape[0], 1, d))


def _combine_kernel(dest_ref, y_ref, gate_ref, x_ref, g2_ref, gfin_ref, o_ref, ybuf, sem, *, tt, tok0, final):
    g_ = pl.program_id(0)
    i = pl.program_id(1)
    base = (tok0 + (g_ * pl.num_programs(1) + i) * tt) * TOP_K

    def row_copy(r, k):
        slot = dest_ref[base + r * TOP_K + k]
        return pltpu.make_async_copy(y_ref.at[pl.ds(slot, 1)], ybuf.at[k, pl.ds(r, 1)], sem)

    def start(r, c):
        for k in range(TOP_K):
            row_copy(r, k).start()
        return c

    def wait(r, c):
        for k in range(TOP_K):
            row_copy(r, k).wait()
        return c

    lax.fori_loop(0, tt, start, 0)
    lax.fori_loop(0, tt, wait, 0)
    gates = gate_ref[...]
    moe = gates[:, 0:1] * ybuf[0]
    for k in range(1, TOP_K):
        moe = moe + gates[:, k:k + 1] * ybuf[k]
    x2 = x_ref[...] + g2_ref[...] * moe
    if final:
        x2 = x2 * lax.rsqrt(jnp.mean(x2 * x2, axis=-1, keepdims=True) + RMS_EPS) * gfin_ref[...]
    o_ref[...] = x2


def _combine(y, dest, gates, x1, mod, g_final, *, tt, tok0, final):
    G, R, D = x1.shape
    rs = mod.shape[1]
    rblk = 1 if rs == 1 else tt
    ridx = (lambda i: 0) if rs == 1 else (lambda i: i)
    kern = functools.partial(_combine_kernel, tt=tt, tok0=tok0, final=final)
    return pl.pallas_call(
        kern,
        grid_spec=pltpu.PrefetchScalarGridSpec(
            num_scalar_prefetch=1,
            grid=(G, R // tt),
            in_specs=[
                pl.BlockSpec(memory_space=pl.ANY),
                pl.BlockSpec((None, tt, LANES), lambda g_, i, d_: (g_, i, 0)),
                pl.BlockSpec((None, tt, D), lambda g_, i, d_: (g_, i, 0)),
                pl.BlockSpec((None, rblk, D), lambda g_, i, d_: (g_, ridx(i), 5)),
                pl.BlockSpec((1, D), lambda g_, i, d_: (0, 0)),
            ],
            out_specs=pl.BlockSpec((None, tt, D), lambda g_, i, d_: (g_, i, 0)),
            scratch_shapes=[pltpu.VMEM((TOP_K, tt, D), F32), pltpu.SemaphoreType.DMA],
        ),
        out_shape=jax.ShapeDtypeStruct((G, R, D), F32),
        compiler_params=_cparams("arbitrary", "arbitrary"),
        name="combine",
    )(dest, y, gates, x1, mod, g_final)


def _row_tile(rows, cap):
    return max(t for t in range(MOBA_BLOCK, cap + 1, MOBA_BLOCK) if rows % t == 0)


def _layer_weights(l, w_in, b_forget, w_br_moba, w_br_sb, w_br_fox, w_out, w_router, b_router):
    d = w_in.shape[1]
    w_l = w_in[l]
    w_main = jnp.concatenate([w_l[:, :3 * D_ATTN], w_l[:, 3 * D_ATTN + H_FOX:]], axis=1).astype(BF16)
    w_f = jnp.pad(w_l[:, 3 * D_ATTN:3 * D_ATTN + H_FOX], ((0, 0), (0, LANES - H_FOX)))
    b_f = jnp.pad(b_forget[l], (0, LANES - H_FOX)).reshape(1, LANES)
    w_r = jnp.pad(w_router[l], ((0, 0), (0, LANES - N_EXPERTS)))
    b_r = jnp.pad(b_router[l], (0, LANES - N_EXPERTS), constant_values=NEG_INF).reshape(1, LANES)
    return dict(w_main=w_main, w_f=w_f, b_f=b_f, w_r=w_r, b_r=b_r,
                w_m=w_br_moba[l].astype(BF16), w_s=w_br_sb[l].astype(BF16), w_x=w_br_fox[l].astype(BF16),
                w_o=w_out[l].astype(BF16))


def kernel(x_prompt, x_sample, c_prompt, c_sample, cache_k, cache_v, cache_log_f, page_table, w_mod, b_mod,
           g_attn, w_in, b_forget, w_br_moba, w_br_sb, w_br_fox, w_out, g_ffn, w_router, b_router, w_gu, b_gu,
           w_down, b_down, g_final):
    depth = w_mod.shape[0]
    B, T, D = x_prompt.shape
    nb, tq, _ = x_sample.shape
    n_pool = cache_k.shape[1]
    n_prompt = B * T
    n_samp = nb * tq
    tm = EXPERT_TILE

    clf = jnp.pad(cache_log_f, ((0, 0), (0, 0), (0, 0), (HPG - H_FOX, 0))).reshape(
        depth, n_pool, PAGE // TOK_PER_BLK, LANES)
    c_all = jnp.pad(jnp.concatenate([c_prompt, c_sample], axis=0), ((0, 16 - B - nb), (0, 0)))
    wgu = w_gu.reshape(depth * N_EXPERTS, D, w_gu.shape[-1])
    bgu = b_gu.reshape(depth * N_EXPERTS, b_gu.shape[-1])
    wdn = w_down.reshape(depth * N_EXPERTS, w_down.shape[2], D)
    bdn = b_down.reshape(depth * N_EXPERTS, D)
    g_fin = g_final.reshape(1, D)

    xp = x_prompt
    xs = x_sample.reshape(1, n_samp, D)
    kp = jnp.zeros((depth, B, T, D), F32)
    vp = jnp.zeros((depth, B, T, D), F32)
    ks = jnp.zeros((depth, 1, n_samp, D), F32)
    vs = jnp.zeros((depth, 1, n_samp, D), F32)
    fp, fs = [], []
    for l in range(depth):
        lw = _layer_weights(l, w_in, b_forget, w_br_moba, w_br_sb, w_br_fox, w_out, w_router, b_router)
        g_a = g_attn[l].reshape(1, D)
        g_f = g_ffn[l].reshape(1, D)
        mod = _mod(c_all, w_mod, b_mod, l)
        mod_p = mod[:B].reshape(B, 1, 6 * D)
        mod_s = jnp.repeat(mod[B:B + nb], tq, axis=0).reshape(1, n_samp, 6 * D)

        qg_p, kp, vp, lf_p = _ln_proj(xp, mod_p, g_a, lw["w_main"], lw["w_f"], lw["b_f"], kp, vp, l,
                                      tr=_row_tile(T, 1024))
        fc, fr = _fcum(lf_p)
        x1p = _merge(_moba(qg_p, kp, vp, l), _sb(qg_p, kp, vp, l), _fox(qg_p, kp, vp, fc, fr, l), qg_p, xp, mod_p,
                     lw["w_m"], lw["w_s"], lw["w_x"], lw["w_o"], tr=256)
        qg_s, ks, vs, lf_s = _ln_proj(xs, mod_s, g_a, lw["w_main"], lw["w_f"], lw["b_f"], ks, vs, l, tr=n_samp)
        o_s = _decode_attn(qg_s[0, 0], ks[l, 0], vs[l, 0], lf_s, cache_k, cache_v, clf, page_table, l)
        o_s = o_s.reshape(1, n_samp, D)
        n_m, n_s = H_MOBA * HEAD_DIM, H_SB * HEAD_DIM
        x1s = _merge(o_s[..., :n_m], o_s[..., n_m:n_m + n_s], o_s[..., n_m + n_s:], qg_s, xs, mod_s,
                     lw["w_m"], lw["w_s"], lw["w_x"], lw["w_o"], tr=n_samp)

        h2p, idx_p, gate_p = _norm_router(x1p, mod_p, g_f, lw["w_r"], lw["b_r"], tr=512)
        h2s, idx_s, gate_s = _norm_router(x1s, mod_s, g_f, lw["w_r"], lw["b_r"], tr=n_samp)
        h2 = jnp.concatenate([h2p.reshape(n_prompt, D), h2s.reshape(n_samp, D)], axis=0)
        top_idx = jnp.concatenate([idx_p.reshape(n_prompt, LANES)[:, :TOP_K],
                                   idx_s.reshape(n_samp, LANES)[:, :TOP_K]], axis=0)
        dest, slot_tok, tile_e, next_e, grp, meta = _route(top_idx, tm)
        rows = _gather_rows(h2, slot_tok, meta, tm)
        sched = (tile_e + l * N_EXPERTS, jnp.where(next_e >= 0, next_e + l * N_EXPERTS, -1), grp, meta)
        act = _expert_gu(rows, wgu, bgu, sched, tm)
        y = _expert_down(act, wdn, bdn, sched, tm)
        final = l == depth - 1
        xp = _combine(y, dest, gate_p, x1p, mod_p, g_fin, tt=256, tok0=0, final=final)
        xs = _combine(y, dest, gate_s, x1s, mod_s, g_fin, tt=n_samp, tok0=n_prompt, final=final)

        fp.append(lf_p[..., :H_FOX])
        fs.append(lf_s[..., :H_FOX])

    hd = (N_HEADS, HEAD_DIM)
    return (xp, xs.reshape(nb, tq, D),
            kp.reshape(depth, B, T, *hd), vp.reshape(depth, B, T, *hd), jnp.stack(fp),
            ks.reshape(depth, nb, tq, *hd), vs.reshape(depth, nb, tq, *hd),
            jnp.stack(fs).reshape(depth, nb, tq, H_FOX))
```

```python
import functools

import jax
import jax.numpy as jnp
from jax import lax
from jax.experimental import pallas as pl
from jax.experimental.pallas import tpu as pltpu

F32 = jnp.float32
BF16 = jnp.bfloat16
I32 = jnp.int32

HEAD_DIM = 128
N_HEADS = 16
H_MOBA = 6
H_SB = 5
H_FOX = 5
D_ATTN = N_HEADS * HEAD_DIM
MOBA_BLOCK = 256
MOBA_TOPK = 3
N_EXPERTS = 32
TOP_K = 4
SWIGLU_ALPHA = 1.702
SWIGLU_LIMIT = 7.0
RMS_EPS = 1e-5
NEG_INF = -1e30
PAGE = 128
LANES = 128
VMEM_LIMIT_BYTES = 56 * 1024 * 1024
EXPERT_TILE = 256
MIXER_TQ = 256
MIXER_TK = 256


def _cparams(*sem):
    return pltpu.CompilerParams(dimension_semantics=sem, vmem_limit_bytes=VMEM_LIMIT_BYTES)


def _dot(a, b):
    return jnp.dot(a, b, preferred_element_type=F32)


def _dot_nt(a, b):
    return lax.dot_general(a, b, (((1,), (1,)), ((), ())), preferred_element_type=F32)


def _split2(x):
    hi = x.astype(BF16)
    lo = (x - hi.astype(F32)).astype(BF16)
    return hi, lo


def _split3(x):
    hi = x.astype(BF16)
    r = x - hi.astype(F32)
    mid = r.astype(BF16)
    lo = (r - mid.astype(F32)).astype(BF16)
    return hi, mid, lo


def _dot_acc(a, b):
    a_hi, a_lo = _split2(a)
    b_hi, b_lo = _split2(b)
    return _dot(a_hi, b_hi) + _dot(a_lo, b_hi) + _dot(a_hi, b_lo)


def _dot_nt_acc(a, b):
    a_hi, a_lo = _split2(a)
    b_hi, b_lo = _split2(b)
    return _dot_nt(a_hi, b_hi) + _dot_nt(a_lo, b_hi) + _dot_nt(a_hi, b_lo)


def _dot_exact_rhs(a, b_exact):
    hi, mid, lo = _split3(a)
    return _dot(hi, b_exact) + _dot(mid, b_exact) + _dot(lo, b_exact)


def _dot_exact_lhs(a_exact, b):
    hi, mid, lo = _split3(b)
    return _dot(a_exact, hi) + _dot(a_exact, mid) + _dot(a_exact, lo)


def _softplus_neg_abs(z):
    return jnp.log1p(jnp.exp(-jnp.abs(z)))


def _log_sigmoid(z):
    return -(jnp.maximum(-z, 0.0) + _softplus_neg_abs(z))


def _rms_modulate(x, g, scale, shift):
    y = x * lax.rsqrt(jnp.mean(x * x, axis=-1, keepdims=True) + RMS_EPS)
    return (y * g) * (1.0 + scale) + shift


def _mod_kernel(c_ref, w_ref, b_ref, o_ref):
    c = c_ref[...]
    o_ref[...] = _dot_acc(c * jax.nn.sigmoid(c), w_ref[...]) + b_ref[...]


def _mod(c, w, b, layer):
    m, d = c.shape
    depth, _, n = w.shape
    tn = 1024
    return pl.pallas_call(
        _mod_kernel,
        grid=(n // tn,),
        in_specs=[
            pl.BlockSpec((m, d), lambda j: (0, 0)),
            pl.BlockSpec((None, d, tn), lambda j: (layer, 0, j)),
            pl.BlockSpec((None, 1, tn), lambda j: (layer, 0, j)),
        ],
        out_specs=pl.BlockSpec((m, tn), lambda j: (0, j)),
        out_shape=jax.ShapeDtypeStruct((m, n), F32),
        compiler_params=_cparams("parallel"),
        name="mod",
    )(c, w, b.reshape(depth, 1, n))


def _ln_proj_kernel(x_ref, sh_ref, sc_ref, g_ref, w_ref, wf_ref, bf_ref, kprev_ref, vprev_ref,
                    qg_ref, k_ref, v_ref, lf_ref, h_scr, *, per):
    del kprev_ref, vprev_ref
    j = pl.program_id(2)

    @pl.when(j == 0)
    def _():
        h = _rms_modulate(x_ref[...], g_ref[...], sc_ref[...], sh_ref[...])
        h_scr[...] = h.astype(BF16)
        lf_ref[...] = _log_sigmoid(_dot_acc(h, wf_ref[...]) + bf_ref[...])

    acc = _dot(h_scr[...], w_ref[...])

    @pl.when(j < per)
    def _():
        qg_ref[...] = acc

    @pl.when(jnp.logical_and(j >= per, j < 2 * per))
    def _():
        k_ref[...] = acc

    @pl.when(jnp.logical_and(j >= 2 * per, j < 3 * per))
    def _():
        v_ref[...] = acc

    @pl.when(j >= 3 * per)
    def _():
        qg_ref[...] = jax.nn.sigmoid(acc)


def _ln_proj(x, mod, g, w_main, w_f, b_f, k_all, v_all, layer, *, tr):
    G, R, D = x.shape
    rs = mod.shape[1]
    tn = 512
    per = D // tn
    n_col = w_main.shape[1] // tn
    rblk = 1 if rs == 1 else tr
    ridx = (lambda i: 0) if rs == 1 else (lambda i: i)
    kern = functools.partial(_ln_proj_kernel, per=per)
    clamp = lambda j, lo: jnp.clip(j - lo, 0, per - 1)

    def qg_idx(g_, i, j):
        grp = jnp.where(j < 3 * per, 0, j // per - 2)
        col = jnp.where(j < per, j, jnp.where(j < 3 * per, per - 1, j % per))
        return (grp, g_, i, col)

    return pl.pallas_call(
        kern,
        grid=(G, R // tr, n_col),
        in_specs=[
            pl.BlockSpec((None, tr, D), lambda g_, i, j: (g_, i, 0)),
            pl.BlockSpec((None, rblk, D), lambda g_, i, j: (g_, ridx(i), 0)),
            pl.BlockSpec((None, rblk, D), lambda g_, i, j: (g_, ridx(i), 1)),
            pl.BlockSpec((1, D), lambda g_, i, j: (0, 0)),
            pl.BlockSpec((D, tn), lambda g_, i, j: (0, j)),
            pl.BlockSpec((D, LANES), lambda g_, i, j: (0, 0)),
            pl.BlockSpec((1, LANES), lambda g_, i, j: (0, 0)),
            pl.BlockSpec(memory_space=pl.ANY),
            pl.BlockSpec(memory_space=pl.ANY),
        ],
        out_specs=[
            pl.BlockSpec((None, None, tr, tn), qg_idx),
            pl.BlockSpec((None, None, tr, tn), lambda g_, i, j: (layer, g_, i, clamp(j, per))),
            pl.BlockSpec((None, None, tr, tn), lambda g_, i, j: (layer, g_, i, clamp(j, 2 * per))),
            pl.BlockSpec((None, tr, LANES), lambda g_, i, j: (g_, i, 0)),
        ],
        out_shape=[
            jax.ShapeDtypeStruct((n_col // per - 2, G, R, D), F32),
            jax.ShapeDtypeStruct(k_all.shape, F32),
            jax.ShapeDtypeStruct(v_all.shape, F32),
            jax.ShapeDtypeStruct((G, R, LANES), F32),
        ],
        input_output_aliases={7: 1, 8: 2},
        scratch_shapes=[pltpu.VMEM((tr, D), BF16)],
        compiler_params=_cparams("parallel", "parallel", "arbitrary"),
        name="ln_proj",
    )(x, mod, mod, g, w_main, w_f, b_f, k_all, v_all)


def _fcum_kernel(lf_ref, fc_ref, fr_ref):
    t = lf_ref.shape[0]
    r = lax.broadcasted_iota(I32, (t, t), 0)
    c = lax.broadcasted_iota(I32, (t, t), 1)
    tri = (r >= c).astype(BF16)
    f = _dot_exact_lhs(tri, lf_ref[...])
    fc_ref[...] = f
    fr_ref[...] = f.T[:8, :]


def _fcum(logf):
    B, T, _ = logf.shape
    return pl.pallas_call(
        _fcum_kernel,
        grid=(B,),
        in_specs=[pl.BlockSpec((None, T, LANES), lambda b: (b, 0, 0))],
        out_specs=[
            pl.BlockSpec((None, T, LANES), lambda b: (b, 0, 0)),
            pl.BlockSpec((None, 8, T), lambda b: (b, 0, 0)),
        ],
        out_shape=[jax.ShapeDtypeStruct((B, T, LANES), F32), jax.ShapeDtypeStruct((B, 8, T), F32)],
        compiler_params=_cparams("parallel"),
        name="fcum",
    )(logf)


def _head_specs(lead, head0, n_heads, rows, row_idx):
    return [
        pl.BlockSpec((None, None, rows, HEAD_DIM), lambda b, i, h=h: (lead, b, row_idx(i), head0 + h))
        for h in range(n_heads)
    ]


def _moba_kernel(*refs, n_heads, tq, n_kb):
    q_refs, k_refs, v_refs = refs[:n_heads], refs[n_heads:2 * n_heads], refs[2 * n_heads:3 * n_heads]
    o_ref, m_scr, l_scr, acc_scr, sel_scr = refs[3 * n_heads:]
    i = pl.program_id(1)
    own_blk = (i * tq) // MOBA_BLOCK
    scale = HEAD_DIM ** -0.5
    n_sel = min(MOBA_TOPK, n_kb)
    lane = lax.broadcasted_iota(I32, (tq, LANES), 1)
    qpos = i * tq + lax.broadcasted_iota(I32, (tq, MOBA_BLOCK), 0)
    koff = lax.broadcasted_iota(I32, (tq, MOBA_BLOCK), 1)
    for h in range(n_heads):
        q = q_refs[h][...]
        kmean = jnp.sum(k_refs[h][...].reshape(n_kb, MOBA_BLOCK, HEAD_DIM), axis=1) * (1.0 / MOBA_BLOCK)
        kmean = jnp.concatenate([kmean, jnp.zeros((LANES - n_kb, HEAD_DIM), F32)], axis=0)
        gate = jnp.where(lane < own_blk, _dot_nt_acc(q, kmean), NEG_INF)
        sel = jnp.zeros((tq, LANES), jnp.bool_)
        for _ in range(n_sel):
            m = jnp.max(gate, axis=-1, keepdims=True)
            idx = jnp.min(jnp.where(gate == m, lane, LANES), axis=-1, keepdims=True)
            hit = lane == idx
            sel = jnp.logical_or(sel, hit)
            gate = jnp.where(hit, -jnp.inf, gate)
        sel_scr[h] = jnp.where(jnp.logical_and(sel, lane < own_blk), 1.0, 0.0)
        m_scr[h] = jnp.full((tq, 1), NEG_INF, F32)
        l_scr[h] = jnp.zeros((tq, 1), F32)
        acc_scr[h] = jnp.zeros((tq, HEAD_DIM), F32)

    def body(n, carry):
        start = pl.multiple_of(n * MOBA_BLOCK, MOBA_BLOCK)
        kpos = n * MOBA_BLOCK + koff
        dist = (qpos - kpos).astype(F32)
        own = (koff * 0 + n) == own_blk
        own_visible = jnp.logical_and(own, kpos <= qpos)
        not_own = jnp.logical_not(own)
        for h in range(n_heads):
            slope = 2.0 ** (-8.0 * (h + 1) / n_heads)
            qb = q_refs[h][...].astype(BF16)
            kn = k_refs[h][pl.ds(start, MOBA_BLOCK), :].astype(BF16)
            vn = v_refs[h][pl.ds(start, MOBA_BLOCK), :].astype(BF16)
            s = _dot_nt(qb, kn) * scale - slope * dist
            sel_n = jnp.sum(jnp.where(lane == n, sel_scr[h], 0.0), axis=-1, keepdims=True) > 0.5
            valid = jnp.logical_or(own_visible, jnp.logical_and(not_own, sel_n))
            s = jnp.where(valid, s, NEG_INF)
            m_run = m_scr[h]
            m_new = jnp.maximum(m_run, jnp.max(s, axis=-1, keepdims=True))
            alpha = jnp.exp(m_run - m_new)
            p = jnp.where(valid, jnp.exp(s - m_new), 0.0)
            l_scr[h] = alpha * l_scr[h] + jnp.sum(p, axis=-1, keepdims=True)
            acc_scr[h] = alpha * acc_scr[h] + _dot(p.astype(BF16), vn)
            m_scr[h] = m_new
        return carry

    lax.fori_loop(0, own_blk + 1, body, 0)
    for h in range(n_heads):
        o_ref[:, h * HEAD_DIM:(h + 1) * HEAD_DIM] = acc_scr[h] / l_scr[h]


def _mixer_scratch(n_heads, tq, n_col, n_wide):
    return ([pltpu.VMEM((n_heads, tq, 1), F32)] * n_col
            + [pltpu.VMEM((n_heads, tq, HEAD_DIM), F32)] * n_wide)


def _moba(qg, k_all, v_all, layer):
    _, B, T, _ = qg.shape
    tq = MIXER_TQ
    n_kb = T // MOBA_BLOCK
    kern = functools.partial(_moba_kernel, n_heads=H_MOBA, tq=tq, n_kb=n_kb)
    specs = (_head_specs(0, 0, H_MOBA, tq, lambda i: i) + _head_specs(layer, 0, H_MOBA, T, lambda i: 0)
             + _head_specs(layer, 0, H_MOBA, T, lambda i: 0))
    return pl.pallas_call(
        kern,
        grid=(B, T // tq),
        in_specs=specs,
        out_specs=pl.BlockSpec((None, tq, H_MOBA * HEAD_DIM), lambda b, i: (b, i, 0)),
        out_shape=jax.ShapeDtypeStruct((B, T, H_MOBA * HEAD_DIM), F32),
        scratch_shapes=_mixer_scratch(H_MOBA, tq, 2, 2),
        compiler_params=_cparams("parallel", "arbitrary"),
        name="moba",
    )(*([qg] * H_MOBA + [k_all] * H_MOBA + [v_all] * H_MOBA))


def _strict_upper(n):
    j = lax.broadcasted_iota(I32, (n, n), 0)
    s = lax.broadcasted_iota(I32, (n, n), 1)
    return (j > s).astype(BF16)


def _sb_kernel(*refs, n_heads, tq, tk):
    q_refs, k_refs, v_refs = refs[:n_heads], refs[n_heads:2 * n_heads], refs[2 * n_heads:3 * n_heads]
    o_ref, c_scr, acc_scr = refs[3 * n_heads:]
    i = pl.program_id(1)
    scale = HEAD_DIM ** -0.5
    qpos = i * tq + lax.broadcasted_iota(I32, (tq, tk), 0)
    koff = lax.broadcasted_iota(I32, (tq, tk), 1)
    after = _strict_upper(tk)
    c_scr[...] = jnp.zeros(c_scr.shape, F32)
    acc_scr[...] = jnp.zeros(acc_scr.shape, F32)

    last = (i * tq) // tk

    def body(t, carry):
        n = last - t
        start = pl.multiple_of(n * tk, tk)
        strict = (n * tk + koff) < qpos
        for h in range(n_heads):
            qb = q_refs[h][...].astype(BF16)
            kn = k_refs[h][pl.ds(start, tk), :].astype(BF16)
            vn = v_refs[h][pl.ds(start, tk), :].astype(BF16)
            z = _dot_nt(qb, kn) * scale
            sp = _softplus_neg_abs(z)
            log_beta = -(jnp.maximum(-z, 0.0) + sp)
            log_keep = jnp.where(strict, -(jnp.maximum(z, 0.0) + sp), 0.0)
            c_run = c_scr[h]
            between = _dot_exact_rhs(log_keep, after) + c_run
            w = jnp.where(strict, jnp.exp(log_beta + between), 0.0)
            acc_scr[h] = acc_scr[h] + _dot(w.astype(BF16), vn)
            c_scr[h] = c_run + jnp.sum(log_keep, axis=-1, keepdims=True)
        return carry

    lax.fori_loop(0, last + 1, body, 0)
    for h in range(n_heads):
        o_ref[:, h * HEAD_DIM:(h + 1) * HEAD_DIM] = acc_scr[h]


def _sb(qg, k_all, v_all, layer):
    _, B, T, _ = qg.shape
    tq, tk = MIXER_TQ, MIXER_TK
    kern = functools.partial(_sb_kernel, n_heads=H_SB, tq=tq, tk=tk)
    specs = (_head_specs(0, H_MOBA, H_SB, tq, lambda i: i) + _head_specs(layer, H_MOBA, H_SB, T, lambda i: 0)
             + _head_specs(layer, H_MOBA, H_SB, T, lambda i: 0))
    return pl.pallas_call(
        kern,
        grid=(B, T // tq),
        in_specs=specs,
        out_specs=pl.BlockSpec((None, tq, H_SB * HEAD_DIM), lambda b, i: (b, i, 0)),
        out_shape=jax.ShapeDtypeStruct((B, T, H_SB * HEAD_DIM), F32),
        scratch_shapes=_mixer_scratch(H_SB, tq, 1, 1),
        compiler_params=_cparams("parallel", "arbitrary"),
        name="sb",
    )(*([qg] * H_SB + [k_all] * H_SB + [v_all] * H_SB))


def _fox_kernel(*refs, n_heads, tq, tk):
    q_refs, k_refs, v_refs = refs[:n_heads], refs[n_heads:2 * n_heads], refs[2 * n_heads:3 * n_heads]
    fc_ref, fr_ref, o_ref, m_scr, l_scr, acc_scr = refs[3 * n_heads:]
    i = pl.program_id(1)
    scale = HEAD_DIM ** -0.5
    qpos = i * tq + lax.broadcasted_iota(I32, (tq, tk), 0)
    koff = lax.broadcasted_iota(I32, (tq, tk), 1)
    m_scr[...] = jnp.full(m_scr.shape, NEG_INF, F32)
    l_scr[...] = jnp.zeros(l_scr.shape, F32)
    acc_scr[...] = jnp.zeros(acc_scr.shape, F32)

    def body(n, carry):
        start = pl.multiple_of(n * tk, tk)
        valid = (n * tk + koff) <= qpos
        for h in range(n_heads):
            qb = q_refs[h][...].astype(BF16)
            kn = k_refs[h][pl.ds(start, tk), :].astype(BF16)
            vn = v_refs[h][pl.ds(start, tk), :].astype(BF16)
            fq = fc_ref[:, h:h + 1]
            fk = fr_ref[h:h + 1, pl.ds(start, tk)]
            s = jnp.where(valid, _dot_nt(qb, kn) * scale + (fq - fk), NEG_INF)
            m_run = m_scr[h]
            m_new = jnp.maximum(m_run, jnp.max(s, axis=-1, keepdims=True))
            alpha = jnp.exp(m_run - m_new)
            p = jnp.where(valid, jnp.exp(s - m_new), 0.0)
            l_scr[h] = alpha * l_scr[h] + jnp.sum(p, axis=-1, keepdims=True)
            acc_scr[h] = alpha * acc_scr[h] + _dot(p.astype(BF16), vn)
            m_scr[h] = m_new
        return carry

    lax.fori_loop(0, (i * tq) // tk + 1, body, 0)
    for h in range(n_heads):
        o_ref[:, h * HEAD_DIM:(h + 1) * HEAD_DIM] = acc_scr[h] / l_scr[h]


def _fox(qg, k_all, v_all, fc, fr, layer):
    _, B, T, _ = qg.shape
    tq, tk = MIXER_TQ, MIXER_TK
    h0 = H_MOBA + H_SB
    kern = functools.partial(_fox_kernel, n_heads=H_FOX, tq=tq, tk=tk)
    specs = (_head_specs(0, h0, H_FOX, tq, lambda i: i) + _head_specs(layer, h0, H_FOX, T, lambda i: 0)
             + _head_specs(layer, h0, H_FOX, T, lambda i: 0)
             + [pl.BlockSpec((None, tq, LANES), lambda b, i: (b, i, 0)),
                pl.BlockSpec((None, 8, T), lambda b, i: (b, 0, 0))])
    return pl.pallas_call(
        kern,
        grid=(B, T // tq),
        in_specs=specs,
        out_specs=pl.BlockSpec((None, tq, H_FOX * HEAD_DIM), lambda b, i: (b, i, 0)),
        out_shape=jax.ShapeDtypeStruct((B, T, H_FOX * HEAD_DIM), F32),
        scratch_shapes=_mixer_scratch(H_FOX, tq, 2, 1),
        compiler_params=_cparams("parallel", "arbitrary"),
        name="fox",
    )(*([qg] * H_FOX + [k_all] * H_FOX + [v_all] * H_FOX), fc, fr)


HPG = 8
TOK_PER_BLK = LANES // HPG


def _lane_class_mats():
    j = lax.broadcasted_iota(I32, (LANES, LANES), 0)
    s = lax.broadcasted_iota(I32, (LANES, LANES), 1)
    same = (j & 7) == (s & 7)
    after = jnp.logical_and(same, (j >> 3) > (s >> 3)).astype(BF16)
    return after, same.astype(BF16)


def _scan_blocked(xb, rows, nblk, carry, after, same):
    within = _dot_exact_rhs(xb, after)
    tot = _dot_exact_rhs(xb, same)
    out = [None] * nblk
    run = carry
    for b in range(nblk - 1, -1, -1):
        out[b] = within[b * rows:(b + 1) * rows] + run
        run = run + tot[b * rows:(b + 1) * rows]
    return out, run


def _decode_kernel(pt_ref, q_ref, kn_ref, vn_ref, lfn_ref, lfnb_ref, kb_ref, vb_ref, lfb_ref,
                   ka_ref, va_ref, lfa_ref, o_ref,
                   mb_m, mb_l, mb_g, mb_o, own_m, own_l, own_o, sb_acc0, sb_acc1, sb_c0, sb_c1,
                   fx_m, fx_l, fx_acc, fx_c, a_col, *, n_pages, tq):
    j = pl.program_id(1)
    page_b = n_pages - 1 - 2 * j
    t_past = n_pages * PAGE
    scale = HEAD_DIM ** -0.5
    n_rows = HPG * tq
    r_mb = H_MOBA * tq
    r_sb1 = (H_MOBA + H_SB - HPG) * tq
    after, same = _lane_class_mats()
    lane1 = lax.broadcasted_iota(I32, (tq, LANES), 1)

    def lane_blocks(x):
        return [x[:, b * LANES:(b + 1) * LANES] for b in range(x.shape[1] // LANES)]

    def process(k_ref, v_ref, lfblk, page, is_new):
        t_tok = k_ref.shape[0]
        nblk = t_tok // TOK_PER_BLK
        w = t_tok * HPG
        lane = lax.broadcasted_iota(I32, (n_rows, w), 1)
        row = lax.broadcasted_iota(I32, (n_rows, w), 0)
        tpos = lane >> 3
        r = row & 7
        mine = (lane & 7) == (row >> 3)
        if is_new:
            vis = jnp.logical_and(mine, tpos <= r)
            vis_strict = jnp.logical_and(mine, tpos < r)
            dist = (r - tpos).astype(F32)
        else:
            vis = vis_strict = mine
            dist = (t_past + r - (page * PAGE + tpos)).astype(F32)
        rowc = lax.broadcasted_iota(I32, (n_rows, 1), 0) >> 3
        slope = jnp.zeros((n_rows, 1), F32)
        for h in range(H_MOBA):
            slope = jnp.where(rowc == h, 2.0 ** (-8.0 * (h + 1) / H_MOBA), slope)

        s_g, v_g = [], []
        for g in range(2):
            kg = k_ref[:, g * HPG:(g + 1) * HPG, :].reshape(w, HEAD_DIM).astype(BF16)
            v_g.append(v_ref[:, g * HPG:(g + 1) * HPG, :].reshape(w, HEAD_DIM).astype(BF16))
            s_g.append(_dot_nt(q_ref[g].astype(BF16), kg) * scale)

        def own_slot(c, row0, rows):
            mine1 = (lax.broadcasted_iota(I32, (rows, LANES), 1) & 7) == (
                (row0 + lax.broadcasted_iota(I32, (rows, LANES), 0)) >> 3)
            return jnp.max(jnp.where(mine1, c, -jnp.inf), axis=-1, keepdims=True)

        def sb_weights(z, visible, c_ref, row0):
            rows = z.shape[0]
            sp = _softplus_neg_abs(z)
            log_beta = -(jnp.maximum(-z, 0.0) + sp)
            log_keep = jnp.where(visible, -(jnp.maximum(z, 0.0) + sp), 0.0)
            carry = c_ref[...]
            between, total = _scan_blocked(jnp.concatenate(lane_blocks(log_keep), axis=0), rows, nblk,
                                           jnp.zeros((rows, LANES), F32), after, same)
            c_ref[...] = carry + total
            ws = [jnp.where(vb, jnp.exp(lb + bt), 0.0)
                  for vb, lb, bt in zip(lane_blocks(visible), lane_blocks(log_beta), between)]
            return jnp.concatenate(ws, axis=1), jnp.exp(own_slot(carry, row0, rows))

        s = jnp.where(vis[:r_mb], s_g[0][:r_mb] - slope[:r_mb] * dist[:r_mb], NEG_INF)
        m = jnp.max(s, axis=-1, keepdims=True)
        p_mb = jnp.exp(s - m)
        l = jnp.sum(p_mb, axis=-1, keepdims=True)
        w_sb0, kept0 = sb_weights(s_g[0][r_mb:], vis_strict[r_mb:], sb_c0, r_mb)
        pv0 = _dot(jnp.concatenate([p_mb, w_sb0], axis=0).astype(BF16), v_g[0])
        sb_acc0[...] = sb_acc0[...] + kept0 * pv0[r_mb:]
        if is_new:
            own_m[...] = m
            own_l[...] = l
            own_o[...] = pv0[:r_mb]
        else:
            ksum = jnp.sum(k_ref[:, 0:HPG, :], axis=0)
            ksum_rows = jnp.concatenate(
                [jnp.broadcast_to(ksum[h:h + 1, :], (tq, HEAD_DIM)) for h in range(H_MOBA)], axis=0)
            g_col = jnp.sum(q_ref[0][:r_mb] * ksum_rows, axis=-1, keepdims=True)
            here = lane1 == page
            for h in range(H_MOBA):
                rs = slice(h * tq, (h + 1) * tq)
                mb_m[h] = jnp.where(here, m[rs], mb_m[h])
                mb_l[h] = jnp.where(here, l[rs], mb_l[h])
                mb_g[h] = jnp.where(here, g_col[rs], mb_g[h])
                for rr in range(tq):
                    mb_o[h, rr, pl.ds(page, 1), :] = pv0[h * tq + rr:h * tq + rr + 1, :]

        w_sb1, kept1 = sb_weights(s_g[1][:r_sb1], vis_strict[:r_sb1], sb_c1, 0)
        n_fx = n_rows - r_sb1
        carry = fx_c[...]
        bias_blk, total = _scan_blocked(lfblk, 1, nblk, jnp.zeros((1, LANES), F32), after, same)
        fx_c[...] = carry + total
        bias = jnp.concatenate([jnp.broadcast_to(bb, (n_fx, LANES)) for bb in bias_blk], axis=1)
        s = jnp.where(vis[r_sb1:], s_g[1][r_sb1:] + bias + a_col[r_sb1:], NEG_INF)
        m_loc = jnp.max(s, axis=-1, keepdims=True)
        p_fx = jnp.exp(s - m_loc)
        pv1 = _dot(jnp.concatenate([w_sb1, p_fx], axis=0).astype(BF16), v_g[1])
        sb_acc1[...] = sb_acc1[...] + kept1 * pv1[:r_sb1]
        m_page = m_loc + own_slot(carry, r_sb1, n_fx)
        m_run = fx_m[...]
        m_new = jnp.maximum(m_run, m_page)
        alpha = jnp.exp(m_run - m_new)
        beta = jnp.exp(m_page - m_new)
        fx_l[...] = alpha * fx_l[...] + beta * jnp.sum(p_fx, axis=-1, keepdims=True)
        fx_acc[...] = alpha * fx_acc[...] + beta * pv1[r_sb1:]
        fx_m[...] = m_new

    @pl.when(j == 0)
    def _():
        mb_m[...] = jnp.full(mb_m.shape, NEG_INF, F32)
        mb_l[...] = jnp.zeros(mb_l.shape, F32)
        mb_g[...] = jnp.full(mb_g.shape, NEG_INF, F32)
        mb_o[...] = jnp.zeros(mb_o.shape, F32)
        sb_acc0[...] = jnp.zeros(sb_acc0.shape, F32)
        sb_acc1[...] = jnp.zeros(sb_acc1.shape, F32)
        sb_c0[...] = jnp.zeros(sb_c0.shape, F32)
        sb_c1[...] = jnp.zeros(sb_c1.shape, F32)
        fx_m[...] = jnp.full(fx_m.shape, NEG_INF, F32)
        fx_l[...] = jnp.zeros(fx_l.shape, F32)
        fx_acc[...] = jnp.zeros(fx_acc.shape, F32)
        lfn = lfn_ref[...]
        row1 = lax.broadcasted_iota(I32, (tq, LANES), 0)
        a_new = jnp.zeros((tq, LANES), F32)
        for rr in range(tq):
            a_new = a_new + jnp.where(row1 >= rr, lfn[rr:rr + 1, :], 0.0)
        a_col[...] = jnp.concatenate([jnp.zeros((r_sb1, 1), F32)]
                                     + [a_new[:, hh:hh + 1] for hh in range(H_FOX)], axis=0)
        lfnb = lfnb_ref[...]
        fx_c[...] = -_dot_exact_rhs(lfnb, same)
        process(kn_ref, vn_ref, lfnb, None, True)

    process(kb_ref, vb_ref, lfb_ref[...], page_b, False)
    process(ka_ref, va_ref, lfa_ref[...], page_b - 1, False)

    @pl.when(j == n_pages // 2 - 1)
    def _():
        pj = lax.broadcasted_iota(I32, (LANES, LANES), 0)
        ps = lax.broadcasted_iota(I32, (LANES, LANES), 1)
        pair = ((pj >> 1) == (ps >> 1)).astype(BF16)
        n_kb = n_pages * PAGE // MOBA_BLOCK + 1
        for h in range(H_MOBA):
            rs = slice(h * tq, (h + 1) * tq)
            gate = jnp.where(lane1 < n_pages, _dot_exact_rhs(mb_g[h], pair), NEG_INF)
            sel = jnp.zeros((tq, LANES), jnp.bool_)
            for _ in range(min(MOBA_TOPK, n_kb)):
                m = jnp.max(gate, axis=-1, keepdims=True)
                idx = jnp.min(jnp.where(gate == m, lane1, LANES), axis=-1, keepdims=True)
                hit = (lane1 >> 1) == (idx >> 1)
                sel = jnp.logical_or(sel, hit)
                gate = jnp.where(hit, -jnp.inf, gate)
            sel = jnp.logical_and(sel, lane1 < n_pages)
            m_pg = mb_m[h]
            m_own = own_m[rs]
            m_fin = jnp.maximum(jnp.max(jnp.where(sel, m_pg, NEG_INF), axis=-1, keepdims=True), m_own)
            wp = jnp.where(sel, jnp.exp(m_pg - m_fin), 0.0)
            w_own = jnp.exp(m_own - m_fin)
            l_fin = jnp.sum(wp * mb_l[h], axis=-1, keepdims=True) + w_own * own_l[rs]
            rows = [_dot_acc(wp[rr:rr + 1, :n_pages], mb_o[h, rr]) for rr in range(tq)]
            o_fin = jnp.concatenate(rows, axis=0) + w_own * own_o[rs]
            o_ref[0, rs, :] = o_fin / l_fin
        o_ref[0, r_mb:, :] = sb_acc0[...]
        o_ref[1, :r_sb1, :] = sb_acc1[...]
        o_ref[1, r_sb1:, :] = fx_acc[...] / fx_l[...]


def _decode_attn(q_s, k_s, v_s, logf_s, cache_k, cache_v, cache_lfb, page_table, layer):
    nb, n_pages = page_table.shape
    d = q_s.shape[-1]
    tq = q_s.shape[0] // nb
    assert tq == HPG and N_HEADS == 2 * HPG and n_pages % 2 == 0 and n_pages <= LANES
    slot0 = H_MOBA + H_SB - HPG
    n_rows = HPG * tq
    qg = q_s.reshape(nb, tq, 2, HPG, HEAD_DIM).transpose(0, 2, 3, 1, 4).reshape(nb, 2, n_rows, HEAD_DIM)
    pad_tok = ((0, 0), (0, TOK_PER_BLK - tq), (0, 0), (0, 0))
    knew = jnp.pad(k_s.reshape(nb, tq, N_HEADS, HEAD_DIM), pad_tok)
    vnew = jnp.pad(v_s.reshape(nb, tq, N_HEADS, HEAD_DIM), pad_tok)
    lfnb = jnp.pad(logf_s[0, :, :H_FOX].reshape(nb, tq, H_FOX),
                   ((0, 0), (0, TOK_PER_BLK - tq), (slot0, 0))).reshape(nb, 1, LANES)
    kern = functools.partial(_decode_kernel, n_pages=n_pages, tq=tq)
    page_spec = lambda back: pl.BlockSpec(
        (None, None, PAGE, N_HEADS, HEAD_DIM), lambda b, j, pt: (layer, pt[b, n_pages - 1 - 2 * j - back], 0, 0, 0))
    lf_spec = lambda back: pl.BlockSpec(
        (None, None, PAGE // TOK_PER_BLK, LANES), lambda b, j, pt: (layer, pt[b, n_pages - 1 - 2 * j - back], 0, 0))
    r_mb = H_MOBA * tq
    r_sb1 = slot0 * tq
    vm = pltpu.VMEM
    out = pl.pallas_call(
        kern,
        grid_spec=pltpu.PrefetchScalarGridSpec(
            num_scalar_prefetch=1,
            grid=(nb, n_pages // 2),
            in_specs=[
                pl.BlockSpec((None, 2, n_rows, HEAD_DIM), lambda b, j, pt: (b, 0, 0, 0)),
                pl.BlockSpec((None, TOK_PER_BLK, N_HEADS, HEAD_DIM), lambda b, j, pt: (b, 0, 0, 0)),
                pl.BlockSpec((None, TOK_PER_BLK, N_HEADS, HEAD_DIM), lambda b, j, pt: (b, 0, 0, 0)),
                pl.BlockSpec((None, tq, LANES), lambda b, j, pt: (0, b, 0)),
                pl.BlockSpec((None, 1, LANES), lambda b, j, pt: (b, 0, 0)),
                page_spec(0), page_spec(0), lf_spec(0),
                page_spec(1), page_spec(1), lf_spec(1),
            ],
            out_specs=pl.BlockSpec((None, 2, n_rows, HEAD_DIM), lambda b, j, pt: (b, 0, 0, 0)),
            scratch_shapes=[
                vm((H_MOBA, tq, LANES), F32), vm((H_MOBA, tq, LANES), F32), vm((H_MOBA, tq, LANES), F32),
                vm((H_MOBA, tq, n_pages, HEAD_DIM), F32),
                vm((r_mb, 1), F32), vm((r_mb, 1), F32), vm((r_mb, HEAD_DIM), F32),
                vm((n_rows - r_mb, HEAD_DIM), F32), vm((r_sb1, HEAD_DIM), F32),
                vm((n_rows - r_mb, LANES), F32), vm((r_sb1, LANES), F32),
                vm((n_rows - r_sb1, 1), F32), vm((n_rows - r_sb1, 1), F32), vm((n_rows - r_sb1, HEAD_DIM), F32),
                vm((1, LANES), F32), vm((n_rows, 1), F32),
            ],
        ),
        out_shape=jax.ShapeDtypeStruct((nb, 2, n_rows, HEAD_DIM), F32),
        compiler_params=_cparams("parallel", "arbitrary"),
        name="decode_attn",
    )(page_table, qg, knew, vnew, logf_s, lfnb, cache_k, cache_v, cache_lfb, cache_k, cache_v, cache_lfb)
    return out.reshape(nb, 2, HPG, tq, HEAD_DIM).transpose(0, 3, 1, 2, 4).reshape(nb * tq, d)


def _merge_kernel(om_ref, os_ref, of_ref, gm_ref, gs_ref, gf_ref, x_ref, g1_ref,
                  wm_ref, ws_ref, wf_ref, wo_ref, o_ref, mrg_scr):
    j = pl.program_id(2)

    @pl.when(j == 0)
    def _():
        merged = (gm_ref[...] * _dot(om_ref[...].astype(BF16), wm_ref[...])
                  + gs_ref[...] * _dot(os_ref[...].astype(BF16), ws_ref[...])
                  + gf_ref[...] * _dot(of_ref[...].astype(BF16), wf_ref[...]))
        mrg_scr[...] = merged.astype(BF16)

    o_ref[...] = x_ref[...] + g1_ref[...] * _dot(mrg_scr[...], wo_ref[...])


def _merge(o_moba, o_sb, o_fox, proj, x, mod, w_m, w_s, w_f, w_o, *, tr):
    G, R, D = x.shape
    rs = mod.shape[1]
    tn = 1024
    per = D // tn
    rblk = 1 if rs == 1 else tr
    ridx = (lambda i: 0) if rs == 1 else (lambda i: i)
    full = lambda a: pl.BlockSpec(a.shape, lambda g_, i, j: (0, 0))
    return pl.pallas_call(
        _merge_kernel,
        grid=(G, R // tr, per),
        in_specs=[
            pl.BlockSpec((None, tr, o_moba.shape[-1]), lambda g_, i, j: (g_, i, 0)),
            pl.BlockSpec((None, tr, o_sb.shape[-1]), lambda g_, i, j: (g_, i, 0)),
            pl.BlockSpec((None, tr, o_fox.shape[-1]), lambda g_, i, j: (g_, i, 0)),
            pl.BlockSpec((None, None, tr, D), lambda g_, i, j: (1, g_, i, 0)),
            pl.BlockSpec((None, None, tr, D), lambda g_, i, j: (2, g_, i, 0)),
            pl.BlockSpec((None, None, tr, D), lambda g_, i, j: (3, g_, i, 0)),
            pl.BlockSpec((None, tr, tn), lambda g_, i, j: (g_, i, j)),
            pl.BlockSpec((None, rblk, tn), lambda g_, i, j: (g_, ridx(i), 2 * per + j)),
            full(w_m), full(w_s), full(w_f),
            pl.BlockSpec((D, tn), lambda g_, i, j: (0, j)),
        ],
        out_specs=pl.BlockSpec((None, tr, tn), lambda g_, i, j: (g_, i, j)),
        out_shape=jax.ShapeDtypeStruct((G, R, D), F32),
        scratch_shapes=[pltpu.VMEM((tr, D), BF16)],
        compiler_params=_cparams("parallel", "parallel", "arbitrary"),
        name="merge",
    )(o_moba, o_sb, o_fox, proj, proj, proj, x, mod, w_m, w_s, w_f, w_o)


def _norm_router_kernel(x_ref, sh_ref, sc_ref, g_ref, wr_ref, br_ref, h_ref, idx_ref, gate_ref):
    h = _rms_modulate(x_ref[...], g_ref[...], sc_ref[...], sh_ref[...])
    h_ref[...] = h
    tr = h.shape[0]
    lane = lax.broadcasted_iota(I32, (tr, LANES), 1)
    logits = _dot_acc(h, wr_ref[...]) + br_ref[...]
    vals, idxs = [], []
    for _ in range(TOP_K):
        m = jnp.max(logits, axis=-1, keepdims=True)
        idx = jnp.min(jnp.where(logits == m, lane, LANES), axis=-1, keepdims=True)
        vals.append(m)
        idxs.append(idx)
        logits = jnp.where(lane == idx, -jnp.inf, logits)
    es = [jnp.exp(v - vals[0]) for v in vals]
    denom = es[0] + es[1] + es[2] + es[3]
    idx_out = jnp.zeros((tr, LANES), I32)
    gate_out = jnp.zeros((tr, LANES), F32)
    for k in range(TOP_K):
        idx_out = jnp.where(lane == k, idxs[k], idx_out)
        gate_out = jnp.where(lane == k, es[k] / denom, gate_out)
    idx_ref[...] = idx_out
    gate_ref[...] = gate_out


def _norm_router(x, mod, g, w_r, b_r, *, tr):
    G, R, D = x.shape
    rs = mod.shape[1]
    rblk = 1 if rs == 1 else tr
    ridx = (lambda i: 0) if rs == 1 else (lambda i: i)
    return pl.pallas_call(
        _norm_router_kernel,
        grid=(G, R // tr),
        in_specs=[
            pl.BlockSpec((None, tr, D), lambda g_, i: (g_, i, 0)),
            pl.BlockSpec((None, rblk, D), lambda g_, i: (g_, ridx(i), 3)),
            pl.BlockSpec((None, rblk, D), lambda g_, i: (g_, ridx(i), 4)),
            pl.BlockSpec((1, D), lambda g_, i: (0, 0)),
            pl.BlockSpec((D, LANES), lambda g_, i: (0, 0)),
            pl.BlockSpec((1, LANES), lambda g_, i: (0, 0)),
        ],
        out_specs=[
            pl.BlockSpec((None, tr, D), lambda g_, i: (g_, i, 0)),
            pl.BlockSpec((None, tr, LANES), lambda g_, i: (g_, i, 0)),
            pl.BlockSpec((None, tr, LANES), lambda g_, i: (g_, i, 0)),
        ],
        out_shape=[
            jax.ShapeDtypeStruct((G, R, D), F32),
            jax.ShapeDtypeStruct((G, R, LANES), I32),
            jax.ShapeDtypeStruct((G, R, LANES), F32),
        ],
        compiler_params=_cparams("parallel", "parallel"),
        name="norm_router",
    )(x, mod, mod, g, w_r, b_r)


def _route(top_idx, tm):
    n_tok = top_idx.shape[0]
    n_assign = n_tok * TOP_K
    n_tiles = -(-(n_assign + N_EXPERTS * (tm - 1)) // tm)
    flat_e = top_idx.reshape(-1)
    onehot = (flat_e[:, None] == jnp.arange(N_EXPERTS, dtype=I32)[None, :]).astype(I32)
    sizes = jnp.sum(onehot, axis=0)
    rank = jnp.sum(jnp.cumsum(onehot, axis=0) * onehot, axis=1) - 1
    padded = (sizes + tm - 1) // tm * tm
    pad_end = jnp.cumsum(padded)
    pad_start = pad_end - padded
    dest = (pad_start[flat_e] + rank).astype(I32)
    slot_tok = jnp.zeros((n_tiles * tm,), I32).at[dest].set(jnp.arange(n_assign, dtype=I32) // TOP_K)
    tile_start = jnp.arange(n_tiles, dtype=I32) * tm
    tile_e = jnp.minimum(jnp.sum((pad_end[None, :] <= tile_start[:, None]).astype(I32), axis=1), N_EXPERTS - 1)
    n_used = (pad_end[-1] // tm).astype(I32)
    tile_i = jnp.arange(n_tiles, dtype=I32)
    prev_e = jnp.concatenate([jnp.full((1,), -1, I32), tile_e[:-1]])
    first = jnp.logical_and(tile_i < n_used, tile_e != prev_e)
    grp = jnp.cumsum(first.astype(I32)) - 1
    next_start = pad_end[tile_e] // tm
    next_e = jnp.where(next_start < n_used, tile_e[jnp.minimum(next_start, n_tiles - 1)], -1).astype(I32)
    meta = jnp.stack([n_used, jnp.sum(first.astype(I32))]).astype(I32)
    return dest, slot_tok, tile_e, next_e, grp.astype(I32), meta


def _gather_kernel(slot_tok_ref, n_used_ref, h_ref, o_ref, buf, sem, *, tm):
    t = pl.program_id(0)
    n_used = n_used_ref[0]

    def row_copy(tile, r):
        slot = tile % 2
        tok = slot_tok_ref[tile * tm + r]
        return pltpu.make_async_copy(h_ref.at[pl.ds(tok, 1)], buf.at[slot, pl.ds(r, 1)], sem.at[slot])

    def start_tile(tile):
        def start(r, c):
            row_copy(tile, r).start()
            return c
        lax.fori_loop(0, tm, start, 0, unroll=8)

    def wait_tile(tile):
        def wait(r, c):
            row_copy(tile, r).wait()
            return c
        lax.fori_loop(0, tm, wait, 0, unroll=8)

    @pl.when(jnp.logical_and(t == 0, n_used > 0))
    def _():
        start_tile(t)

    @pl.when(t + 1 < n_used)
    def _():
        start_tile(t + 1)

    @pl.when(t < n_used)
    def _():
        wait_tile(t)
        o_ref[...] = buf[t % 2].astype(o_ref.dtype)

    @pl.when(t >= n_used)
    def _():
        o_ref[...] = jnp.zeros(o_ref.shape, o_ref.dtype)


def _gather_rows(h, slot_tok, n_used, tm):
    n_slots = slot_tok.shape[0]
    d = h.shape[1]
    return pl.pallas_call(
        functools.partial(_gather_kernel, tm=tm),
        grid_spec=pltpu.PrefetchScalarGridSpec(
            num_scalar_prefetch=2,
            grid=(n_slots // tm,),
            in_specs=[pl.BlockSpec(memory_space=pl.ANY)],
            out_specs=pl.BlockSpec((tm, d), lambda t, st, nu: (t, 0)),
            scratch_shapes=[pltpu.VMEM((2, tm, d), h.dtype), pltpu.SemaphoreType.DMA((2,))],
        ),
        out_shape=jax.ShapeDtypeStruct((n_slots, d), BF16),
        compiler_params=_cparams("arbitrary"),
        name="gather_rows",
    )(slot_tok, n_used, h)


def _stream_expert_weights(te_ref, nx_ref, grp_ref, meta_ref, copies, cast):
    j = pl.program_id(0)
    i = pl.program_id(1)
    first = jnp.logical_and(i < meta_ref[0],
                            jnp.logical_or(i == 0, te_ref[i] != te_ref[jnp.maximum(i - 1, 0)]))

    @pl.when(first)
    def _():
        run = j * meta_ref[1] + grp_ref[i]
        slot = run % 2

        @pl.when(run == 0)
        def _():
            for c in copies(te_ref[i], j, slot):
                c.start()

        for c in copies(te_ref[i], j, slot):
            c.wait()
        cast(slot)
        nxt = nx_ref[i]

        @pl.when(nxt >= 0)
        def _():
            for c in copies(nxt, j, 1 - slot):
                c.start()

        @pl.when(jnp.logical_and(nxt < 0, j + 1 < pl.num_programs(0)))
        def _():
            for c in copies(te_ref[0], j + 1, 1 - slot):
                c.start()


def _expert_gu_kernel(te_ref, nx_ref, grp_ref, meta_ref, x_ref, w_ref, bg_ref, bu_ref, o_ref,
                      wbuf, wg_scr, wu_scr, sem, *, tn, nj):
    i = pl.program_id(1)

    def copies(e, col, slot):
        return [pltpu.make_async_copy(w_ref.at[e, :, pl.ds(pl.multiple_of((half * nj + col) * tn, tn), tn)],
                                      wbuf.at[slot, half], sem.at[slot]) for half in range(2)]

    def cast(slot):
        wg_scr[...] = wbuf[slot, 0].astype(BF16)
        wu_scr[...] = wbuf[slot, 1].astype(BF16)

    _stream_expert_weights(te_ref, nx_ref, grp_ref, meta_ref, copies, cast)

    @pl.when(i < meta_ref[0])
    def _():
        x = x_ref[...]
        g = _dot(x, wg_scr[...]) + bg_ref[...]
        u = _dot(x, wu_scr[...]) + bu_ref[...]
        g = jnp.minimum(g, SWIGLU_LIMIT)
        u = jnp.clip(u, -SWIGLU_LIMIT, SWIGLU_LIMIT)
        o_ref[...] = ((u + 1.0) * g * jax.nn.sigmoid(SWIGLU_ALPHA * g)).astype(BF16)

    @pl.when(i >= meta_ref[0])
    def _():
        o_ref[...] = jnp.zeros(o_ref.shape, o_ref.dtype)


def _expert_gu(xs, w_gu, b_gu, sched, tm):
    n_slots, d = xs.shape
    de = w_gu.shape[2] // 2
    tn = 1024
    nj = de // tn
    row = lambda j, i, te, nx, gr, mt: jnp.minimum(i, mt[0] - 1)
    exp = lambda j, i, te, nx, gr, mt: te[jnp.minimum(i, mt[0] - 1)]
    b3 = b_gu.reshape(w_gu.shape[0], 1, 2 * de)
    return pl.pallas_call(
        functools.partial(_expert_gu_kernel, tn=tn, nj=nj),
        grid_spec=pltpu.PrefetchScalarGridSpec(
            num_scalar_prefetch=4,
            grid=(nj, n_slots // tm),
            in_specs=[
                pl.BlockSpec((tm, d), lambda j, i, *s: (row(j, i, *s), 0)),
                pl.BlockSpec(memory_space=pl.ANY),
                pl.BlockSpec((None, 1, tn), lambda j, i, *s: (exp(j, i, *s), 0, j)),
                pl.BlockSpec((None, 1, tn), lambda j, i, *s: (exp(j, i, *s), 0, nj + j)),
            ],
            out_specs=pl.BlockSpec((tm, tn), lambda j, i, *s: (i, j)),
            scratch_shapes=[pltpu.VMEM((2, 2, d, tn), F32), pltpu.VMEM((d, tn), BF16), pltpu.VMEM((d, tn), BF16),
                            pltpu.SemaphoreType.DMA((2,))],
        ),
        out_shape=jax.ShapeDtypeStruct((n_slots, de), BF16),
        compiler_params=_cparams("arbitrary", "arbitrary"),
        name="expert_gu",
    )(*sched, xs, w_gu, b3, b3)


def _expert_down_kernel(te_ref, nx_ref, grp_ref, meta_ref, a_ref, w_ref, b_ref, o_ref, wbuf, w_scr, sem, *, tn):
    i = pl.program_id(1)

    def copies(e, col, slot):
        return [pltpu.make_async_copy(w_ref.at[e, :, pl.ds(pl.multiple_of(col * tn, tn), tn)],
                                      wbuf.at[slot], sem.at[slot])]

    def cast(slot):
        w_scr[...] = wbuf[slot].astype(BF16)

    _stream_expert_weights(te_ref, nx_ref, grp_ref, meta_ref, copies, cast)

    @pl.when(i < meta_ref[0])
    def _():
        o_ref[...] = _dot(a_ref[...], w_scr[...]) + b_ref[...]

    @pl.when(i >= meta_ref[0])
    def _():
        o_ref[...] = jnp.zeros(o_ref.shape, o_ref.dtype)


def _expert_down(act, w_down, b_down, sched, tm):
    n_slots, de = act.shape
    d = w_down.shape[2]
    tn = 1024
    nj = d // tn
    row = lambda j, i, te, nx, gr, mt: jnp.minimum(i, mt[0] - 1)
    exp = lambda j, i, te, nx, gr, mt: te[jnp.minimum(i, mt[0] - 1)]
    return pl.pallas_call(
        functools.partial(_expert_down_kernel, tn=tn),
        grid_spec=pltpu.PrefetchScalarGridSpec(
            num_scalar_prefetch=4,
            grid=(nj, n_slots // tm),
            in_specs=[
                pl.BlockSpec((tm, de), lambda j, i, *s: (row(j, i, *s), 0)),
                pl.BlockSpec(memory_space=pl.ANY),
                pl.BlockSpec((None, 1, tn), lambda j, i, *s: (exp(j, i, *s), 0, j)),
            ],
            out_specs=pl.BlockSpec((tm, tn), lambda j, i, *s: (i, j)),
            scratch_shapes=[pltpu.VMEM((2, de, tn), F32), pltpu.VMEM((de, tn), BF16),
                            pltpu.SemaphoreType.DMA((2,))],
        ),
        out_shape=jax.ShapeDtypeStruct((n_slots, d), F32),
        compiler_params=_cparams("arbitrary", "arbitrary"),
        name="expert_down",
    )(*sched, act, w_down, b_down.reshape(w_down.shape[0], 1, d))


def _combine_kernel(dest_ref, y_ref, gate_ref, x_ref, g2_ref, gfin_ref, o_ref, ybuf, sem, *, tt, tok0, final):
    g_ = pl.program_id(0)
    i = pl.program_id(1)
    base = (tok0 + (g_ * pl.num_programs(1) + i) * tt) * TOP_K

    def row_copy(r, k):
        slot = dest_ref[base + r * TOP_K + k]
        return pltpu.make_async_copy(y_ref.at[pl.ds(slot, 1)], ybuf.at[k, pl.ds(r, 1)], sem)

    def start(r, c):
        for k in range(TOP_K):
            row_copy(r, k).start()
        return c

    def wait(r, c):
        for k in range(TOP_K):
            row_copy(r, k).wait()
        return c

    lax.fori_loop(0, tt, start, 0)
    lax.fori_loop(0, tt, wait, 0)
    gates = gate_ref[...]
    moe = gates[:, 0:1] * ybuf[0]
    for k in range(1, TOP_K):
        moe = moe + gates[:, k:k + 1] * ybuf[k]
    x2 = x_ref[...] + g2_ref[...] * moe
    if final:
        x2 = x2 * lax.rsqrt(jnp.mean(x2 * x2, axis=-1, keepdims=True) + RMS_EPS) * gfin_ref[...]
    o_ref[...] = x2


def _combine(y, dest, gates, x1, mod, g_final, *, tt, tok0, final):
    G, R, D = x1.shape
    rs = mod.shape[1]
    rblk = 1 if rs == 1 else tt
    ridx = (lambda i: 0) if rs == 1 else (lambda i: i)
    kern = functools.partial(_combine_kernel, tt=tt, tok0=tok0, final=final)
    return pl.pallas_call(
        kern,
        grid_spec=pltpu.PrefetchScalarGridSpec(
            num_scalar_prefetch=1,
            grid=(G, R // tt),
            in_specs=[
                pl.BlockSpec(memory_space=pl.ANY),
                pl.BlockSpec((None, tt, LANES), lambda g_, i, d_: (g_, i, 0)),
                pl.BlockSpec((None, tt, D), lambda g_, i, d_: (g_, i, 0)),
                pl.BlockSpec((None, rblk, D), lambda g_, i, d_: (g_, ridx(i), 5)),
                pl.BlockSpec((1, D), lambda g_, i, d_: (0, 0)),
            ],
            out_specs=pl.BlockSpec((None, tt, D), lambda g_, i, d_: (g_, i, 0)),
            scratch_shapes=[pltpu.VMEM((TOP_K, tt, D), F32), pltpu.SemaphoreType.DMA],
        ),
        out_shape=jax.ShapeDtypeStruct((G, R, D), F32),
        compiler_params=_cparams("arbitrary", "arbitrary"),
        name="combine",
    )(dest, y, gates, x1, mod, g_final)


def _row_tile(rows, cap):
    return max(t for t in range(MOBA_BLOCK, cap + 1, MOBA_BLOCK) if rows % t == 0)


def _layer_weights(l, w_in, b_forget, w_br_moba, w_br_sb, w_br_fox, w_out, w_router, b_router):
    d = w_in.shape[1]
    w_l = w_in[l]
    w_main = jnp.concatenate([w_l[:, :3 * D_ATTN], w_l[:, 3 * D_ATTN + H_FOX:]], axis=1).astype(BF16)
    w_f = jnp.pad(w_l[:, 3 * D_ATTN:3 * D_ATTN + H_FOX], ((0, 0), (0, LANES - H_FOX)))
    b_f = jnp.pad(b_forget[l], (0, LANES - H_FOX)).reshape(1, LANES)
    w_r = jnp.pad(w_router[l], ((0, 0), (0, LANES - N_EXPERTS)))
    b_r = jnp.pad(b_router[l], (0, LANES - N_EXPERTS), constant_values=NEG_INF).reshape(1, LANES)
    return dict(w_main=w_main, w_f=w_f, b_f=b_f, w_r=w_r, b_r=b_r,
                w_m=w_br_moba[l].astype(BF16), w_s=w_br_sb[l].astype(BF16), w_x=w_br_fox[l].astype(BF16),
                w_o=w_out[l].astype(BF16))


def kernel(x_prompt, x_sample, c_prompt, c_sample, cache_k, cache_v, cache_log_f, page_table, w_mod, b_mod,
           g_attn, w_in, b_forget, w_br_moba, w_br_sb, w_br_fox, w_out, g_ffn, w_router, b_router, w_gu, b_gu,
           w_down, b_down, g_final):
    depth = w_mod.shape[0]
    B, T, D = x_prompt.shape
    nb, tq, _ = x_sample.shape
    n_pool = cache_k.shape[1]
    n_prompt = B * T
    n_samp = nb * tq
    tm = EXPERT_TILE

    clf = jnp.pad(cache_log_f, ((0, 0), (0, 0), (0, 0), (HPG - H_FOX, 0))).reshape(
        depth, n_pool, PAGE // TOK_PER_BLK, LANES)
    c_all = jnp.pad(jnp.concatenate([c_prompt, c_sample], axis=0), ((0, 16 - B - nb), (0, 0)))
    wgu = w_gu.reshape(depth * N_EXPERTS, D, w_gu.shape[-1])
    bgu = b_gu.reshape(depth * N_EXPERTS, b_gu.shape[-1])
    wdn = w_down.reshape(depth * N_EXPERTS, w_down.shape[2], D)
    bdn = b_down.reshape(depth * N_EXPERTS, D)
    g_fin = g_final.reshape(1, D)

    xp = x_prompt
    xs = x_sample.reshape(1, n_samp, D)
    kp = jnp.zeros((depth, B, T, D), F32)
    vp = jnp.zeros((depth, B, T, D), F32)
    ks = jnp.zeros((depth, 1, n_samp, D), F32)
    vs = jnp.zeros((depth, 1, n_samp, D), F32)
    fp, fs = [], []
    for l in range(depth):
        lw = _layer_weights(l, w_in, b_forget, w_br_moba, w_br_sb, w_br_fox, w_out, w_router, b_router)
        g_a = g_attn[l].reshape(1, D)
        g_f = g_ffn[l].reshape(1, D)
        mod = _mod(c_all, w_mod, b_mod, l)
        mod_p = mod[:B].reshape(B, 1, 6 * D)
        mod_s = jnp.repeat(mod[B:B + nb], tq, axis=0).reshape(1, n_samp, 6 * D)

        qg_p, kp, vp, lf_p = _ln_proj(xp, mod_p, g_a, lw["w_main"], lw["w_f"], lw["b_f"], kp, vp, l,
                                      tr=_row_tile(T, 1024))
        fc, fr = _fcum(lf_p)
        x1p = _merge(_moba(qg_p, kp, vp, l), _sb(qg_p, kp, vp, l), _fox(qg_p, kp, vp, fc, fr, l), qg_p, xp, mod_p,
                     lw["w_m"], lw["w_s"], lw["w_x"], lw["w_o"], tr=256)
        qg_s, ks, vs, lf_s = _ln_proj(xs, mod_s, g_a, lw["w_main"], lw["w_f"], lw["b_f"], ks, vs, l, tr=n_samp)
        o_s = _decode_attn(qg_s[0, 0], ks[l, 0], vs[l, 0], lf_s, cache_k, cache_v, clf, page_table, l)
        o_s = o_s.reshape(1, n_samp, D)
        n_m, n_s = H_MOBA * HEAD_DIM, H_SB * HEAD_DIM
        x1s = _merge(o_s[..., :n_m], o_s[..., n_m:n_m + n_s], o_s[..., n_m + n_s:], qg_s, xs, mod_s,
                     lw["w_m"], lw["w_s"], lw["w_x"], lw["w_o"], tr=n_samp)

        h2p, idx_p, gate_p = _norm_router(x1p, mod_p, g_f, lw["w_r"], lw["b_r"], tr=512)
        h2s, idx_s, gate_s = _norm_router(x1s, mod_s, g_f, lw["w_r"], lw["b_r"], tr=n_samp)
        h2 = jnp.concatenate([h2p.reshape(n_prompt, D), h2s.reshape(n_samp, D)], axis=0)
        top_idx = jnp.concatenate([idx_p.reshape(n_prompt, LANES)[:, :TOP_K],
                                   idx_s.reshape(n_samp, LANES)[:, :TOP_K]], axis=0)
        dest, slot_tok, tile_e, next_e, grp, meta = _route(top_idx, tm)
        rows = _gather_rows(h2, slot_tok, meta, tm)
        sched = (tile_e + l * N_EXPERTS, jnp.where(next_e >= 0, next_e + l * N_EXPERTS, -1), grp, meta)
        act = _expert_gu(rows, wgu, bgu, sched, tm)
        y = _expert_down(act, wdn, bdn, sched, tm)
        final = l == depth - 1
        xp = _combine(y, dest, gate_p, x1p, mod_p, g_fin, tt=256, tok0=0, final=final)
        xs = _combine(y, dest, gate_s, x1s, mod_s, g_fin, tt=n_samp, tok0=n_prompt, final=final)

        fp.append(lf_p[..., :H_FOX])
        fs.append(lf_s[..., :H_FOX])

    hd = (N_HEADS, HEAD_DIM)
    return (xp, xs.reshape(nb, tq, D),
            kp.reshape(depth, B, T, *hd), vp.reshape(depth, B, T, *hd), jnp.stack(fp),
            ks.reshape(depth, nb, tq, *hd), vs.reshape(depth, nb, tq, *hd),
            jnp.stack(fs).reshape(depth, nb, tq, H_FOX))
```

```python
import functools

import jax
import jax.numpy as jnp
from jax import lax
from jax.experimental import pallas as pl
from jax.experimental.pallas import tpu as pltpu

F32 = jnp.float32
BF16 = jnp.bfloat16
I32 = jnp.int32

HEAD_DIM = 128
N_HEADS = 16
H_MOBA = 6
H_SB = 5
H_FOX = 5
D_ATTN = N_HEADS * HEAD_DIM
MOBA_BLOCK = 256
MOBA_TOPK = 3
N_EXPERTS = 32
TOP_K = 4
SWIGLU_ALPHA = 1.702
SWIGLU_LIMIT = 7.0
RMS_EPS = 1e-5
NEG_INF = -1e30
PAGE = 128
LANES = 128
VMEM_LIMIT_BYTES = 56 * 1024 * 1024
EXPERT_TILE = 256
MIXER_TQ = 256
MIXER_TK = 256


def _cparams(*sem):
    return pltpu.CompilerParams(dimension_semantics=sem, vmem_limit_bytes=VMEM_LIMIT_BYTES)


def _dot(a, b):
    return jnp.dot(a, b, preferred_element_type=F32)


def _dot_nt(a, b):
    return lax.dot_general(a, b, (((1,), (1,)), ((), ())), preferred_element_type=F32)


def _split2(x):
    hi = x.astype(BF16)
    lo = (x - hi.astype(F32)).astype(BF16)
    return hi, lo


def _split3(x):
    hi = x.astype(BF16)
    r = x - hi.astype(F32)
    mid = r.astype(BF16)
    lo = (r - mid.astype(F32)).astype(BF16)
    return hi, mid, lo


def _dot_acc(a, b):
    a_hi, a_lo = _split2(a)
    b_hi, b_lo = _split2(b)
    return _dot(a_hi, b_hi) + _dot(a_lo, b_hi) + _dot(a_hi, b_lo)


def _dot_nt_acc(a, b):
    a_hi, a_lo = _split2(a)
    b_hi, b_lo = _split2(b)
    return _dot_nt(a_hi, b_hi) + _dot_nt(a_lo, b_hi) + _dot_nt(a_hi, b_lo)


def _dot_exact_rhs(a, b_exact):
    hi, mid, lo = _split3(a)
    return _dot(hi, b_exact) + _dot(mid, b_exact) + _dot(lo, b_exact)


def _dot_exact_lhs(a_exact, b):
    hi, mid, lo = _split3(b)
    return _dot(a_exact, hi) + _dot(a_exact, mid) + _dot(a_exact, lo)


def _softplus_neg_abs(z):
    return jnp.log1p(jnp.exp(-jnp.abs(z)))


def _log_sigmoid(z):
    return -(jnp.maximum(-z, 0.0) + _softplus_neg_abs(z))


def _rms_modulate(x, g, scale, shift):
    y = x * lax.rsqrt(jnp.mean(x * x, axis=-1, keepdims=True) + RMS_EPS)
    return (y * g) * (1.0 + scale) + shift


def _mod_kernel(c_ref, w_ref, b_ref, o_ref):
    c = c_ref[...]
    o_ref[...] = _dot_acc(c * jax.nn.sigmoid(c), w_ref[...]) + b_ref[...]


def _mod(c, w, b, layer):
    m, d = c.shape
    depth, _, n = w.shape
    tn = 1024
    return pl.pallas_call(
        _mod_kernel,
        grid=(n // tn,),
        in_specs=[
            pl.BlockSpec((m, d), lambda j: (0, 0)),
            pl.BlockSpec((None, d, tn), lambda j: (layer, 0, j)),
            pl.BlockSpec((None, 1, tn), lambda j: (layer, 0, j)),
        ],
        out_specs=pl.BlockSpec((m, tn), lambda j: (0, j)),
        out_shape=jax.ShapeDtypeStruct((m, n), F32),
        compiler_params=_cparams("parallel"),
        name="mod",
    )(c, w, b.reshape(depth, 1, n))


def _ln_proj_kernel(x_ref, sh_ref, sc_ref, g_ref, w_ref, wf_ref, bf_ref, kprev_ref, vprev_ref,
                    qg_ref, k_ref, v_ref, lf_ref, h_scr, *, per):
    del kprev_ref, vprev_ref
    j = pl.program_id(2)

    @pl.when(j == 0)
    def _():
        h = _rms_modulate(x_ref[...], g_ref[...], sc_ref[...], sh_ref[...])
        h_scr[...] = h.astype(BF16)
        lf_ref[...] = _log_sigmoid(_dot_acc(h, wf_ref[...]) + bf_ref[...])

    acc = _dot(h_scr[...], w_ref[...])

    @pl.when(j < per)
    def _():
        qg_ref[...] = acc

    @pl.when(jnp.logical_and(j >= per, j < 2 * per))
    def _():
        k_ref[...] = acc

    @pl.when(jnp.logical_and(j >= 2 * per, j < 3 * per))
    def _():
        v_ref[...] = acc

    @pl.when(j >= 3 * per)
    def _():
        qg_ref[...] = jax.nn.sigmoid(acc)


def _ln_proj(x, mod, g, w_main, w_f, b_f, k_all, v_all, layer, *, tr):
    G, R, D = x.shape
    rs = mod.shape[1]
    tn = 512
    per = D // tn
    n_col = w_main.shape[1] // tn
    rblk = 1 if rs == 1 else tr
    ridx = (lambda i: 0) if rs == 1 else (lambda i: i)
    kern = functools.partial(_ln_proj_kernel, per=per)
    clamp = lambda j, lo: jnp.clip(j - lo, 0, per - 1)

    def qg_idx(g_, i, j):
        grp = jnp.where(j < 3 * per, 0, j // per - 2)
        col = jnp.where(j < per, j, jnp.where(j < 3 * per, per - 1, j % per))
        return (grp, g_, i, col)

    return pl.pallas_call(
        kern,
        grid=(G, R // tr, n_col),
        in_specs=[
            pl.BlockSpec((None, tr, D), lambda g_, i, j: (g_, i, 0)),
            pl.BlockSpec((None, rblk, D), lambda g_, i, j: (g_, ridx(i), 0)),
            pl.BlockSpec((None, rblk, D), lambda g_, i, j: (g_, ridx(i), 1)),
            pl.BlockSpec((1, D), lambda g_, i, j: (0, 0)),
            pl.BlockSpec((D, tn), lambda g_, i, j: (0, j)),
            pl.BlockSpec((D, LANES), lambda g_, i, j: (0, 0)),
            pl.BlockSpec((1, LANES), lambda g_, i, j: (0, 0)),
            pl.BlockSpec(memory_space=pl.ANY),
            pl.BlockSpec(memory_space=pl.ANY),
        ],
        out_specs=[
            pl.BlockSpec((None, None, tr, tn), qg_idx),
            pl.BlockSpec((None, None, tr, tn), lambda g_, i, j: (layer, g_, i, clamp(j, per))),
            pl.BlockSpec((None, None, tr, tn), lambda g_, i, j: (layer, g_, i, clamp(j, 2 * per))),
            pl.BlockSpec((None, tr, LANES), lambda g_, i, j: (g_, i, 0)),
        ],
        out_shape=[
            jax.ShapeDtypeStruct((n_col // per - 2, G, R, D), F32),
            jax.ShapeDtypeStruct(k_all.shape, F32),
            jax.ShapeDtypeStruct(v_all.shape, F32),
            jax.ShapeDtypeStruct((G, R, LANES), F32),
        ],
        input_output_aliases={7: 1, 8: 2},
        scratch_shapes=[pltpu.VMEM((tr, D), BF16)],
        compiler_params=_cparams("parallel", "parallel", "arbitrary"),
        name="ln_proj",
    )(x, mod, mod, g, w_main, w_f, b_f, k_all, v_all)


def _fcum_kernel(lf_ref, fc_ref, fr_ref):
    t = lf_ref.shape[0]
    r = lax.broadcasted_iota(I32, (t, t), 0)
    c = lax.broadcasted_iota(I32, (t, t), 1)
    tri = (r >= c).astype(BF16)
    f = _dot_exact_lhs(tri, lf_ref[...])
    fc_ref[...] = f
    fr_ref[...] = f.T[:8, :]


def _fcum(logf):
    B, T, _ = logf.shape
    return pl.pallas_call(
        _fcum_kernel,
        grid=(B,),
        in_specs=[pl.BlockSpec((None, T, LANES), lambda b: (b, 0, 0))],
        out_specs=[
            pl.BlockSpec((None, T, LANES), lambda b: (b, 0, 0)),
            pl.BlockSpec((None, 8, T), lambda b: (b, 0, 0)),
        ],
        out_shape=[jax.ShapeDtypeStruct((B, T, LANES), F32), jax.ShapeDtypeStruct((B, 8, T), F32)],
        compiler_params=_cparams("parallel"),
        name="fcum",
    )(logf)


def _head_specs(lead, head0, n_heads, rows, row_idx):
    return [
        pl.BlockSpec((None, None, rows, HEAD_DIM), lambda b, i, h=h: (lead, b, row_idx(i), head0 + h))
        for h in range(n_heads)
    ]


def _moba_kernel(*refs, n_heads, tq, n_kb):
    q_refs, k_refs, v_refs = refs[:n_heads], refs[n_heads:2 * n_heads], refs[2 * n_heads:3 * n_heads]
    o_ref, m_scr, l_scr, acc_scr, sel_scr, km_scr = refs[3 * n_heads:]
    i = pl.program_id(1)
    own_blk = (i * tq) // MOBA_BLOCK
    scale = HEAD_DIM ** -0.5
    n_sel = min(MOBA_TOPK, n_kb)
    lane = lax.broadcasted_iota(I32, (tq, LANES), 1)
    qpos = i * tq + lax.broadcasted_iota(I32, (tq, MOBA_BLOCK), 0)
    koff = lax.broadcasted_iota(I32, (tq, MOBA_BLOCK), 1)
    @pl.when(i == 0)
    def _():
        for h in range(n_heads):
            kmean = jnp.sum(k_refs[h][...].reshape(n_kb, MOBA_BLOCK, HEAD_DIM), axis=1) * (1.0 / MOBA_BLOCK)
            km_scr[h] = jnp.concatenate([kmean, jnp.zeros((LANES - n_kb, HEAD_DIM), F32)], axis=0)

    for h in range(n_heads):
        q = q_refs[h][...]
        gate = jnp.where(lane < own_blk, _dot_nt_acc(q, km_scr[h]), NEG_INF)
        sel = jnp.zeros((tq, LANES), jnp.bool_)
        for _ in range(n_sel):
            m = jnp.max(gate, axis=-1, keepdims=True)
            idx = jnp.min(jnp.where(gate == m, lane, LANES), axis=-1, keepdims=True)
            hit = lane == idx
            sel = jnp.logical_or(sel, hit)
            gate = jnp.where(hit, -jnp.inf, gate)
        sel_scr[h] = jnp.where(jnp.logical_and(sel, lane < own_blk), 1.0, 0.0)
        m_scr[h] = jnp.full((tq, 1), NEG_INF, F32)
        l_scr[h] = jnp.zeros((tq, 1), F32)
        acc_scr[h] = jnp.zeros((tq, HEAD_DIM), F32)

    def body(n, carry):
        start = pl.multiple_of(n * MOBA_BLOCK, MOBA_BLOCK)
        kpos = n * MOBA_BLOCK + koff
        dist = (qpos - kpos).astype(F32)
        own = (koff * 0 + n) == own_blk
        own_visible = jnp.logical_and(own, kpos <= qpos)
        not_own = jnp.logical_not(own)
        for h in range(n_heads):
            slope = 2.0 ** (-8.0 * (h + 1) / n_heads)
            qb = q_refs[h][...].astype(BF16)
            kn = k_refs[h][pl.ds(start, MOBA_BLOCK), :].astype(BF16)
            vn = v_refs[h][pl.ds(start, MOBA_BLOCK), :].astype(BF16)
            s = _dot_nt(qb, kn) * scale - slope * dist
            sel_n = jnp.sum(jnp.where(lane == n, sel_scr[h], 0.0), axis=-1, keepdims=True) > 0.5
            valid = jnp.logical_or(own_visible, jnp.logical_and(not_own, sel_n))
            s = jnp.where(valid, s, NEG_INF)
            m_run = m_scr[h]
            m_new = jnp.maximum(m_run, jnp.max(s, axis=-1, keepdims=True))
            alpha = jnp.exp(m_run - m_new)
            p = jnp.where(valid, jnp.exp(s - m_new), 0.0)
            l_scr[h] = alpha * l_scr[h] + jnp.sum(p, axis=-1, keepdims=True)
            acc_scr[h] = alpha * acc_scr[h] + _dot(p.astype(BF16), vn)
            m_scr[h] = m_new
        return carry

    lax.fori_loop(0, own_blk + 1, body, 0)
    for h in range(n_heads):
        o_ref[:, h * HEAD_DIM:(h + 1) * HEAD_DIM] = acc_scr[h] / l_scr[h]


def _mixer_scratch(n_heads, tq, n_col, n_wide):
    return ([pltpu.VMEM((n_heads, tq, 1), F32)] * n_col
            + [pltpu.VMEM((n_heads, tq, HEAD_DIM), F32)] * n_wide)


def _moba(qg, k_all, v_all, layer):
    _, B, T, _ = qg.shape
    tq = MIXER_TQ
    n_kb = T // MOBA_BLOCK
    kern = functools.partial(_moba_kernel, n_heads=H_MOBA, tq=tq, n_kb=n_kb)
    specs = (_head_specs(0, 0, H_MOBA, tq, lambda i: i) + _head_specs(layer, 0, H_MOBA, T, lambda i: 0)
             + _head_specs(layer, 0, H_MOBA, T, lambda i: 0))
    return pl.pallas_call(
        kern,
        grid=(B, T // tq),
        in_specs=specs,
        out_specs=pl.BlockSpec((None, tq, H_MOBA * HEAD_DIM), lambda b, i: (b, i, 0)),
        out_shape=jax.ShapeDtypeStruct((B, T, H_MOBA * HEAD_DIM), F32),
        scratch_shapes=_mixer_scratch(H_MOBA, tq, 2, 2) + [pltpu.VMEM((H_MOBA, LANES, HEAD_DIM), F32)],
        compiler_params=_cparams("parallel", "arbitrary"),
        name="moba",
    )(*([qg] * H_MOBA + [k_all] * H_MOBA + [v_all] * H_MOBA))


def _strict_upper(n):
    j = lax.broadcasted_iota(I32, (n, n), 0)
    s = lax.broadcasted_iota(I32, (n, n), 1)
    return (j > s).astype(BF16)


def _sb_kernel(*refs, n_heads, tq, tk):
    q_refs, k_refs, v_refs = refs[:n_heads], refs[n_heads:2 * n_heads], refs[2 * n_heads:3 * n_heads]
    o_ref, c_scr, acc_scr = refs[3 * n_heads:]
    i = pl.program_id(1)
    scale = HEAD_DIM ** -0.5
    qpos = i * tq + lax.broadcasted_iota(I32, (tq, tk), 0)
    koff = lax.broadcasted_iota(I32, (tq, tk), 1)
    after = _strict_upper(tk)
    c_scr[...] = jnp.zeros(c_scr.shape, F32)
    acc_scr[...] = jnp.zeros(acc_scr.shape, F32)

    last = (i * tq) // tk

    def body(t, carry):
        n = last - t
        start = pl.multiple_of(n * tk, tk)
        strict = (n * tk + koff) < qpos
        for h in range(n_heads):
            qb = q_refs[h][...].astype(BF16)
            kn = k_refs[h][pl.ds(start, tk), :].astype(BF16)
            vn = v_refs[h][pl.ds(start, tk), :].astype(BF16)
            z = _dot_nt(qb, kn) * scale
            sp = _softplus_neg_abs(z)
            log_beta = -(jnp.maximum(-z, 0.0) + sp)
            log_keep = jnp.where(strict, -(jnp.maximum(z, 0.0) + sp), 0.0)
            c_run = c_scr[h]
            between = _dot_exact_rhs(log_keep, after) + c_run
            w = jnp.where(strict, jnp.exp(log_beta + between), 0.0)
            acc_scr[h] = acc_scr[h] + _dot(w.astype(BF16), vn)
            c_scr[h] = c_run + jnp.sum(log_keep, axis=-1, keepdims=True)
        return carry

    lax.fori_loop(0, last + 1, body, 0)
    for h in range(n_heads):
        o_ref[:, h * HEAD_DIM:(h + 1) * HEAD_DIM] = acc_scr[h]


def _sb(qg, k_all, v_all, layer):
    _, B, T, _ = qg.shape
    tq, tk = MIXER_TQ, MIXER_TK
    kern = functools.partial(_sb_kernel, n_heads=H_SB, tq=tq, tk=tk)
    specs = (_head_specs(0, H_MOBA, H_SB, tq, lambda i: i) + _head_specs(layer, H_MOBA, H_SB, T, lambda i: 0)
             + _head_specs(layer, H_MOBA, H_SB, T, lambda i: 0))
    return pl.pallas_call(
        kern,
        grid=(B, T // tq),
        in_specs=specs,
        out_specs=pl.BlockSpec((None, tq, H_SB * HEAD_DIM), lambda b, i: (b, i, 0)),
        out_shape=jax.ShapeDtypeStruct((B, T, H_SB * HEAD_DIM), F32),
        scratch_shapes=_mixer_scratch(H_SB, tq, 1, 1),
        compiler_params=_cparams("parallel", "arbitrary"),
        name="sb",
    )(*([qg] * H_SB + [k_all] * H_SB + [v_all] * H_SB))


def _fox_kernel(*refs, n_heads, tq, tk):
    q_refs, k_refs, v_refs = refs[:n_heads], refs[n_heads:2 * n_heads], refs[2 * n_heads:3 * n_heads]
    fc_ref, fr_ref, o_ref, m_scr, l_scr, acc_scr = refs[3 * n_heads:]
    i = pl.program_id(1)
    scale = HEAD_DIM ** -0.5
    qpos = i * tq + lax.broadcasted_iota(I32, (tq, tk), 0)
    koff = lax.broadcasted_iota(I32, (tq, tk), 1)
    m_scr[...] = jnp.full(m_scr.shape, NEG_INF, F32)
    l_scr[...] = jnp.zeros(l_scr.shape, F32)
    acc_scr[...] = jnp.zeros(acc_scr.shape, F32)

    def body(n, carry):
        start = pl.multiple_of(n * tk, tk)
        valid = (n * tk + koff) <= qpos
        for h in range(n_heads):
            qb = q_refs[h][...].astype(BF16)
            kn = k_refs[h][pl.ds(start, tk), :].astype(BF16)
            vn = v_refs[h][pl.ds(start, tk), :].astype(BF16)
            fq = fc_ref[:, h:h + 1]
            fk = fr_ref[h:h + 1, pl.ds(start, tk)]
            s = jnp.where(valid, _dot_nt(qb, kn) * scale + (fq - fk), NEG_INF)
            m_run = m_scr[h]
            m_new = jnp.maximum(m_run, jnp.max(s, axis=-1, keepdims=True))
            alpha = jnp.exp(m_run - m_new)
            p = jnp.where(valid, jnp.exp(s - m_new), 0.0)
            l_scr[h] = alpha * l_scr[h] + jnp.sum(p, axis=-1, keepdims=True)
            acc_scr[h] = alpha * acc_scr[h] + _dot(p.astype(BF16), vn)
            m_scr[h] = m_new
        return carry

    lax.fori_loop(0, (i * tq) // tk + 1, body, 0)
    for h in range(n_heads):
        o_ref[:, h * HEAD_DIM:(h + 1) * HEAD_DIM] = acc_scr[h] / l_scr[h]


def _fox(qg, k_all, v_all, fc, fr, layer):
    _, B, T, _ = qg.shape
    tq, tk = MIXER_TQ, MIXER_TK
    h0 = H_MOBA + H_SB
    kern = functools.partial(_fox_kernel, n_heads=H_FOX, tq=tq, tk=tk)
    specs = (_head_specs(0, h0, H_FOX, tq, lambda i: i) + _head_specs(layer, h0, H_FOX, T, lambda i: 0)
             + _head_specs(layer, h0, H_FOX, T, lambda i: 0)
             + [pl.BlockSpec((None, tq, LANES), lambda b, i: (b, i, 0)),
                pl.BlockSpec((None, 8, T), lambda b, i: (b, 0, 0))])
    return pl.pallas_call(
        kern,
        grid=(B, T // tq),
        in_specs=specs,
        out_specs=pl.BlockSpec((None, tq, H_FOX * HEAD_DIM), lambda b, i: (b, i, 0)),
        out_shape=jax.ShapeDtypeStruct((B, T, H_FOX * HEAD_DIM), F32),
        scratch_shapes=_mixer_scratch(H_FOX, tq, 2, 1),
        compiler_params=_cparams("parallel", "arbitrary"),
        name="fox",
    )(*([qg] * H_FOX + [k_all] * H_FOX + [v_all] * H_FOX), fc, fr)


HPG = 8
TOK_PER_BLK = LANES // HPG


def _lane_class_mats():
    j = lax.broadcasted_iota(I32, (LANES, LANES), 0)
    s = lax.broadcasted_iota(I32, (LANES, LANES), 1)
    same = (j & 7) == (s & 7)
    after = jnp.logical_and(same, (j >> 3) > (s >> 3)).astype(BF16)
    return after, same.astype(BF16)


def _scan_blocked(xb, rows, nblk, carry, after, same):
    within = _dot_exact_rhs(xb, after)
    tot = _dot_exact_rhs(xb, same)
    out = [None] * nblk
    run = carry
    for b in range(nblk - 1, -1, -1):
        out[b] = within[b * rows:(b + 1) * rows] + run
        run = run + tot[b * rows:(b + 1) * rows]
    return out, run


def _decode_kernel(pt_ref, q_ref, kn_ref, vn_ref, lfn_ref, lfnb_ref, kb_ref, vb_ref, lfb_ref,
                   ka_ref, va_ref, lfa_ref, o_ref,
                   mb_m, mb_l, mb_g, mb_o, own_m, own_l, own_o, sb_acc0, sb_acc1, sb_c0, sb_c1,
                   fx_m, fx_l, fx_acc, fx_c, a_col, *, n_pages, tq):
    j = pl.program_id(1)
    page_b = n_pages - 1 - 2 * j
    t_past = n_pages * PAGE
    scale = HEAD_DIM ** -0.5
    n_rows = HPG * tq
    r_mb = H_MOBA * tq
    r_sb1 = (H_MOBA + H_SB - HPG) * tq
    after, same = _lane_class_mats()
    lane1 = lax.broadcasted_iota(I32, (tq, LANES), 1)

    def lane_blocks(x):
        return [x[:, b * LANES:(b + 1) * LANES] for b in range(x.shape[1] // LANES)]

    def process(k_ref, v_ref, lfblk, page, is_new):
        t_tok = k_ref.shape[0]
        nblk = t_tok // TOK_PER_BLK
        w = t_tok * HPG
        lane = lax.broadcasted_iota(I32, (n_rows, w), 1)
        row = lax.broadcasted_iota(I32, (n_rows, w), 0)
        tpos = lane >> 3
        r = row & 7
        mine = (lane & 7) == (row >> 3)
        if is_new:
            vis = jnp.logical_and(mine, tpos <= r)
            vis_strict = jnp.logical_and(mine, tpos < r)
            dist = (r - tpos).astype(F32)
        else:
            vis = vis_strict = mine
            dist = (t_past + r - (page * PAGE + tpos)).astype(F32)
        rowc = lax.broadcasted_iota(I32, (n_rows, 1), 0) >> 3
        slope = jnp.zeros((n_rows, 1), F32)
        for h in range(H_MOBA):
            slope = jnp.where(rowc == h, 2.0 ** (-8.0 * (h + 1) / H_MOBA), slope)

        s_g, v_g = [], []
        for g in range(2):
            kg = k_ref[:, g * HPG:(g + 1) * HPG, :].reshape(w, HEAD_DIM).astype(BF16)
            v_g.append(v_ref[:, g * HPG:(g + 1) * HPG, :].reshape(w, HEAD_DIM).astype(BF16))
            s_g.append(_dot_nt(q_ref[g].astype(BF16), kg) * scale)

        def own_slot(c, row0, rows):
            mine1 = (lax.broadcasted_iota(I32, (rows, LANES), 1) & 7) == (
                (row0 + lax.broadcasted_iota(I32, (rows, LANES), 0)) >> 3)
            return jnp.max(jnp.where(mine1, c, -jnp.inf), axis=-1, keepdims=True)

        def sb_weights(z, visible, c_ref, row0):
            rows = z.shape[0]
            sp = _softplus_neg_abs(z)
            log_beta = -(jnp.maximum(-z, 0.0) + sp)
            log_keep = jnp.where(visible, -(jnp.maximum(z, 0.0) + sp), 0.0)
            carry = c_ref[...]
            between, total = _scan_blocked(jnp.concatenate(lane_blocks(log_keep), axis=0), rows, nblk,
                                           jnp.zeros((rows, LANES), F32), after, same)
            c_ref[...] = carry + total
            ws = [jnp.where(vb, jnp.exp(lb + bt), 0.0)
                  for vb, lb, bt in zip(lane_blocks(visible), lane_blocks(log_beta), between)]
            return jnp.concatenate(ws, axis=1), jnp.exp(own_slot(carry, row0, rows))

        s = jnp.where(vis[:r_mb], s_g[0][:r_mb] - slope[:r_mb] * dist[:r_mb], NEG_INF)
        m = jnp.max(s, axis=-1, keepdims=True)
        p_mb = jnp.exp(s - m)
        l = jnp.sum(p_mb, axis=-1, keepdims=True)
        w_sb0, kept0 = sb_weights(s_g[0][r_mb:], vis_strict[r_mb:], sb_c0, r_mb)
        pv0 = _dot(jnp.concatenate([p_mb, w_sb0], axis=0).astype(BF16), v_g[0])
        sb_acc0[...] = sb_acc0[...] + kept0 * pv0[r_mb:]
        if is_new:
            own_m[...] = m
            own_l[...] = l
            own_o[...] = pv0[:r_mb]
        else:
            ksum = jnp.sum(k_ref[:, 0:HPG, :], axis=0)
            ksum_rows = jnp.concatenate(
                [jnp.broadcast_to(ksum[h:h + 1, :], (tq, HEAD_DIM)) for h in range(H_MOBA)], axis=0)
            g_col = jnp.sum(q_ref[0][:r_mb] * ksum_rows, axis=-1, keepdims=True)
            here = lane1 == page
            for h in range(H_MOBA):
                rs = slice(h * tq, (h + 1) * tq)
                mb_m[h] = jnp.where(here, m[rs], mb_m[h])
                mb_l[h] = jnp.where(here, l[rs], mb_l[h])
                mb_g[h] = jnp.where(here, g_col[rs], mb_g[h])
                for rr in range(tq):
                    mb_o[h, rr, pl.ds(page, 1), :] = pv0[h * tq + rr:h * tq + rr + 1, :]

        w_sb1, kept1 = sb_weights(s_g[1][:r_sb1], vis_strict[:r_sb1], sb_c1, 0)
        n_fx = n_rows - r_sb1
        carry = fx_c[...]
        bias_blk, total = _scan_blocked(lfblk, 1, nblk, jnp.zeros((1, LANES), F32), after, same)
        fx_c[...] = carry + total
        bias = jnp.concatenate([jnp.broadcast_to(bb, (n_fx, LANES)) for bb in bias_blk], axis=1)
        s = jnp.where(vis[r_sb1:], s_g[1][r_sb1:] + bias + a_col[r_sb1:], NEG_INF)
        m_loc = jnp.max(s, axis=-1, keepdims=True)
        p_fx = jnp.exp(s - m_loc)
        pv1 = _dot(jnp.concatenate([w_sb1, p_fx], axis=0).astype(BF16), v_g[1])
        sb_acc1[...] = sb_acc1[...] + kept1 * pv1[:r_sb1]
        m_page = m_loc + own_slot(carry, r_sb1, n_fx)
        m_run = fx_m[...]
        m_new = jnp.maximum(m_run, m_page)
        alpha = jnp.exp(m_run - m_new)
        beta = jnp.exp(m_page - m_new)
        fx_l[...] = alpha * fx_l[...] + beta * jnp.sum(p_fx, axis=-1, keepdims=True)
        fx_acc[...] = alpha * fx_acc[...] + beta * pv1[r_sb1:]
        fx_m[...] = m_new

    @pl.when(j == 0)
    def _():
        mb_m[...] = jnp.full(mb_m.shape, NEG_INF, F32)
        mb_l[...] = jnp.zeros(mb_l.shape, F32)
        mb_g[...] = jnp.full(mb_g.shape, NEG_INF, F32)
        mb_o[...] = jnp.zeros(mb_o.shape, F32)
        sb_acc0[...] = jnp.zeros(sb_acc0.shape, F32)
        sb_acc1[...] = jnp.zeros(sb_acc1.shape, F32)
        sb_c0[...] = jnp.zeros(sb_c0.shape, F32)
        sb_c1[...] = jnp.zeros(sb_c1.shape, F32)
        fx_m[...] = jnp.full(fx_m.shape, NEG_INF, F32)
        fx_l[...] = jnp.zeros(fx_l.shape, F32)
        fx_acc[...] = jnp.zeros(fx_acc.shape, F32)
        lfn = lfn_ref[...]
        row1 = lax.broadcasted_iota(I32, (tq, LANES), 0)
        a_new = jnp.zeros((tq, LANES), F32)
        for rr in range(tq):
            a_new = a_new + jnp.where(row1 >= rr, lfn[rr:rr + 1, :], 0.0)
        a_col[...] = jnp.concatenate([jnp.zeros((r_sb1, 1), F32)]
                                     + [a_new[:, hh:hh + 1] for hh in range(H_FOX)], axis=0)
        lfnb = lfnb_ref[...]
        fx_c[...] = -_dot_exact_rhs(lfnb, same)
        process(kn_ref, vn_ref, lfnb, None, True)

    process(kb_ref, vb_ref, lfb_ref[...], page_b, False)
    process(ka_ref, va_ref, lfa_ref[...], page_b - 1, False)

    @pl.when(j == n_pages // 2 - 1)
    def _():
        pj = lax.broadcasted_iota(I32, (LANES, LANES), 0)
        ps = lax.broadcasted_iota(I32, (LANES, LANES), 1)
        pair = ((pj >> 1) == (ps >> 1)).astype(BF16)
        n_kb = n_pages * PAGE // MOBA_BLOCK + 1
        for h in range(H_MOBA):
            rs = slice(h * tq, (h + 1) * tq)
            gate = jnp.where(lane1 < n_pages, _dot_exact_rhs(mb_g[h], pair), NEG_INF)
            sel = jnp.zeros((tq, LANES), jnp.bool_)
            for _ in range(min(MOBA_TOPK, n_kb)):
                m = jnp.max(gate, axis=-1, keepdims=True)
                idx = jnp.min(jnp.where(gate == m, lane1, LANES), axis=-1, keepdims=True)
                hit = (lane1 >> 1) == (idx >> 1)
                sel = jnp.logical_or(sel, hit)
                gate = jnp.where(hit, -jnp.inf, gate)
            sel = jnp.logical_and(sel, lane1 < n_pages)
            m_pg = mb_m[h]
            m_own = own_m[rs]
            m_fin = jnp.maximum(jnp.max(jnp.where(sel, m_pg, NEG_INF), axis=-1, keepdims=True), m_own)
            wp = jnp.where(sel, jnp.exp(m_pg - m_fin), 0.0)
            w_own = jnp.exp(m_own - m_fin)
            l_fin = jnp.sum(wp * mb_l[h], axis=-1, keepdims=True) + w_own * own_l[rs]
            rows = [_dot_acc(wp[rr:rr + 1, :n_pages], mb_o[h, rr]) for rr in range(tq)]
            o_fin = jnp.concatenate(rows, axis=0) + w_own * own_o[rs]
            o_ref[0, rs, :] = o_fin / l_fin
        o_ref[0, r_mb:, :] = sb_acc0[...]
        o_ref[1, :r_sb1, :] = sb_acc1[...]
        o_ref[1, r_sb1:, :] = fx_acc[...] / fx_l[...]


def _decode_attn(q_s, k_s, v_s, logf_s, cache_k, cache_v, cache_lfb, page_table, layer):
    nb, n_pages = page_table.shape
    d = q_s.shape[-1]
    tq = q_s.shape[0] // nb
    assert tq == HPG and N_HEADS == 2 * HPG and n_pages % 2 == 0 and n_pages <= LANES
    slot0 = H_MOBA + H_SB - HPG
    n_rows = HPG * tq
    qg = q_s.reshape(nb, tq, 2, HPG, HEAD_DIM).transpose(0, 2, 3, 1, 4).reshape(nb, 2, n_rows, HEAD_DIM)
    pad_tok = ((0, 0), (0, TOK_PER_BLK - tq), (0, 0), (0, 0))
    knew = jnp.pad(k_s.reshape(nb, tq, N_HEADS, HEAD_DIM), pad_tok)
    vnew = jnp.pad(v_s.reshape(nb, tq, N_HEADS, HEAD_DIM), pad_tok)
    lfnb = jnp.pad(logf_s[0, :, :H_FOX].reshape(nb, tq, H_FOX),
                   ((0, 0), (0, TOK_PER_BLK - tq), (slot0, 0))).reshape(nb, 1, LANES)
    kern = functools.partial(_decode_kernel, n_pages=n_pages, tq=tq)
    page_spec = lambda back: pl.BlockSpec(
        (None, None, PAGE, N_HEADS, HEAD_DIM), lambda b, j, pt: (layer, pt[b, n_pages - 1 - 2 * j - back], 0, 0, 0))
    lf_spec = lambda back: pl.BlockSpec(
        (None, None, PAGE // TOK_PER_BLK, LANES), lambda b, j, pt: (layer, pt[b, n_pages - 1 - 2 * j - back], 0, 0))
    r_mb = H_MOBA * tq
    r_sb1 = slot0 * tq
    vm = pltpu.VMEM
    out = pl.pallas_call(
        kern,
        grid_spec=pltpu.PrefetchScalarGridSpec(
            num_scalar_prefetch=1,
            grid=(nb, n_pages // 2),
            in_specs=[
                pl.BlockSpec((None, 2, n_rows, HEAD_DIM), lambda b, j, pt: (b, 0, 0, 0)),
                pl.BlockSpec((None, TOK_PER_BLK, N_HEADS, HEAD_DIM), lambda b, j, pt: (b, 0, 0, 0)),
                pl.BlockSpec((None, TOK_PER_BLK, N_HEADS, HEAD_DIM), lambda b, j, pt: (b, 0, 0, 0)),
                pl.BlockSpec((None, tq, LANES), lambda b, j, pt: (0, b, 0)),
                pl.BlockSpec((None, 1, LANES), lambda b, j, pt: (b, 0, 0)),
                page_spec(0), page_spec(0), lf_spec(0),
                page_spec(1), page_spec(1), lf_spec(1),
            ],
            out_specs=pl.BlockSpec((None, 2, n_rows, HEAD_DIM), lambda b, j, pt: (b, 0, 0, 0)),
            scratch_shapes=[
                vm((H_MOBA, tq, LANES), F32), vm((H_MOBA, tq, LANES), F32), vm((H_MOBA, tq, LANES), F32),
                vm((H_MOBA, tq, n_pages, HEAD_DIM), F32),
                vm((r_mb, 1), F32), vm((r_mb, 1), F32), vm((r_mb, HEAD_DIM), F32),
                vm((n_rows - r_mb, HEAD_DIM), F32), vm((r_sb1, HEAD_DIM), F32),
                vm((n_rows - r_mb, LANES), F32), vm((r_sb1, LANES), F32),
                vm((n_rows - r_sb1, 1), F32), vm((n_rows - r_sb1, 1), F32), vm((n_rows - r_sb1, HEAD_DIM), F32),
                vm((1, LANES), F32), vm((n_rows, 1), F32),
            ],
        ),
        out_shape=jax.ShapeDtypeStruct((nb, 2, n_rows, HEAD_DIM), F32),
        compiler_params=_cparams("parallel", "arbitrary"),
        name="decode_attn",
    )(page_table, qg, knew, vnew, logf_s, lfnb, cache_k, cache_v, cache_lfb, cache_k, cache_v, cache_lfb)
    return out.reshape(nb, 2, HPG, tq, HEAD_DIM).transpose(0, 3, 1, 2, 4).reshape(nb * tq, d)


def _merge_kernel(om_ref, os_ref, of_ref, gm_ref, gs_ref, gf_ref, x_ref, g1_ref,
                  wm_ref, ws_ref, wf_ref, wo_ref, o_ref, mrg_scr):
    j = pl.program_id(2)

    @pl.when(j == 0)
    def _():
        merged = (gm_ref[...] * _dot(om_ref[...].astype(BF16), wm_ref[...])
                  + gs_ref[...] * _dot(os_ref[...].astype(BF16), ws_ref[...])
                  + gf_ref[...] * _dot(of_ref[...].astype(BF16), wf_ref[...]))
        mrg_scr[...] = merged.astype(BF16)

    o_ref[...] = x_ref[...] + g1_ref[...] * _dot(mrg_scr[...], wo_ref[...])


def _merge(o_moba, o_sb, o_fox, proj, x, mod, w_m, w_s, w_f, w_o, *, tr):
    G, R, D = x.shape
    rs = mod.shape[1]
    tn = 1024
    per = D // tn
    rblk = 1 if rs == 1 else tr
    ridx = (lambda i: 0) if rs == 1 else (lambda i: i)
    full = lambda a: pl.BlockSpec(a.shape, lambda g_, i, j: (0, 0))
    return pl.pallas_call(
        _merge_kernel,
        grid=(G, R // tr, per),
        in_specs=[
            pl.BlockSpec((None, tr, o_moba.shape[-1]), lambda g_, i, j: (g_, i, 0)),
            pl.BlockSpec((None, tr, o_sb.shape[-1]), lambda g_, i, j: (g_, i, 0)),
            pl.BlockSpec((None, tr, o_fox.shape[-1]), lambda g_, i, j: (g_, i, 0)),
            pl.BlockSpec((None, None, tr, D), lambda g_, i, j: (1, g_, i, 0)),
            pl.BlockSpec((None, None, tr, D), lambda g_, i, j: (2, g_, i, 0)),
            pl.BlockSpec((None, None, tr, D), lambda g_, i, j: (3, g_, i, 0)),
            pl.BlockSpec((None, tr, tn), lambda g_, i, j: (g_, i, j)),
            pl.BlockSpec((None, rblk, tn), lambda g_, i, j: (g_, ridx(i), 2 * per + j)),
            full(w_m), full(w_s), full(w_f),
            pl.BlockSpec((D, tn), lambda g_, i, j: (0, j)),
        ],
        out_specs=pl.BlockSpec((None, tr, tn), lambda g_, i, j: (g_, i, j)),
        out_shape=jax.ShapeDtypeStruct((G, R, D), F32),
        scratch_shapes=[pltpu.VMEM((tr, D), BF16)],
        compiler_params=_cparams("parallel", "parallel", "arbitrary"),
        name="merge",
    )(o_moba, o_sb, o_fox, proj, proj, proj, x, mod, w_m, w_s, w_f, w_o)


def _norm_router_kernel(x_ref, sh_ref, sc_ref, g_ref, wr_ref, br_ref, h_ref, idx_ref, gate_ref):
    h = _rms_modulate(x_ref[...], g_ref[...], sc_ref[...], sh_ref[...])
    h_ref[...] = h
    tr = h.shape[0]
    lane = lax.broadcasted_iota(I32, (tr, LANES), 1)
    logits = _dot_acc(h, wr_ref[...]) + br_ref[...]
    vals, idxs = [], []
    for _ in range(TOP_K):
        m = jnp.max(logits, axis=-1, keepdims=True)
        idx = jnp.min(jnp.where(logits == m, lane, LANES), axis=-1, keepdims=True)
        vals.append(m)
        idxs.append(idx)
        logits = jnp.where(lane == idx, -jnp.inf, logits)
    es = [jnp.exp(v - vals[0]) for v in vals]
    denom = es[0] + es[1] + es[2] + es[3]
    idx_out = jnp.zeros((tr, LANES), I32)
    gate_out = jnp.zeros((tr, LANES), F32)
    for k in range(TOP_K):
        idx_out = jnp.where(lane == k, idxs[k], idx_out)
        gate_out = jnp.where(lane == k, es[k] / denom, gate_out)
    idx_ref[...] = idx_out
    gate_ref[...] = gate_out


def _norm_router(x, mod, g, w_r, b_r, *, tr):
    G, R, D = x.shape
    rs = mod.shape[1]
    rblk = 1 if rs == 1 else tr
    ridx = (lambda i: 0) if rs == 1 else (lambda i: i)
    return pl.pallas_call(
        _norm_router_kernel,
        grid=(G, R // tr),
        in_specs=[
            pl.BlockSpec((None, tr, D), lambda g_, i: (g_, i, 0)),
            pl.BlockSpec((None, rblk, D), lambda g_, i: (g_, ridx(i), 3)),
            pl.BlockSpec((None, rblk, D), lambda g_, i: (g_, ridx(i), 4)),
            pl.BlockSpec((1, D), lambda g_, i: (0, 0)),
            pl.BlockSpec((D, LANES), lambda g_, i: (0, 0)),
            pl.BlockSpec((1, LANES), lambda g_, i: (0, 0)),
        ],
        out_specs=[
            pl.BlockSpec((None, tr, D), lambda g_, i: (g_, i, 0)),
            pl.BlockSpec((None, tr, LANES), lambda g_, i: (g_, i, 0)),
            pl.BlockSpec((None, tr, LANES), lambda g_, i: (g_, i, 0)),
        ],
        out_shape=[
            jax.ShapeDtypeStruct((G, R, D), F32),
            jax.ShapeDtypeStruct((G, R, LANES), I32),
            jax.ShapeDtypeStruct((G, R, LANES), F32),
        ],
        compiler_params=_cparams("parallel", "parallel"),
        name="norm_router",
    )(x, mod, mod, g, w_r, b_r)


def _route(top_idx, tm):
    n_tok = top_idx.shape[0]
    n_assign = n_tok * TOP_K
    n_tiles = -(-(n_assign + N_EXPERTS * (tm - 1)) // tm)
    flat_e = top_idx.reshape(-1)
    onehot = (flat_e[:, None] == jnp.arange(N_EXPERTS, dtype=I32)[None, :]).astype(I32)
    sizes = jnp.sum(onehot, axis=0)
    rank = jnp.sum(jnp.cumsum(onehot, axis=0) * onehot, axis=1) - 1
    padded = (sizes + tm - 1) // tm * tm
    pad_end = jnp.cumsum(padded)
    pad_start = pad_end - padded
    dest = (pad_start[flat_e] + rank).astype(I32)
    slot_tok = jnp.zeros((n_tiles * tm,), I32).at[dest].set(jnp.arange(n_assign, dtype=I32) // TOP_K)
    tile_start = jnp.arange(n_tiles, dtype=I32) * tm
    tile_e = jnp.minimum(jnp.sum((pad_end[None, :] <= tile_start[:, None]).astype(I32), axis=1), N_EXPERTS - 1)
    n_used = (pad_end[-1] // tm).astype(I32)
    tile_i = jnp.arange(n_tiles, dtype=I32)
    prev_e = jnp.concatenate([jnp.full((1,), -1, I32), tile_e[:-1]])
    first = jnp.logical_and(tile_i < n_used, tile_e != prev_e)
    grp = jnp.cumsum(first.astype(I32)) - 1
    next_start = pad_end[tile_e] // tm
    next_e = jnp.where(next_start < n_used, tile_e[jnp.minimum(next_start, n_tiles - 1)], -1).astype(I32)
    meta = jnp.stack([n_used, jnp.sum(first.astype(I32))]).astype(I32)
    return dest, slot_tok, tile_e, next_e, grp.astype(I32), meta


def _gather_kernel(slot_tok_ref, n_used_ref, h_ref, o_ref, buf, sem, *, tm):
    t = pl.program_id(0)
    n_used = n_used_ref[0]

    def row_copy(tile, r):
        slot = tile % 2
        tok = slot_tok_ref[tile * tm + r]
        return pltpu.make_async_copy(h_ref.at[pl.ds(tok, 1)], buf.at[slot, pl.ds(r, 1)], sem.at[slot])

    def start_tile(tile):
        def start(r, c):
            row_copy(tile, r).start()
            return c
        lax.fori_loop(0, tm, start, 0, unroll=8)

    def wait_tile(tile):
        def wait(r, c):
            row_copy(tile, r).wait()
            return c
        lax.fori_loop(0, tm, wait, 0, unroll=8)

    @pl.when(jnp.logical_and(t == 0, n_used > 0))
    def _():
        start_tile(t)

    @pl.when(t + 1 < n_used)
    def _():
        start_tile(t + 1)

    @pl.when(t < n_used)
    def _():
        wait_tile(t)
        o_ref[...] = buf[t % 2].astype(o_ref.dtype)

    @pl.when(t >= n_used)
    def _():
        o_ref[...] = jnp.zeros(o_ref.shape, o_ref.dtype)


def _gather_rows(h, slot_tok, n_used, tm):
    n_slots = slot_tok.shape[0]
    d = h.shape[1]
    return pl.pallas_call(
        functools.partial(_gather_kernel, tm=tm),
        grid_spec=pltpu.PrefetchScalarGridSpec(
            num_scalar_prefetch=2,
            grid=(n_slots // tm,),
            in_specs=[pl.BlockSpec(memory_space=pl.ANY)],
            out_specs=pl.BlockSpec((tm, d), lambda t, st, nu: (t, 0)),
            scratch_shapes=[pltpu.VMEM((2, tm, d), h.dtype), pltpu.SemaphoreType.DMA((2,))],
        ),
        out_shape=jax.ShapeDtypeStruct((n_slots, d), BF16),
        compiler_params=_cparams("arbitrary"),
        name="gather_rows",
    )(slot_tok, n_used, h)


def _stream_expert_weights(te_ref, nx_ref, grp_ref, meta_ref, copies, cast):
    j = pl.program_id(0)
    i = pl.program_id(1)
    first = jnp.logical_and(i < meta_ref[0],
                            jnp.logical_or(i == 0, te_ref[i] != te_ref[jnp.maximum(i - 1, 0)]))

    @pl.when(first)
    def _():
        run = j * meta_ref[1] + grp_ref[i]
        slot = run % 2

        @pl.when(run == 0)
        def _():
            for c in copies(te_ref[i], j, slot):
                c.start()

        for c in copies(te_ref[i], j, slot):
            c.wait()
        cast(slot)
        nxt = nx_ref[i]

        @pl.when(nxt >= 0)
        def _():
            for c in copies(nxt, j, 1 - slot):
                c.start()

        @pl.when(jnp.logical_and(nxt < 0, j + 1 < pl.num_programs(0)))
        def _():
            for c in copies(te_ref[0], j + 1, 1 - slot):
                c.start()


def _expert_gu_kernel(te_ref, nx_ref, grp_ref, meta_ref, x_ref, w_ref, bg_ref, bu_ref, o_ref,
                      wbuf, wg_scr, wu_scr, sem, *, tn, nj):
    i = pl.program_id(1)

    def copies(e, col, slot):
        return [pltpu.make_async_copy(w_ref.at[e, :, pl.ds(pl.multiple_of((half * nj + col) * tn, tn), tn)],
                                      wbuf.at[slot, half], sem.at[slot]) for half in range(2)]

    def cast(slot):
        wg_scr[...] = wbuf[slot, 0].astype(BF16)
        wu_scr[...] = wbuf[slot, 1].astype(BF16)

    _stream_expert_weights(te_ref, nx_ref, grp_ref, meta_ref, copies, cast)

    @pl.when(i < meta_ref[0])
    def _():
        x = x_ref[...]
        g = _dot(x, wg_scr[...]) + bg_ref[...]
        u = _dot(x, wu_scr[...]) + bu_ref[...]
        g = jnp.minimum(g, SWIGLU_LIMIT)
        u = jnp.clip(u, -SWIGLU_LIMIT, SWIGLU_LIMIT)
        o_ref[...] = ((u + 1.0) * g * jax.nn.sigmoid(SWIGLU_ALPHA * g)).astype(BF16)

    @pl.when(i >= meta_ref[0])
    def _():
        o_ref[...] = jnp.zeros(o_ref.shape, o_ref.dtype)


def _expert_gu(xs, w_gu, b_gu, sched, tm):
    n_slots, d = xs.shape
    de = w_gu.shape[2] // 2
    tn = 1024
    nj = de // tn
    row = lambda j, i, te, nx, gr, mt: jnp.minimum(i, mt[0] - 1)
    exp = lambda j, i, te, nx, gr, mt: te[jnp.minimum(i, mt[0] - 1)]
    b3 = b_gu.reshape(w_gu.shape[0], 1, 2 * de)
    return pl.pallas_call(
        functools.partial(_expert_gu_kernel, tn=tn, nj=nj),
        grid_spec=pltpu.PrefetchScalarGridSpec(
            num_scalar_prefetch=4,
            grid=(nj, n_slots // tm),
            in_specs=[
                pl.BlockSpec((tm, d), lambda j, i, *s: (row(j, i, *s), 0)),
                pl.BlockSpec(memory_space=pl.ANY),
                pl.BlockSpec((None, 1, tn), lambda j, i, *s: (exp(j, i, *s), 0, j)),
                pl.BlockSpec((None, 1, tn), lambda j, i, *s: (exp(j, i, *s), 0, nj + j)),
            ],
            out_specs=pl.BlockSpec((tm, tn), lambda j, i, *s: (i, j)),
            scratch_shapes=[pltpu.VMEM((2, 2, d, tn), F32), pltpu.VMEM((d, tn), BF16), pltpu.VMEM((d, tn), BF16),
                            pltpu.SemaphoreType.DMA((2,))],
        ),
        out_shape=jax.ShapeDtypeStruct((n_slots, de), BF16),
        compiler_params=_cparams("arbitrary", "arbitrary"),
        name="expert_gu",
    )(*sched, xs, w_gu, b3, b3)


def _expert_down_kernel(te_ref, nx_ref, grp_ref, meta_ref, a_ref, w_ref, b_ref, o_ref, wbuf, w_scr, sem, *, tn):
    i = pl.program_id(1)

    def copies(e, col, slot):
        return [pltpu.make_async_copy(w_ref.at[e, :, pl.ds(pl.multiple_of(col * tn, tn), tn)],
                                      wbuf.at[slot], sem.at[slot])]

    def cast(slot):
        w_scr[...] = wbuf[slot].astype(BF16)

    _stream_expert_weights(te_ref, nx_ref, grp_ref, meta_ref, copies, cast)

    @pl.when(i < meta_ref[0])
    def _():
        o_ref[...] = _dot(a_ref[...], w_scr[...]) + b_ref[...]

    @pl.when(i >= meta_ref[0])
    def _():
        o_ref[...] = jnp.zeros(o_ref.shape, o_ref.dtype)


def _expert_down(act, w_down, b_down, sched, tm):
    n_slots, de = act.shape
    d = w_down.shape[2]
    tn = 1024
    nj = d // tn
    row = lambda j, i, te, nx, gr, mt: jnp.minimum(i, mt[0] - 1)
    exp = lambda j, i, te, nx, gr, mt: te[jnp.minimum(i, mt[0] - 1)]
    return pl.pallas_call(
        functools.partial(_expert_down_kernel, tn=tn),
        grid_spec=pltpu.PrefetchScalarGridSpec(
            num_scalar_prefetch=4,
            grid=(nj, n_slots // tm),
            in_specs=[
                pl.BlockSpec((tm, de), lambda j, i, *s: (row(j, i, *s), 0)),
                pl.BlockSpec(memory_space=pl.ANY),
                pl.BlockSpec((None, 1, tn), lambda j, i, *s: (exp(j, i, *s), 0, j)),
            ],
            out_specs=pl.BlockSpec((tm, tn), lambda j, i, *s: (i, j)),
            scratch_shapes=[pltpu.VMEM((2, de, tn), F32), pltpu.VMEM((de, tn), BF16),
                            pltpu.SemaphoreType.DMA((2,))],
        ),
        out_shape=jax.ShapeDtypeStruct((n_slots, d), F32),
        compiler_params=_cparams("arbitrary", "arbitrary"),
        name="expert_down",
    )(*sched, act, w_down, b_down.reshape(w_down.shape[0], 1, d))


def _combine_kernel(dest_ref, y_ref, gate_ref, x_ref, g2_ref, gfin_ref, o_ref, ybuf, sem, *, tt, tok0, final):
    s = pl.program_id(0) * pl.num_programs(1) + pl.program_id(1)
    n_steps = pl.num_programs(0) * pl.num_programs(1)

    def row_copy(tile, r, k):
        buf = tile % 2
        slot = dest_ref[(tok0 + tile * tt + r) * TOP_K + k]
        return pltpu.make_async_copy(y_ref.at[pl.ds(slot, 1)], ybuf.at[buf, k, pl.ds(r, 1)], sem.at[buf])

    def start_tile(tile):
        def start(r, c):
            for k in range(TOP_K):
                row_copy(tile, r, k).start()
            return c
        lax.fori_loop(0, tt, start, 0, unroll=2)

    def wait(r, c):
        for k in range(TOP_K):
            row_copy(s, r, k).wait()
        return c

    @pl.when(s == 0)
    def _():
        start_tile(s)

    @pl.when(s + 1 < n_steps)
    def _():
        start_tile(s + 1)

    lax.fori_loop(0, tt, wait, 0, unroll=2)
    gates = gate_ref[...]
    yb = ybuf.at[s % 2]
    moe = gates[:, 0:1] * yb[0]
    for k in range(1, TOP_K):
        moe = moe + gates[:, k:k + 1] * yb[k]
    x2 = x_ref[...] + g2_ref[...] * moe
    if final:
        x2 = x2 * lax.rsqrt(jnp.mean(x2 * x2, axis=-1, keepdims=True) + RMS_EPS) * gfin_ref[...]
    o_ref[...] = x2


def _combine(y, dest, gates, x1, mod, g_final, *, tt, tok0, final):
    G, R, D = x1.shape
    rs = mod.shape[1]
    rblk = 1 if rs == 1 else tt
    ridx = (lambda i: 0) if rs == 1 else (lambda i: i)
    kern = functools.partial(_combine_kernel, tt=tt, tok0=tok0, final=final)
    return pl.pallas_call(
        kern,
        grid_spec=pltpu.PrefetchScalarGridSpec(
            num_scalar_prefetch=1,
            grid=(G, R // tt),
            in_specs=[
                pl.BlockSpec(memory_space=pl.ANY),
                pl.BlockSpec((None, tt, LANES), lambda g_, i, d_: (g_, i, 0)),
                pl.BlockSpec((None, tt, D), lambda g_, i, d_: (g_, i, 0)),
                pl.BlockSpec((None, rblk, D), lambda g_, i, d_: (g_, ridx(i), 5)),
                pl.BlockSpec((1, D), lambda g_, i, d_: (0, 0)),
            ],
            out_specs=pl.BlockSpec((None, tt, D), lambda g_, i, d_: (g_, i, 0)),
            scratch_shapes=[pltpu.VMEM((2, TOP_K, tt, D), F32), pltpu.SemaphoreType.DMA((2,))],
        ),
        out_shape=jax.ShapeDtypeStruct((G, R, D), F32),
        compiler_params=_cparams("arbitrary", "arbitrary"),
        name="combine",
    )(dest, y, gates, x1, mod, g_final)


def _row_tile(rows, cap):
    return max(t for t in range(MOBA_BLOCK, cap + 1, MOBA_BLOCK) if rows % t == 0)


def _layer_weights(l, w_in, b_forget, w_br_moba, w_br_sb, w_br_fox, w_out, w_router, b_router):
    w_l = w_in[l]
    w_main = jnp.concatenate([w_l[:, :3 * D_ATTN], w_l[:, 3 * D_ATTN + H_FOX:]], axis=1).astype(BF16)
    w_f = jnp.pad(w_l[:, 3 * D_ATTN:3 * D_ATTN + H_FOX], ((0, 0), (0, LANES - H_FOX)))
    b_f = jnp.pad(b_forget[l], (0, LANES - H_FOX)).reshape(1, LANES)
    w_r = jnp.pad(w_router[l], ((0, 0), (0, LANES - N_EXPERTS)))
    b_r = jnp.pad(b_router[l], (0, LANES - N_EXPERTS), constant_values=NEG_INF).reshape(1, LANES)
    return dict(w_main=w_main, w_f=w_f, b_f=b_f, w_r=w_r, b_r=b_r,
                w_m=w_br_moba[l].astype(BF16), w_s=w_br_sb[l].astype(BF16), w_x=w_br_fox[l].astype(BF16),
                w_o=w_out[l].astype(BF16))


def kernel(x_prompt, x_sample, c_prompt, c_sample, cache_k, cache_v, cache_log_f, page_table, w_mod, b_mod,
           g_attn, w_in, b_forget, w_br_moba, w_br_sb, w_br_fox, w_out, g_ffn, w_router, b_router, w_gu, b_gu,
           w_down, b_down, g_final):
    depth = w_mod.shape[0]
    B, T, D = x_prompt.shape
    nb, tq, _ = x_sample.shape
    n_pool = cache_k.shape[1]
    n_prompt = B * T
    n_samp = nb * tq
    tm = EXPERT_TILE

    clf = jnp.pad(cache_log_f, ((0, 0), (0, 0), (0, 0), (HPG - H_FOX, 0))).reshape(
        depth, n_pool, PAGE // TOK_PER_BLK, LANES)
    c_all = jnp.pad(jnp.concatenate([c_prompt, c_sample], axis=0), ((0, 16 - B - nb), (0, 0)))
    wgu = w_gu.reshape(depth * N_EXPERTS, D, w_gu.shape[-1])
    bgu = b_gu.reshape(depth * N_EXPERTS, b_gu.shape[-1])
    wdn = w_down.reshape(depth * N_EXPERTS, w_down.shape[2], D)
    bdn = b_down.reshape(depth * N_EXPERTS, D)
    g_fin = g_final.reshape(1, D)

    xp = x_prompt
    xs = x_sample.reshape(1, n_samp, D)
    kp = jnp.zeros((depth, B, T, D), F32)
    vp = jnp.zeros((depth, B, T, D), F32)
    ks = jnp.zeros((depth, 1, n_samp, D), F32)
    vs = jnp.zeros((depth, 1, n_samp, D), F32)
    fp, fs = [], []
    for l in range(depth):
        lw = _layer_weights(l, w_in, b_forget, w_br_moba, w_br_sb, w_br_fox, w_out, w_router, b_router)
        g_a = g_attn[l].reshape(1, D)
        g_f = g_ffn[l].reshape(1, D)
        mod = _mod(c_all, w_mod, b_mod, l)
        mod_p = mod[:B].reshape(B, 1, 6 * D)
        mod_s = jnp.repeat(mod[B:B + nb], tq, axis=0).reshape(1, n_samp, 6 * D)

        qg_p, kp, vp, lf_p = _ln_proj(xp, mod_p, g_a, lw["w_main"], lw["w_f"], lw["b_f"], kp, vp, l,
                                      tr=_row_tile(T, 1024))
        fc, fr = _fcum(lf_p)
        x1p = _merge(_moba(qg_p, kp, vp, l), _sb(qg_p, kp, vp, l), _fox(qg_p, kp, vp, fc, fr, l), qg_p, xp, mod_p,
                     lw["w_m"], lw["w_s"], lw["w_x"], lw["w_o"], tr=256)
        qg_s, ks, vs, lf_s = _ln_proj(xs, mod_s, g_a, lw["w_main"], lw["w_f"], lw["b_f"], ks, vs, l, tr=n_samp)
        o_s = _decode_attn(qg_s[0, 0], ks[l, 0], vs[l, 0], lf_s, cache_k, cache_v, clf, page_table, l)
        o_s = o_s.reshape(1, n_samp, D)
        n_m, n_s = H_MOBA * HEAD_DIM, H_SB * HEAD_DIM
        x1s = _merge(o_s[..., :n_m], o_s[..., n_m:n_m + n_s], o_s[..., n_m + n_s:], qg_s, xs, mod_s,
                     lw["w_m"], lw["w_s"], lw["w_x"], lw["w_o"], tr=n_samp)

        h2p, idx_p, gate_p = _norm_router(x1p, mod_p, g_f, lw["w_r"], lw["b_r"], tr=512)
        h2s, idx_s, gate_s = _norm_router(x1s, mod_s, g_f, lw["w_r"], lw["b_r"], tr=n_samp)
        h2 = jnp.concatenate([h2p.reshape(n_prompt, D), h2s.reshape(n_samp, D)], axis=0)
        top_idx = jnp.concatenate([idx_p.reshape(n_prompt, LANES)[:, :TOP_K],
                                   idx_s.reshape(n_samp, LANES)[:, :TOP_K]], axis=0)
        dest, slot_tok, tile_e, next_e, grp, meta = _route(top_idx, tm)
        rows = _gather_rows(h2, slot_tok, meta, tm)
        sched = (tile_e + l * N_EXPERTS, jnp.where(next_e >= 0, next_e + l * N_EXPERTS, -1), grp, meta)
        act = _expert_gu(rows, wgu, bgu, sched, tm)
        y = _expert_down(act, wdn, bdn, sched, tm)
        final = l == depth - 1
        xp = _combine(y, dest, gate_p, x1p, mod_p, g_fin, tt=256, tok0=0, final=final)
        xs = _combine(y, dest, gate_s, x1s, mod_s, g_fin, tt=n_samp, tok0=n_prompt, final=final)

        fp.append(lf_p[..., :H_FOX])
        fs.append(lf_s[..., :H_FOX])

    hd = (N_HEADS, HEAD_DIM)
    return (xp, xs.reshape(nb, tq, D),
            kp.reshape(depth, B, T, *hd), vp.reshape(depth, B, T, *hd), jnp.stack(fp),
            ks.reshape(depth, nb, tq, *hd), vs.reshape(depth, nb, tq, *hd),
            jnp.stack(fs).reshape(depth, nb, tq, H_FOX))
```

```python
import functools

import jax
import jax.numpy as jnp
from jax import lax
from jax.experimental import pallas as pl
from jax.experimental.pallas import tpu as pltpu

F32 = jnp.float32
BF16 = jnp.bfloat16
I32 = jnp.int32

HEAD_DIM = 128
N_HEADS = 16
H_MOBA = 6
H_SB = 5
H_FOX = 5
D_ATTN = N_HEADS * HEAD_DIM
MOBA_BLOCK = 256
MOBA_TOPK = 3
N_EXPERTS = 32
TOP_K = 4
SWIGLU_ALPHA = 1.702
SWIGLU_LIMIT = 7.0
RMS_EPS = 1e-5
NEG_INF = -1e30
PAGE = 128
LANES = 128
VMEM_LIMIT_BYTES = 56 * 1024 * 1024
EXPERT_TILE = 256
MIXER_TQ = 256
MIXER_TK = 256
DECODE_PAGES = 4


def _cparams(*sem):
    return pltpu.CompilerParams(dimension_semantics=sem, vmem_limit_bytes=VMEM_LIMIT_BYTES)


def _dot(a, b):
    return jnp.dot(a, b, preferred_element_type=F32)


def _dot_nt(a, b):
    return lax.dot_general(a, b, (((1,), (1,)), ((), ())), preferred_element_type=F32)


def _split2(x):
    hi = x.astype(BF16)
    lo = (x - hi.astype(F32)).astype(BF16)
    return hi, lo


def _split3(x):
    hi = x.astype(BF16)
    r = x - hi.astype(F32)
    mid = r.astype(BF16)
    lo = (r - mid.astype(F32)).astype(BF16)
    return hi, mid, lo


def _dot_acc(a, b):
    a_hi, a_lo = _split2(a)
    b_hi, b_lo = _split2(b)
    return _dot(a_hi, b_hi) + _dot(a_lo, b_hi) + _dot(a_hi, b_lo)


def _dot_nt_acc(a, b):
    a_hi, a_lo = _split2(a)
    b_hi, b_lo = _split2(b)
    return _dot_nt(a_hi, b_hi) + _dot_nt(a_lo, b_hi) + _dot_nt(a_hi, b_lo)


def _dot_exact_rhs(a, b_exact):
    hi, mid, lo = _split3(a)
    return _dot(hi, b_exact) + _dot(mid, b_exact) + _dot(lo, b_exact)


def _dot_exact_lhs(a_exact, b):
    hi, mid, lo = _split3(b)
    return _dot(a_exact, hi) + _dot(a_exact, mid) + _dot(a_exact, lo)


def _softplus_neg_abs(z):
    return jnp.log1p(jnp.exp(-jnp.abs(z)))


def _log_sigmoid(z):
    return -(jnp.maximum(-z, 0.0) + _softplus_neg_abs(z))


def _rms_modulate(x, g, scale, shift):
    y = x * lax.rsqrt(jnp.mean(x * x, axis=-1, keepdims=True) + RMS_EPS)
    return (y * g) * (1.0 + scale) + shift


def _mod_kernel(c_ref, w_ref, b_ref, o_ref):
    c = c_ref[...]
    o_ref[...] = _dot_acc(c * jax.nn.sigmoid(c), w_ref[...]) + b_ref[...]


def _mod(c, w, b, layer):
    m, d = c.shape
    depth, _, n = w.shape
    tn = 1024
    return pl.pallas_call(
        _mod_kernel,
        grid=(n // tn,),
        in_specs=[
            pl.BlockSpec((m, d), lambda j: (0, 0)),
            pl.BlockSpec((None, d, tn), lambda j: (layer, 0, j)),
            pl.BlockSpec((None, 1, tn), lambda j: (layer, 0, j)),
        ],
        out_specs=pl.BlockSpec((m, tn), lambda j: (0, j)),
        out_shape=jax.ShapeDtypeStruct((m, n), F32),
        compiler_params=_cparams("parallel"),
        name="mod",
    )(c, w, b.reshape(depth, 1, n))


def _ln_proj_kernel(x_ref, sh_ref, sc_ref, g_ref, w_ref, wf_ref, bf_ref, kprev_ref, vprev_ref,
                    qg_ref, k_ref, v_ref, lf_ref, h_scr, *, per):
    del kprev_ref, vprev_ref
    j = pl.program_id(2)

    @pl.when(j == 0)
    def _():
        h = _rms_modulate(x_ref[...], g_ref[...], sc_ref[...], sh_ref[...])
        h_scr[...] = h.astype(BF16)
        lf_ref[...] = _log_sigmoid(_dot_acc(h, wf_ref[...]) + bf_ref[...])

    acc = _dot(h_scr[...], w_ref[...])

    @pl.when(j < per)
    def _():
        qg_ref[...] = acc

    @pl.when(jnp.logical_and(j >= per, j < 2 * per))
    def _():
        k_ref[...] = acc

    @pl.when(jnp.logical_and(j >= 2 * per, j < 3 * per))
    def _():
        v_ref[...] = acc

    @pl.when(j >= 3 * per)
    def _():
        qg_ref[...] = jax.nn.sigmoid(acc)


def _ln_proj(x, mod, g, w_main, w_f, b_f, k_all, v_all, layer, *, tr):
    G, R, D = x.shape
    rs = mod.shape[1]
    tn = 512
    per = D // tn
    n_col = w_main.shape[1] // tn
    rblk = 1 if rs == 1 else tr
    ridx = (lambda i: 0) if rs == 1 else (lambda i: i)
    kern = functools.partial(_ln_proj_kernel, per=per)
    clamp = lambda j, lo: jnp.clip(j - lo, 0, per - 1)

    def qg_idx(g_, i, j):
        grp = jnp.where(j < 3 * per, 0, j // per - 2)
        col = jnp.where(j < per, j, jnp.where(j < 3 * per, per - 1, j % per))
        return (grp, g_, i, col)

    return pl.pallas_call(
        kern,
        grid=(G, R // tr, n_col),
        in_specs=[
            pl.BlockSpec((None, tr, D), lambda g_, i, j: (g_, i, 0)),
            pl.BlockSpec((None, rblk, D), lambda g_, i, j: (g_, ridx(i), 0)),
            pl.BlockSpec((None, rblk, D), lambda g_, i, j: (g_, ridx(i), 1)),
            pl.BlockSpec((1, D), lambda g_, i, j: (0, 0)),
            pl.BlockSpec((D, tn), lambda g_, i, j: (0, j)),
            pl.BlockSpec((D, LANES), lambda g_, i, j: (0, 0)),
            pl.BlockSpec((1, LANES), lambda g_, i, j: (0, 0)),
            pl.BlockSpec(memory_space=pl.ANY),
            pl.BlockSpec(memory_space=pl.ANY),
        ],
        out_specs=[
            pl.BlockSpec((None, None, tr, tn), qg_idx),
            pl.BlockSpec((None, None, tr, tn), lambda g_, i, j: (layer, g_, i, clamp(j, per))),
            pl.BlockSpec((None, None, tr, tn), lambda g_, i, j: (layer, g_, i, clamp(j, 2 * per))),
            pl.BlockSpec((None, tr, LANES), lambda g_, i, j: (g_, i, 0)),
        ],
        out_shape=[
            jax.ShapeDtypeStruct((n_col // per - 2, G, R, D), F32),
            jax.ShapeDtypeStruct(k_all.shape, F32),
            jax.ShapeDtypeStruct(v_all.shape, F32),
            jax.ShapeDtypeStruct((G, R, LANES), F32),
        ],
        input_output_aliases={7: 1, 8: 2},
        scratch_shapes=[pltpu.VMEM((tr, D), BF16)],
        compiler_params=_cparams("parallel", "parallel", "arbitrary"),
        name="ln_proj",
    )(x, mod, mod, g, w_main, w_f, b_f, k_all, v_all)


def _fcum_kernel(lf_ref, fc_ref, fr_ref):
    t = lf_ref.shape[0]
    r = lax.broadcasted_iota(I32, (t, t), 0)
    c = lax.broadcasted_iota(I32, (t, t), 1)
    tri = (r >= c).astype(BF16)
    f = _dot_exact_lhs(tri, lf_ref[...])
    fc_ref[...] = f
    fr_ref[...] = f.T[:8, :]


def _fcum(logf):
    B, T, _ = logf.shape
    return pl.pallas_call(
        _fcum_kernel,
        grid=(B,),
        in_specs=[pl.BlockSpec((None, T, LANES), lambda b: (b, 0, 0))],
        out_specs=[
            pl.BlockSpec((None, T, LANES), lambda b: (b, 0, 0)),
            pl.BlockSpec((None, 8, T), lambda b: (b, 0, 0)),
        ],
        out_shape=[jax.ShapeDtypeStruct((B, T, LANES), F32), jax.ShapeDtypeStruct((B, 8, T), F32)],
        compiler_params=_cparams("parallel"),
        name="fcum",
    )(logf)


def _head_specs(lead, head0, n_heads, rows, row_idx):
    return [
        pl.BlockSpec((None, None, rows, HEAD_DIM), lambda b, i, h=h: (lead, b, row_idx(i), head0 + h))
        for h in range(n_heads)
    ]


def _moba_kernel(*refs, n_heads, tq, n_kb):
    q_refs, k_refs, v_refs = refs[:n_heads], refs[n_heads:2 * n_heads], refs[2 * n_heads:3 * n_heads]
    o_ref, m_scr, l_scr, acc_scr, sel_scr, km_scr = refs[3 * n_heads:]
    i = pl.program_id(1)
    own_blk = (i * tq) // MOBA_BLOCK
    scale = HEAD_DIM ** -0.5
    n_sel = min(MOBA_TOPK, n_kb)
    lane = lax.broadcasted_iota(I32, (tq, LANES), 1)
    qpos = i * tq + lax.broadcasted_iota(I32, (tq, MOBA_BLOCK), 0)
    koff = lax.broadcasted_iota(I32, (tq, MOBA_BLOCK), 1)
    @pl.when(i == 0)
    def _():
        for h in range(n_heads):
            kmean = jnp.sum(k_refs[h][...].reshape(n_kb, MOBA_BLOCK, HEAD_DIM), axis=1) * (1.0 / MOBA_BLOCK)
            km_scr[h] = jnp.concatenate([kmean, jnp.zeros((LANES - n_kb, HEAD_DIM), F32)], axis=0)

    for h in range(n_heads):
        q = q_refs[h][...]
        gate = jnp.where(lane < own_blk, _dot_nt_acc(q, km_scr[h]), NEG_INF)
        sel = jnp.zeros((tq, LANES), jnp.bool_)
        for _ in range(n_sel):
            m = jnp.max(gate, axis=-1, keepdims=True)
            idx = jnp.min(jnp.where(gate == m, lane, LANES), axis=-1, keepdims=True)
            hit = lane == idx
            sel = jnp.logical_or(sel, hit)
            gate = jnp.where(hit, -jnp.inf, gate)
        sel_scr[h] = jnp.where(jnp.logical_and(sel, lane < own_blk), 1.0, 0.0)
        m_scr[h] = jnp.full((tq, 1), NEG_INF, F32)
        l_scr[h] = jnp.zeros((tq, 1), F32)
        acc_scr[h] = jnp.zeros((tq, HEAD_DIM), F32)

    def body(n, carry):
        start = pl.multiple_of(n * MOBA_BLOCK, MOBA_BLOCK)
        kpos = n * MOBA_BLOCK + koff
        dist = (qpos - kpos).astype(F32)
        own = (koff * 0 + n) == own_blk
        own_visible = jnp.logical_and(own, kpos <= qpos)
        not_own = jnp.logical_not(own)
        for h in range(n_heads):
            slope = 2.0 ** (-8.0 * (h + 1) / n_heads)
            qb = q_refs[h][...].astype(BF16)
            kn = k_refs[h][pl.ds(start, MOBA_BLOCK), :].astype(BF16)
            vn = v_refs[h][pl.ds(start, MOBA_BLOCK), :].astype(BF16)
            s = _dot_nt(qb, kn) * scale - slope * dist
            sel_n = jnp.sum(jnp.where(lane == n, sel_scr[h], 0.0), axis=-1, keepdims=True) > 0.5
            valid = jnp.logical_or(own_visible, jnp.logical_and(not_own, sel_n))
            s = jnp.where(valid, s, NEG_INF)
            m_run = m_scr[h]
            m_new = jnp.maximum(m_run, jnp.max(s, axis=-1, keepdims=True))
            alpha = jnp.exp(m_run - m_new)
            p = jnp.where(valid, jnp.exp(s - m_new), 0.0)
            l_scr[h] = alpha * l_scr[h] + jnp.sum(p, axis=-1, keepdims=True)
            acc_scr[h] = alpha * acc_scr[h] + _dot(p.astype(BF16), vn)
            m_scr[h] = m_new
        return carry

    lax.fori_loop(0, own_blk + 1, body, 0)
    for h in range(n_heads):
        o_ref[:, h * HEAD_DIM:(h + 1) * HEAD_DIM] = acc_scr[h] / l_scr[h]


def _mixer_scratch(n_heads, tq, n_col, n_wide):
    return ([pltpu.VMEM((n_heads, tq, 1), F32)] * n_col
            + [pltpu.VMEM((n_heads, tq, HEAD_DIM), F32)] * n_wide)


def _moba(qg, k_all, v_all, layer):
    _, B, T, _ = qg.shape
    tq = MIXER_TQ
    n_kb = T // MOBA_BLOCK
    kern = functools.partial(_moba_kernel, n_heads=H_MOBA, tq=tq, n_kb=n_kb)
    specs = (_head_specs(0, 0, H_MOBA, tq, lambda i: i) + _head_specs(layer, 0, H_MOBA, T, lambda i: 0)
             + _head_specs(layer, 0, H_MOBA, T, lambda i: 0))
    return pl.pallas_call(
        kern,
        grid=(B, T // tq),
        in_specs=specs,
        out_specs=pl.BlockSpec((None, tq, H_MOBA * HEAD_DIM), lambda b, i: (b, i, 0)),
        out_shape=jax.ShapeDtypeStruct((B, T, H_MOBA * HEAD_DIM), F32),
        scratch_shapes=_mixer_scratch(H_MOBA, tq, 2, 2) + [pltpu.VMEM((H_MOBA, LANES, HEAD_DIM), F32)],
        compiler_params=_cparams("parallel", "arbitrary"),
        name="moba",
    )(*([qg] * H_MOBA + [k_all] * H_MOBA + [v_all] * H_MOBA))


def _strict_upper(n):
    j = lax.broadcasted_iota(I32, (n, n), 0)
    s = lax.broadcasted_iota(I32, (n, n), 1)
    return (j > s).astype(BF16)


def _sb_kernel(*refs, n_heads, tq, tk):
    q_refs, k_refs, v_refs = refs[:n_heads], refs[n_heads:2 * n_heads], refs[2 * n_heads:3 * n_heads]
    o_ref, c_scr, acc_scr = refs[3 * n_heads:]
    i = pl.program_id(1)
    scale = HEAD_DIM ** -0.5
    qpos = i * tq + lax.broadcasted_iota(I32, (tq, tk), 0)
    koff = lax.broadcasted_iota(I32, (tq, tk), 1)
    after = _strict_upper(tk)
    c_scr[...] = jnp.zeros(c_scr.shape, F32)
    acc_scr[...] = jnp.zeros(acc_scr.shape, F32)

    last = (i * tq) // tk

    def body(t, carry):
        n = last - t
        start = pl.multiple_of(n * tk, tk)
        strict = (n * tk + koff) < qpos
        for h in range(n_heads):
            qb = q_refs[h][...].astype(BF16)
            kn = k_refs[h][pl.ds(start, tk), :].astype(BF16)
            vn = v_refs[h][pl.ds(start, tk), :].astype(BF16)
            z = _dot_nt(qb, kn) * scale
            sp = _softplus_neg_abs(z)
            log_beta = -(jnp.maximum(-z, 0.0) + sp)
            log_keep = jnp.where(strict, -(jnp.maximum(z, 0.0) + sp), 0.0)
            c_run = c_scr[h]
            between = _dot_exact_rhs(log_keep, after) + c_run
            w = jnp.where(strict, jnp.exp(log_beta + between), 0.0)
            acc_scr[h] = acc_scr[h] + _dot(w.astype(BF16), vn)
            c_scr[h] = c_run + jnp.sum(log_keep, axis=-1, keepdims=True)
        return carry

    lax.fori_loop(0, last + 1, body, 0)
    for h in range(n_heads):
        o_ref[:, h * HEAD_DIM:(h + 1) * HEAD_DIM] = acc_scr[h]


def _sb(qg, k_all, v_all, layer):
    _, B, T, _ = qg.shape
    tq, tk = MIXER_TQ, MIXER_TK
    kern = functools.partial(_sb_kernel, n_heads=H_SB, tq=tq, tk=tk)
    specs = (_head_specs(0, H_MOBA, H_SB, tq, lambda i: i) + _head_specs(layer, H_MOBA, H_SB, T, lambda i: 0)
             + _head_specs(layer, H_MOBA, H_SB, T, lambda i: 0))
    return pl.pallas_call(
        kern,
        grid=(B, T // tq),
        in_specs=specs,
        out_specs=pl.BlockSpec((None, tq, H_SB * HEAD_DIM), lambda b, i: (b, i, 0)),
        out_shape=jax.ShapeDtypeStruct((B, T, H_SB * HEAD_DIM), F32),
        scratch_shapes=_mixer_scratch(H_SB, tq, 1, 1),
        compiler_params=_cparams("parallel", "arbitrary"),
        name="sb",
    )(*([qg] * H_SB + [k_all] * H_SB + [v_all] * H_SB))


def _fox_kernel(*refs, n_heads, tq, tk):
    q_refs, k_refs, v_refs = refs[:n_heads], refs[n_heads:2 * n_heads], refs[2 * n_heads:3 * n_heads]
    fc_ref, fr_ref, o_ref, m_scr, l_scr, acc_scr = refs[3 * n_heads:]
    i = pl.program_id(1)
    scale = HEAD_DIM ** -0.5
    qpos = i * tq + lax.broadcasted_iota(I32, (tq, tk), 0)
    koff = lax.broadcasted_iota(I32, (tq, tk), 1)
    m_scr[...] = jnp.full(m_scr.shape, NEG_INF, F32)
    l_scr[...] = jnp.zeros(l_scr.shape, F32)
    acc_scr[...] = jnp.zeros(acc_scr.shape, F32)

    def body(n, carry):
        start = pl.multiple_of(n * tk, tk)
        valid = (n * tk + koff) <= qpos
        for h in range(n_heads):
            qb = q_refs[h][...].astype(BF16)
            kn = k_refs[h][pl.ds(start, tk), :].astype(BF16)
            vn = v_refs[h][pl.ds(start, tk), :].astype(BF16)
            fq = fc_ref[:, h:h + 1]
            fk = fr_ref[h:h + 1, pl.ds(start, tk)]
            s = jnp.where(valid, _dot_nt(qb, kn) * scale + (fq - fk), NEG_INF)
            m_run = m_scr[h]
            m_new = jnp.maximum(m_run, jnp.max(s, axis=-1, keepdims=True))
            alpha = jnp.exp(m_run - m_new)
            p = jnp.where(valid, jnp.exp(s - m_new), 0.0)
            l_scr[h] = alpha * l_scr[h] + jnp.sum(p, axis=-1, keepdims=True)
            acc_scr[h] = alpha * acc_scr[h] + _dot(p.astype(BF16), vn)
            m_scr[h] = m_new
        return carry

    lax.fori_loop(0, (i * tq) // tk + 1, body, 0)
    for h in range(n_heads):
        o_ref[:, h * HEAD_DIM:(h + 1) * HEAD_DIM] = acc_scr[h] / l_scr[h]


def _fox(qg, k_all, v_all, fc, fr, layer):
    _, B, T, _ = qg.shape
    tq, tk = MIXER_TQ, MIXER_TK
    h0 = H_MOBA + H_SB
    kern = functools.partial(_fox_kernel, n_heads=H_FOX, tq=tq, tk=tk)
    specs = (_head_specs(0, h0, H_FOX, tq, lambda i: i) + _head_specs(layer, h0, H_FOX, T, lambda i: 0)
             + _head_specs(layer, h0, H_FOX, T, lambda i: 0)
             + [pl.BlockSpec((None, tq, LANES), lambda b, i: (b, i, 0)),
                pl.BlockSpec((None, 8, T), lambda b, i: (b, 0, 0))])
    return pl.pallas_call(
        kern,
        grid=(B, T // tq),
        in_specs=specs,
        out_specs=pl.BlockSpec((None, tq, H_FOX * HEAD_DIM), lambda b, i: (b, i, 0)),
        out_shape=jax.ShapeDtypeStruct((B, T, H_FOX * HEAD_DIM), F32),
        scratch_shapes=_mixer_scratch(H_FOX, tq, 2, 1),
        compiler_params=_cparams("parallel", "arbitrary"),
        name="fox",
    )(*([qg] * H_FOX + [k_all] * H_FOX + [v_all] * H_FOX), fc, fr)


HPG = 8
TOK_PER_BLK = LANES // HPG


def _lane_class_mats():
    j = lax.broadcasted_iota(I32, (LANES, LANES), 0)
    s = lax.broadcasted_iota(I32, (LANES, LANES), 1)
    same = (j & 7) == (s & 7)
    after = jnp.logical_and(same, (j >> 3) > (s >> 3)).astype(BF16)
    return after, same.astype(BF16)


def _scan_blocked(xb, rows, nblk, carry, after, same):
    within = _dot_exact_rhs(xb, after)
    tot = _dot_exact_rhs(xb, same)
    out = [None] * nblk
    run = carry
    for b in range(nblk - 1, -1, -1):
        out[b] = within[b * rows:(b + 1) * rows] + run
        run = run + tot[b * rows:(b + 1) * rows]
    return out, run


def _decode_kernel(pt_ref, q_ref, kn_ref, vn_ref, lfn_ref, lfnb_ref, *refs, n_pages, tq):
    page_refs = [refs[3 * p:3 * p + 3] for p in range(DECODE_PAGES)]
    (o_ref, mb_m, mb_l, mb_g, mb_o, own_m, own_l, own_o, sb_acc0, sb_acc1, sb_c0, sb_c1,
     fx_m, fx_l, fx_acc, fx_c, a_col) = refs[3 * DECODE_PAGES:]
    j = pl.program_id(1)
    page_b = n_pages - 1 - DECODE_PAGES * j
    t_past = n_pages * PAGE
    scale = HEAD_DIM ** -0.5
    n_rows = HPG * tq
    r_mb = H_MOBA * tq
    r_sb1 = (H_MOBA + H_SB - HPG) * tq
    after, same = _lane_class_mats()
    lane1 = lax.broadcasted_iota(I32, (tq, LANES), 1)

    def lane_blocks(x):
        return [x[:, b * LANES:(b + 1) * LANES] for b in range(x.shape[1] // LANES)]

    def process(k_ref, v_ref, lfblk, page, is_new):
        t_tok = k_ref.shape[0]
        nblk = t_tok // TOK_PER_BLK
        w = t_tok * HPG
        lane = lax.broadcasted_iota(I32, (n_rows, w), 1)
        row = lax.broadcasted_iota(I32, (n_rows, w), 0)
        tpos = lane >> 3
        r = row & 7
        mine = (lane & 7) == (row >> 3)
        if is_new:
            vis = jnp.logical_and(mine, tpos <= r)
            vis_strict = jnp.logical_and(mine, tpos < r)
            dist = (r - tpos).astype(F32)
        else:
            vis = vis_strict = mine
            dist = (t_past + r - (page * PAGE + tpos)).astype(F32)
        rowc = lax.broadcasted_iota(I32, (n_rows, 1), 0) >> 3
        slope = jnp.zeros((n_rows, 1), F32)
        for h in range(H_MOBA):
            slope = jnp.where(rowc == h, 2.0 ** (-8.0 * (h + 1) / H_MOBA), slope)

        s_g, v_g = [], []
        for g in range(2):
            kg = k_ref[:, g * HPG:(g + 1) * HPG, :].reshape(w, HEAD_DIM).astype(BF16)
            v_g.append(v_ref[:, g * HPG:(g + 1) * HPG, :].reshape(w, HEAD_DIM).astype(BF16))
            s_g.append(_dot_nt(q_ref[g].astype(BF16), kg) * scale)

        def own_slot(c, row0, rows):
            mine1 = (lax.broadcasted_iota(I32, (rows, LANES), 1) & 7) == (
                (row0 + lax.broadcasted_iota(I32, (rows, LANES), 0)) >> 3)
            return jnp.max(jnp.where(mine1, c, -jnp.inf), axis=-1, keepdims=True)

        def sb_weights(z, visible, c_ref, row0):
            rows = z.shape[0]
            sp = _softplus_neg_abs(z)
            log_beta = -(jnp.maximum(-z, 0.0) + sp)
            log_keep = jnp.where(visible, -(jnp.maximum(z, 0.0) + sp), 0.0)
            carry = c_ref[...]
            between, total = _scan_blocked(jnp.concatenate(lane_blocks(log_keep), axis=0), rows, nblk,
                                           jnp.zeros((rows, LANES), F32), after, same)
            c_ref[...] = carry + total
            ws = [jnp.where(vb, jnp.exp(lb + bt), 0.0)
                  for vb, lb, bt in zip(lane_blocks(visible), lane_blocks(log_beta), between)]
            return jnp.concatenate(ws, axis=1), jnp.exp(own_slot(carry, row0, rows))

        s = jnp.where(vis[:r_mb], s_g[0][:r_mb] - slope[:r_mb] * dist[:r_mb], NEG_INF)
        m = jnp.max(s, axis=-1, keepdims=True)
        p_mb = jnp.exp(s - m)
        l = jnp.sum(p_mb, axis=-1, keepdims=True)
        w_sb0, kept0 = sb_weights(s_g[0][r_mb:], vis_strict[r_mb:], sb_c0, r_mb)
        pv0 = _dot(jnp.concatenate([p_mb, w_sb0], axis=0).astype(BF16), v_g[0])
        sb_acc0[...] = sb_acc0[...] + kept0 * pv0[r_mb:]
        if is_new:
            own_m[...] = m
            own_l[...] = l
            own_o[...] = pv0[:r_mb]
        else:
            ksum = jnp.sum(k_ref[:, 0:HPG, :], axis=0)
            ksum_rows = jnp.concatenate(
                [jnp.broadcast_to(ksum[h:h + 1, :], (tq, HEAD_DIM)) for h in range(H_MOBA)], axis=0)
            g_col = jnp.sum(q_ref[0][:r_mb] * ksum_rows, axis=-1, keepdims=True)
            here = lane1 == page
            for h in range(H_MOBA):
                rs = slice(h * tq, (h + 1) * tq)
                mb_m[h] = jnp.where(here, m[rs], mb_m[h])
                mb_l[h] = jnp.where(here, l[rs], mb_l[h])
                mb_g[h] = jnp.where(here, g_col[rs], mb_g[h])
                for rr in range(tq):
                    mb_o[h, rr, pl.ds(page, 1), :] = pv0[h * tq + rr:h * tq + rr + 1, :]

        w_sb1, kept1 = sb_weights(s_g[1][:r_sb1], vis_strict[:r_sb1], sb_c1, 0)
        n_fx = n_rows - r_sb1
        carry = fx_c[...]
        bias_blk, total = _scan_blocked(lfblk, 1, nblk, jnp.zeros((1, LANES), F32), after, same)
        fx_c[...] = carry + total
        bias = jnp.concatenate([jnp.broadcast_to(bb, (n_fx, LANES)) for bb in bias_blk], axis=1)
        s = jnp.where(vis[r_sb1:], s_g[1][r_sb1:] + bias + a_col[r_sb1:], NEG_INF)
        m_loc = jnp.max(s, axis=-1, keepdims=True)
        p_fx = jnp.exp(s - m_loc)
        pv1 = _dot(jnp.concatenate([w_sb1, p_fx], axis=0).astype(BF16), v_g[1])
        sb_acc1[...] = sb_acc1[...] + kept1 * pv1[:r_sb1]
        m_page = m_loc + own_slot(carry, r_sb1, n_fx)
        m_run = fx_m[...]
        m_new = jnp.maximum(m_run, m_page)
        alpha = jnp.exp(m_run - m_new)
        beta = jnp.exp(m_page - m_new)
        fx_l[...] = alpha * fx_l[...] + beta * jnp.sum(p_fx, axis=-1, keepdims=True)
        fx_acc[...] = alpha * fx_acc[...] + beta * pv1[r_sb1:]
        fx_m[...] = m_new

    @pl.when(j == 0)
    def _():
        mb_m[...] = jnp.full(mb_m.shape, NEG_INF, F32)
        mb_l[...] = jnp.zeros(mb_l.shape, F32)
        mb_g[...] = jnp.full(mb_g.shape, NEG_INF, F32)
        mb_o[...] = jnp.zeros(mb_o.shape, F32)
        sb_acc0[...] = jnp.zeros(sb_acc0.shape, F32)
        sb_acc1[...] = jnp.zeros(sb_acc1.shape, F32)
        sb_c0[...] = jnp.zeros(sb_c0.shape, F32)
        sb_c1[...] = jnp.zeros(sb_c1.shape, F32)
        fx_m[...] = jnp.full(fx_m.shape, NEG_INF, F32)
        fx_l[...] = jnp.zeros(fx_l.shape, F32)
        fx_acc[...] = jnp.zeros(fx_acc.shape, F32)
        lfn = lfn_ref[...]
        row1 = lax.broadcasted_iota(I32, (tq, LANES), 0)
        a_new = jnp.zeros((tq, LANES), F32)
        for rr in range(tq):
            a_new = a_new + jnp.where(row1 >= rr, lfn[rr:rr + 1, :], 0.0)
        a_col[...] = jnp.concatenate([jnp.zeros((r_sb1, 1), F32)]
                                     + [a_new[:, hh:hh + 1] for hh in range(H_FOX)], axis=0)
        lfnb = lfnb_ref[...]
        fx_c[...] = -_dot_exact_rhs(lfnb, same)
        process(kn_ref, vn_ref, lfnb, None, True)

    for back, (kc_ref, vc_ref, lfc_ref) in enumerate(page_refs):
        process(kc_ref, vc_ref, lfc_ref[...], page_b - back, False)

    @pl.when(j == n_pages // DECODE_PAGES - 1)
    def _():
        pj = lax.broadcasted_iota(I32, (LANES, LANES), 0)
        ps = lax.broadcasted_iota(I32, (LANES, LANES), 1)
        pair = ((pj >> 1) == (ps >> 1)).astype(BF16)
        n_kb = n_pages * PAGE // MOBA_BLOCK + 1
        for h in range(H_MOBA):
            rs = slice(h * tq, (h + 1) * tq)
            gate = jnp.where(lane1 < n_pages, _dot_exact_rhs(mb_g[h], pair), NEG_INF)
            sel = jnp.zeros((tq, LANES), jnp.bool_)
            for _ in range(min(MOBA_TOPK, n_kb)):
                m = jnp.max(gate, axis=-1, keepdims=True)
                idx = jnp.min(jnp.where(gate == m, lane1, LANES), axis=-1, keepdims=True)
                hit = (lane1 >> 1) == (idx >> 1)
                sel = jnp.logical_or(sel, hit)
                gate = jnp.where(hit, -jnp.inf, gate)
            sel = jnp.logical_and(sel, lane1 < n_pages)
            m_pg = mb_m[h]
            m_own = own_m[rs]
            m_fin = jnp.maximum(jnp.max(jnp.where(sel, m_pg, NEG_INF), axis=-1, keepdims=True), m_own)
            wp = jnp.where(sel, jnp.exp(m_pg - m_fin), 0.0)
            w_own = jnp.exp(m_own - m_fin)
            l_fin = jnp.sum(wp * mb_l[h], axis=-1, keepdims=True) + w_own * own_l[rs]
            rows = [_dot_acc(wp[rr:rr + 1, :n_pages], mb_o[h, rr]) for rr in range(tq)]
            o_fin = jnp.concatenate(rows, axis=0) + w_own * own_o[rs]
            o_ref[0, rs, :] = o_fin / l_fin
        o_ref[0, r_mb:, :] = sb_acc0[...]
        o_ref[1, :r_sb1, :] = sb_acc1[...]
        o_ref[1, r_sb1:, :] = fx_acc[...] / fx_l[...]


def _decode_attn(q_s, k_s, v_s, logf_s, cache_k, cache_v, cache_lfb, page_table, layer):
    nb, n_pages = page_table.shape
    d = q_s.shape[-1]
    tq = q_s.shape[0] // nb
    assert tq == HPG and N_HEADS == 2 * HPG and n_pages % DECODE_PAGES == 0 and n_pages <= LANES
    slot0 = H_MOBA + H_SB - HPG
    n_rows = HPG * tq
    qg = q_s.reshape(nb, tq, 2, HPG, HEAD_DIM).transpose(0, 2, 3, 1, 4).reshape(nb, 2, n_rows, HEAD_DIM)
    pad_tok = ((0, 0), (0, TOK_PER_BLK - tq), (0, 0), (0, 0))
    knew = jnp.pad(k_s.reshape(nb, tq, N_HEADS, HEAD_DIM), pad_tok)
    vnew = jnp.pad(v_s.reshape(nb, tq, N_HEADS, HEAD_DIM), pad_tok)
    lfnb = jnp.pad(logf_s[0, :, :H_FOX].reshape(nb, tq, H_FOX),
                   ((0, 0), (0, TOK_PER_BLK - tq), (slot0, 0))).reshape(nb, 1, LANES)
    kern = functools.partial(_decode_kernel, n_pages=n_pages, tq=tq)
    page_of = lambda b, j, pt, back: pt[b, n_pages - 1 - DECODE_PAGES * j - back]
    page_spec = lambda back: pl.BlockSpec(
        (None, None, PAGE, N_HEADS, HEAD_DIM), lambda b, j, pt: (layer, page_of(b, j, pt, back), 0, 0, 0))
    lf_spec = lambda back: pl.BlockSpec(
        (None, None, PAGE // TOK_PER_BLK, LANES), lambda b, j, pt: (layer, page_of(b, j, pt, back), 0, 0))
    page_specs = [spec for back in range(DECODE_PAGES) for spec in (page_spec(back), page_spec(back), lf_spec(back))]
    r_mb = H_MOBA * tq
    r_sb1 = slot0 * tq
    vm = pltpu.VMEM
    out = pl.pallas_call(
        kern,
        grid_spec=pltpu.PrefetchScalarGridSpec(
            num_scalar_prefetch=1,
            grid=(nb, n_pages // DECODE_PAGES),
            in_specs=[
                pl.BlockSpec((None, 2, n_rows, HEAD_DIM), lambda b, j, pt: (b, 0, 0, 0)),
                pl.BlockSpec((None, TOK_PER_BLK, N_HEADS, HEAD_DIM), lambda b, j, pt: (b, 0, 0, 0)),
                pl.BlockSpec((None, TOK_PER_BLK, N_HEADS, HEAD_DIM), lambda b, j, pt: (b, 0, 0, 0)),
                pl.BlockSpec((None, tq, LANES), lambda b, j, pt: (0, b, 0)),
                pl.BlockSpec((None, 1, LANES), lambda b, j, pt: (b, 0, 0)),
            ] + page_specs,
            out_specs=pl.BlockSpec((None, 2, n_rows, HEAD_DIM), lambda b, j, pt: (b, 0, 0, 0)),
            scratch_shapes=[
                vm((H_MOBA, tq, LANES), F32), vm((H_MOBA, tq, LANES), F32), vm((H_MOBA, tq, LANES), F32),
                vm((H_MOBA, tq, n_pages, HEAD_DIM), F32),
                vm((r_mb, 1), F32), vm((r_mb, 1), F32), vm((r_mb, HEAD_DIM), F32),
                vm((n_rows - r_mb, HEAD_DIM), F32), vm((r_sb1, HEAD_DIM), F32),
                vm((n_rows - r_mb, LANES), F32), vm((r_sb1, LANES), F32),
                vm((n_rows - r_sb1, 1), F32), vm((n_rows - r_sb1, 1), F32), vm((n_rows - r_sb1, HEAD_DIM), F32),
                vm((1, LANES), F32), vm((n_rows, 1), F32),
            ],
        ),
        out_shape=jax.ShapeDtypeStruct((nb, 2, n_rows, HEAD_DIM), F32),
        compiler_params=_cparams("parallel", "arbitrary"),
        name="decode_attn",
    )(page_table, qg, knew, vnew, logf_s, lfnb, *([cache_k, cache_v, cache_lfb] * DECODE_PAGES))
    return out.reshape(nb, 2, HPG, tq, HEAD_DIM).transpose(0, 3, 1, 2, 4).reshape(nb * tq, d)


def _merge_kernel(om_ref, os_ref, of_ref, gm_ref, gs_ref, gf_ref, x_ref, g1_ref,
                  wm_ref, ws_ref, wf_ref, wo_ref, o_ref, mrg_scr):
    j = pl.program_id(2)

    @pl.when(j == 0)
    def _():
        merged = (gm_ref[...] * _dot(om_ref[...].astype(BF16), wm_ref[...])
                  + gs_ref[...] * _dot(os_ref[...].astype(BF16), ws_ref[...])
                  + gf_ref[...] * _dot(of_ref[...].astype(BF16), wf_ref[...]))
        mrg_scr[...] = merged.astype(BF16)

    o_ref[...] = x_ref[...] + g1_ref[...] * _dot(mrg_scr[...], wo_ref[...])


def _merge(o_moba, o_sb, o_fox, proj, x, mod, w_m, w_s, w_f, w_o, *, tr):
    G, R, D = x.shape
    rs = mod.shape[1]
    tn = 1024
    per = D // tn
    rblk = 1 if rs == 1 else tr
    ridx = (lambda i: 0) if rs == 1 else (lambda i: i)
    full = lambda a: pl.BlockSpec(a.shape, lambda g_, i, j: (0, 0))
    return pl.pallas_call(
        _merge_kernel,
        grid=(G, R // tr, per),
        in_specs=[
            pl.BlockSpec((None, tr, o_moba.shape[-1]), lambda g_, i, j: (g_, i, 0)),
            pl.BlockSpec((None, tr, o_sb.shape[-1]), lambda g_, i, j: (g_, i, 0)),
            pl.BlockSpec((None, tr, o_fox.shape[-1]), lambda g_, i, j: (g_, i, 0)),
            pl.BlockSpec((None, None, tr, D), lambda g_, i, j: (1, g_, i, 0)),
            pl.BlockSpec((None, None, tr, D), lambda g_, i, j: (2, g_, i, 0)),
            pl.BlockSpec((None, None, tr, D), lambda g_, i, j: (3, g_, i, 0)),
            pl.BlockSpec((None, tr, tn), lambda g_, i, j: (g_, i, j)),
            pl.BlockSpec((None, rblk, tn), lambda g_, i, j: (g_, ridx(i), 2 * per + j)),
            full(w_m), full(w_s), full(w_f),
            pl.BlockSpec((D, tn), lambda g_, i, j: (0, j)),
        ],
        out_specs=pl.BlockSpec((None, tr, tn), lambda g_, i, j: (g_, i, j)),
        out_shape=jax.ShapeDtypeStruct((G, R, D), F32),
        scratch_shapes=[pltpu.VMEM((tr, D), BF16)],
        compiler_params=_cparams("parallel", "parallel", "arbitrary"),
        name="merge",
    )(o_moba, o_sb, o_fox, proj, proj, proj, x, mod, w_m, w_s, w_f, w_o)


def _norm_router_kernel(x_ref, sh_ref, sc_ref, g_ref, wr_ref, br_ref, h_ref, idx_ref, gate_ref):
    h = _rms_modulate(x_ref[...], g_ref[...], sc_ref[...], sh_ref[...])
    h_ref[...] = h
    tr = h.shape[0]
    lane = lax.broadcasted_iota(I32, (tr, LANES), 1)
    logits = _dot_acc(h, wr_ref[...]) + br_ref[...]
    vals, idxs = [], []
    for _ in range(TOP_K):
        m = jnp.max(logits, axis=-1, keepdims=True)
        idx = jnp.min(jnp.where(logits == m, lane, LANES), axis=-1, keepdims=True)
        vals.append(m)
        idxs.append(idx)
        logits = jnp.where(lane == idx, -jnp.inf, logits)
    es = [jnp.exp(v - vals[0]) for v in vals]
    denom = es[0] + es[1] + es[2] + es[3]
    idx_out = jnp.zeros((tr, LANES), I32)
    gate_out = jnp.zeros((tr, LANES), F32)
    for k in range(TOP_K):
        idx_out = jnp.where(lane == k, idxs[k], idx_out)
        gate_out = jnp.where(lane == k, es[k] / denom, gate_out)
    idx_ref[...] = idx_out
    gate_ref[...] = gate_out


def _norm_router(x, mod, g, w_r, b_r, *, tr):
    G, R, D = x.shape
    rs = mod.shape[1]
    rblk = 1 if rs == 1 else tr
    ridx = (lambda i: 0) if rs == 1 else (lambda i: i)
    return pl.pallas_call(
        _norm_router_kernel,
        grid=(G, R // tr),
        in_specs=[
            pl.BlockSpec((None, tr, D), lambda g_, i: (g_, i, 0)),
            pl.BlockSpec((None, rblk, D), lambda g_, i: (g_, ridx(i), 3)),
            pl.BlockSpec((None, rblk, D), lambda g_, i: (g_, ridx(i), 4)),
            pl.BlockSpec((1, D), lambda g_, i: (0, 0)),
            pl.BlockSpec((D, LANES), lambda g_, i: (0, 0)),
            pl.BlockSpec((1, LANES), lambda g_, i: (0, 0)),
        ],
        out_specs=[
            pl.BlockSpec((None, tr, D), lambda g_, i: (g_, i, 0)),
            pl.BlockSpec((None, tr, LANES), lambda g_, i: (g_, i, 0)),
            pl.BlockSpec((None, tr, LANES), lambda g_, i: (g_, i, 0)),
        ],
        out_shape=[
            jax.ShapeDtypeStruct((G, R, D), F32),
            jax.ShapeDtypeStruct((G, R, LANES), I32),
            jax.ShapeDtypeStruct((G, R, LANES), F32),
        ],
        compiler_params=_cparams("parallel", "parallel"),
        name="norm_router",
    )(x, mod, mod, g, w_r, b_r)


def _route(top_idx, tm):
    n_tok = top_idx.shape[0]
    n_assign = n_tok * TOP_K
    n_tiles = -(-(n_assign + N_EXPERTS * (tm - 1)) // tm)
    flat_e = top_idx.reshape(-1)
    onehot = (flat_e[:, None] == jnp.arange(N_EXPERTS, dtype=I32)[None, :]).astype(I32)
    sizes = jnp.sum(onehot, axis=0)
    rank = jnp.sum(jnp.cumsum(onehot, axis=0) * onehot, axis=1) - 1
    padded = (sizes + tm - 1) // tm * tm
    pad_end = jnp.cumsum(padded)
    pad_start = pad_end - padded
    dest = (pad_start[flat_e] + rank).astype(I32)
    slot_tok = jnp.zeros((n_tiles * tm,), I32).at[dest].set(jnp.arange(n_assign, dtype=I32) // TOP_K)
    tile_start = jnp.arange(n_tiles, dtype=I32) * tm
    tile_e = jnp.minimum(jnp.sum((pad_end[None, :] <= tile_start[:, None]).astype(I32), axis=1), N_EXPERTS - 1)
    n_used = (pad_end[-1] // tm).astype(I32)
    tile_i = jnp.arange(n_tiles, dtype=I32)
    prev_e = jnp.concatenate([jnp.full((1,), -1, I32), tile_e[:-1]])
    first = jnp.logical_and(tile_i < n_used, tile_e != prev_e)
    grp = jnp.cumsum(first.astype(I32)) - 1
    next_start = pad_end[tile_e] // tm
    next_e = jnp.where(next_start < n_used, tile_e[jnp.minimum(next_start, n_tiles - 1)], -1).astype(I32)
    meta = jnp.stack([n_used, jnp.sum(first.astype(I32))]).astype(I32)
    return dest, slot_tok, tile_e, next_e, grp.astype(I32), meta


def _gather_kernel(slot_tok_ref, n_used_ref, h_ref, o_ref, buf, sem, *, tm):
    t = pl.program_id(0)
    n_used = n_used_ref[0]

    def row_copy(tile, r):
        slot = tile % 2
        tok = slot_tok_ref[tile * tm + r]
        return pltpu.make_async_copy(h_ref.at[pl.ds(tok, 1)], buf.at[slot, pl.ds(r, 1)], sem.at[slot])

    def start_tile(tile):
        def start(r, c):
            row_copy(tile, r).start()
            return c
        lax.fori_loop(0, tm, start, 0, unroll=8)

    def wait_tile(tile):
        def wait(r, c):
            row_copy(tile, r).wait()
            return c
        lax.fori_loop(0, tm, wait, 0, unroll=8)

    @pl.when(jnp.logical_and(t == 0, n_used > 0))
    def _():
        start_tile(t)

    @pl.when(t + 1 < n_used)
    def _():
        start_tile(t + 1)

    @pl.when(t < n_used)
    def _():
        wait_tile(t)
        o_ref[...] = buf[t % 2].astype(o_ref.dtype)

    @pl.when(t >= n_used)
    def _():
        o_ref[...] = jnp.zeros(o_ref.shape, o_ref.dtype)


def _gather_rows(h, slot_tok, n_used, tm):
    n_slots = slot_tok.shape[0]
    d = h.shape[1]
    return pl.pallas_call(
        functools.partial(_gather_kernel, tm=tm),
        grid_spec=pltpu.PrefetchScalarGridSpec(
            num_scalar_prefetch=2,
            grid=(n_slots // tm,),
            in_specs=[pl.BlockSpec(memory_space=pl.ANY)],
            out_specs=pl.BlockSpec((tm, d), lambda t, st, nu: (t, 0)),
            scratch_shapes=[pltpu.VMEM((2, tm, d), h.dtype), pltpu.SemaphoreType.DMA((2,))],
        ),
        out_shape=jax.ShapeDtypeStruct((n_slots, d), BF16),
        compiler_params=_cparams("arbitrary"),
        name="gather_rows",
    )(slot_tok, n_used, h)


def _stream_expert_weights(te_ref, nx_ref, grp_ref, meta_ref, copies, cast):
    j = pl.program_id(0)
    i = pl.program_id(1)
    first = jnp.logical_and(i < meta_ref[0],
                            jnp.logical_or(i == 0, te_ref[i] != te_ref[jnp.maximum(i - 1, 0)]))

    @pl.when(first)
    def _():
        run = j * meta_ref[1] + grp_ref[i]
        slot = run % 2

        @pl.when(run == 0)
        def _():
            for c in copies(te_ref[i], j, slot):
                c.start()

        for c in copies(te_ref[i], j, slot):
            c.wait()
        cast(slot)
        nxt = nx_ref[i]

        @pl.when(nxt >= 0)
        def _():
            for c in copies(nxt, j, 1 - slot):
                c.start()

        @pl.when(jnp.logical_and(nxt < 0, j + 1 < pl.num_programs(0)))
        def _():
            for c in copies(te_ref[0], j + 1, 1 - slot):
                c.start()


def _expert_gu_kernel(te_ref, nx_ref, grp_ref, meta_ref, x_ref, w_ref, bg_ref, bu_ref, o_ref,
                      wbuf, wg_scr, wu_scr, sem, *, tn, nj):
    i = pl.program_id(1)

    def copies(e, col, slot):
        return [pltpu.make_async_copy(w_ref.at[e, :, pl.ds(pl.multiple_of((half * nj + col) * tn, tn), tn)],
                                      wbuf.at[slot, half], sem.at[slot]) for half in range(2)]

    def cast(slot):
        wg_scr[...] = wbuf[slot, 0].astype(BF16)
        wu_scr[...] = wbuf[slot, 1].astype(BF16)

    _stream_expert_weights(te_ref, nx_ref, grp_ref, meta_ref, copies, cast)

    @pl.when(i < meta_ref[0])
    def _():
        x = x_ref[...]
        g = _dot(x, wg_scr[...]) + bg_ref[...]
        u = _dot(x, wu_scr[...]) + bu_ref[...]
        g = jnp.minimum(g, SWIGLU_LIMIT)
        u = jnp.clip(u, -SWIGLU_LIMIT, SWIGLU_LIMIT)
        o_ref[...] = ((u + 1.0) * g * jax.nn.sigmoid(SWIGLU_ALPHA * g)).astype(BF16)

    @pl.when(i >= meta_ref[0])
    def _():
        o_ref[...] = jnp.zeros(o_ref.shape, o_ref.dtype)


def _expert_gu(xs, w_gu, b_gu, sched, tm):
    n_slots, d = xs.shape
    de = w_gu.shape[2] // 2
    tn = 1024
    nj = de // tn
    row = lambda j, i, te, nx, gr, mt: jnp.minimum(i, mt[0] - 1)
    exp = lambda j, i, te, nx, gr, mt: te[jnp.minimum(i, mt[0] - 1)]
    b3 = b_gu.reshape(w_gu.shape[0], 1, 2 * de)
    return pl.pallas_call(
        functools.partial(_expert_gu_kernel, tn=tn, nj=nj),
        grid_spec=pltpu.PrefetchScalarGridSpec(
            num_scalar_prefetch=4,
            grid=(nj, n_slots // tm),
            in_specs=[
                pl.BlockSpec((tm, d), lambda j, i, *s: (row(j, i, *s), 0)),
                pl.BlockSpec(memory_space=pl.ANY),
                pl.BlockSpec((None, 1, tn), lambda j, i, *s: (exp(j, i, *s), 0, j)),
                pl.BlockSpec((None, 1, tn), lambda j, i, *s: (exp(j, i, *s), 0, nj + j)),
            ],
            out_specs=pl.BlockSpec((tm, tn), lambda j, i, *s: (i, j)),
            scratch_shapes=[pltpu.VMEM((2, 2, d, tn), F32), pltpu.VMEM((d, tn), BF16), pltpu.VMEM((d, tn), BF16),
                            pltpu.SemaphoreType.DMA((2,))],
        ),
        out_shape=jax.ShapeDtypeStruct((n_slots, de), BF16),
        compiler_params=_cparams("arbitrary", "arbitrary"),
        name="expert_gu",
    )(*sched, xs, w_gu, b3, b3)


def _expert_down_kernel(te_ref, nx_ref, grp_ref, meta_ref, a_ref, w_ref, b_ref, o_ref, wbuf, w_scr, sem, *, tn):
    i = pl.program_id(1)

    def copies(e, col, slot):
        return [pltpu.make_async_copy(w_ref.at[e, :, pl.ds(pl.multiple_of(col * tn, tn), tn)],
                                      wbuf.at[slot], sem.at[slot])]

    def cast(slot):
        w_scr[...] = wbuf[slot].astype(BF16)

    _stream_expert_weights(te_ref, nx_ref, grp_ref, meta_ref, copies, cast)

    @pl.when(i < meta_ref[0])
    def _():
        o_ref[...] = _dot(a_ref[...], w_scr[...]) + b_ref[...]

    @pl.when(i >= meta_ref[0])
    def _():
        o_ref[...] = jnp.zeros(o_ref.shape, o_ref.dtype)


def _expert_down(act, w_down, b_down, sched, tm):
    n_slots, de = act.shape
    d = w_down.shape[2]
    tn = 1024
    nj = d // tn
    row = lambda j, i, te, nx, gr, mt: jnp.minimum(i, mt[0] - 1)
    exp = lambda j, i, te, nx, gr, mt: te[jnp.minimum(i, mt[0] - 1)]
    return pl.pallas_call(
        functools.partial(_expert_down_kernel, tn=tn),
        grid_spec=pltpu.PrefetchScalarGridSpec(
            num_scalar_prefetch=4,
            grid=(nj, n_slots // tm),
            in_specs=[
                pl.BlockSpec((tm, de), lambda j, i, *s: (row(j, i, *s), 0)),
                pl.BlockSpec(memory_space=pl.ANY),
                pl.BlockSpec((None, 1, tn), lambda j, i, *s: (exp(j, i, *s), 0, j)),
            ],
            out_specs=pl.BlockSpec((tm, tn), lambda j, i, *s: (i, j)),
            scratch_shapes=[pltpu.VMEM((2, de, tn), F32), pltpu.VMEM((de, tn), BF16),
                            pltpu.SemaphoreType.DMA((2,))],
        ),
        out_shape=jax.ShapeDtypeStruct((n_slots, d), F32),
        compiler_params=_cparams("arbitrary", "arbitrary"),
        name="expert_down",
    )(*sched, act, w_down, b_down.reshape(w_down.shape[0], 1, d))


def _combine_kernel(dest_ref, y_ref, gate_ref, x_ref, g2_ref, gfin_ref, o_ref, ybuf, sem, *, tt, tok0, final):
    s = pl.program_id(0) * pl.num_programs(1) + pl.program_id(1)
    n_steps = pl.num_programs(0) * pl.num_programs(1)

    def row_copy(tile, r, k):
        buf = tile % 2
        slot = dest_ref[(tok0 + tile * tt + r) * TOP_K + k]
        return pltpu.make_async_copy(y_ref.at[pl.ds(slot, 1)], ybuf.at[buf, k, pl.ds(r, 1)], sem.at[buf])

    def start_tile(tile):
        def start(r, c):
            for k in range(TOP_K):
                row_copy(tile, r, k).start()
            return c
        lax.fori_loop(0, tt, start, 0, unroll=2)

    def wait(r, c):
        for k in range(TOP_K):
            row_copy(s, r, k).wait()
        return c

    @pl.when(s == 0)
    def _():
        start_tile(s)

    @pl.when(s + 1 < n_steps)
    def _():
        start_tile(s + 1)

    lax.fori_loop(0, tt, wait, 0, unroll=2)
    gates = gate_ref[...]
    yb = ybuf.at[s % 2]
    moe = gates[:, 0:1] * yb[0]
    for k in range(1, TOP_K):
        moe = moe + gates[:, k:k + 1] * yb[k]
    x2 = x_ref[...] + g2_ref[...] * moe
    if final:
        x2 = x2 * lax.rsqrt(jnp.mean(x2 * x2, axis=-1, keepdims=True) + RMS_EPS) * gfin_ref[...]
    o_ref[...] = x2


def _combine(y, dest, gates, x1, mod, g_final, *, tt, tok0, final):
    G, R, D = x1.shape
    rs = mod.shape[1]
    rblk = 1 if rs == 1 else tt
    ridx = (lambda i: 0) if rs == 1 else (lambda i: i)
    kern = functools.partial(_combine_kernel, tt=tt, tok0=tok0, final=final)
    return pl.pallas_call(
        kern,
        grid_spec=pltpu.PrefetchScalarGridSpec(
            num_scalar_prefetch=1,
            grid=(G, R // tt),
            in_specs=[
                pl.BlockSpec(memory_space=pl.ANY),
                pl.BlockSpec((None, tt, LANES), lambda g_, i, d_: (g_, i, 0)),
                pl.BlockSpec((None, tt, D), lambda g_, i, d_: (g_, i, 0)),
                pl.BlockSpec((None, rblk, D), lambda g_, i, d_: (g_, ridx(i), 5)),
                pl.BlockSpec((1, D), lambda g_, i, d_: (0, 0)),
            ],
            out_specs=pl.BlockSpec((None, tt, D), lambda g_, i, d_: (g_, i, 0)),
            scratch_shapes=[pltpu.VMEM((2, TOP_K, tt, D), F32), pltpu.SemaphoreType.DMA((2,))],
        ),
        out_shape=jax.ShapeDtypeStruct((G, R, D), F32),
        compiler_params=_cparams("arbitrary", "arbitrary"),
        name="combine",
    )(dest, y, gates, x1, mod, g_final)


def _row_tile(rows, cap):
    return max(t for t in range(MOBA_BLOCK, cap + 1, MOBA_BLOCK) if rows % t == 0)


def _layer_weights(l, w_in, b_forget, w_br_moba, w_br_sb, w_br_fox, w_out, w_router, b_router):
    w_l = w_in[l]
    w_main = jnp.concatenate([w_l[:, :3 * D_ATTN], w_l[:, 3 * D_ATTN + H_FOX:]], axis=1).astype(BF16)
    w_f = jnp.pad(w_l[:, 3 * D_ATTN:3 * D_ATTN + H_FOX], ((0, 0), (0, LANES - H_FOX)))
    b_f = jnp.pad(b_forget[l], (0, LANES - H_FOX)).reshape(1, LANES)
    w_r = jnp.pad(w_router[l], ((0, 0), (0, LANES - N_EXPERTS)))
    b_r = jnp.pad(b_router[l], (0, LANES - N_EXPERTS), constant_values=NEG_INF).reshape(1, LANES)
    return dict(w_main=w_main, w_f=w_f, b_f=b_f, w_r=w_r, b_r=b_r,
                w_m=w_br_moba[l].astype(BF16), w_s=w_br_sb[l].astype(BF16), w_x=w_br_fox[l].astype(BF16),
                w_o=w_out[l].astype(BF16))


def kernel(x_prompt, x_sample, c_prompt, c_sample, cache_k, cache_v, cache_log_f, page_table, w_mod, b_mod,
           g_attn, w_in, b_forget, w_br_moba, w_br_sb, w_br_fox, w_out, g_ffn, w_router, b_router, w_gu, b_gu,
           w_down, b_down, g_final):
    depth = w_mod.shape[0]
    B, T, D = x_prompt.shape
    nb, tq, _ = x_sample.shape
    n_pool = cache_k.shape[1]
    n_prompt = B * T
    n_samp = nb * tq
    tm = EXPERT_TILE

    clf = jnp.pad(cache_log_f, ((0, 0), (0, 0), (0, 0), (HPG - H_FOX, 0))).reshape(
        depth, n_pool, PAGE // TOK_PER_BLK, LANES)
    c_all = jnp.pad(jnp.concatenate([c_prompt, c_sample], axis=0), ((0, 16 - B - nb), (0, 0)))
    wgu = w_gu.reshape(depth * N_EXPERTS, D, w_gu.shape[-1])
    bgu = b_gu.reshape(depth * N_EXPERTS, b_gu.shape[-1])
    wdn = w_down.reshape(depth * N_EXPERTS, w_down.shape[2], D)
    bdn = b_down.reshape(depth * N_EXPERTS, D)
    g_fin = g_final.reshape(1, D)

    xp = x_prompt
    xs = x_sample.reshape(1, n_samp, D)
    kp = jnp.zeros((depth, B, T, D), F32)
    vp = jnp.zeros((depth, B, T, D), F32)
    ks = jnp.zeros((depth, 1, n_samp, D), F32)
    vs = jnp.zeros((depth, 1, n_samp, D), F32)
    fp, fs = [], []
    for l in range(depth):
        lw = _layer_weights(l, w_in, b_forget, w_br_moba, w_br_sb, w_br_fox, w_out, w_router, b_router)
        g_a = g_attn[l].reshape(1, D)
        g_f = g_ffn[l].reshape(1, D)
        mod = _mod(c_all, w_mod, b_mod, l)
        mod_p = mod[:B].reshape(B, 1, 6 * D)
        mod_s = jnp.repeat(mod[B:B + nb], tq, axis=0).reshape(1, n_samp, 6 * D)

        qg_p, kp, vp, lf_p = _ln_proj(xp, mod_p, g_a, lw["w_main"], lw["w_f"], lw["b_f"], kp, vp, l,
                                      tr=_row_tile(T, 1024))
        fc, fr = _fcum(lf_p)
        x1p = _merge(_moba(qg_p, kp, vp, l), _sb(qg_p, kp, vp, l), _fox(qg_p, kp, vp, fc, fr, l), qg_p, xp, mod_p,
                     lw["w_m"], lw["w_s"], lw["w_x"], lw["w_o"], tr=256)
        qg_s, ks, vs, lf_s = _ln_proj(xs, mod_s, g_a, lw["w_main"], lw["w_f"], lw["b_f"], ks, vs, l, tr=n_samp)
        o_s = _decode_attn(qg_s[0, 0], ks[l, 0], vs[l, 0], lf_s, cache_k, cache_v, clf, page_table, l)
        o_s = o_s.reshape(1, n_samp, D)
        n_m, n_s = H_MOBA * HEAD_DIM, H_SB * HEAD_DIM
        x1s = _merge(o_s[..., :n_m], o_s[..., n_m:n_m + n_s], o_s[..., n_m + n_s:], qg_s, xs, mod_s,
                     lw["w_m"], lw["w_s"], lw["w_x"], lw["w_o"], tr=n_samp)

        h2p, idx_p, gate_p = _norm_router(x1p, mod_p, g_f, lw["w_r"], lw["b_r"], tr=512)
        h2s, idx_s, gate_s = _norm_router(x1s, mod_s, g_f, lw["w_r"], lw["b_r"], tr=n_samp)
        h2 = jnp.concatenate([h2p.reshape(n_prompt, D), h2s.reshape(n_samp, D)], axis=0)
        top_idx = jnp.concatenate([idx_p.reshape(n_prompt, LANES)[:, :TOP_K],
                                   idx_s.reshape(n_samp, LANES)[:, :TOP_K]], axis=0)
        dest, slot_tok, tile_e, next_e, grp, meta = _route(top_idx, tm)
        rows = _gather_rows(h2, slot_tok, meta, tm)
        sched = (tile_e + l * N_EXPERTS, jnp.where(next_e >= 0, next_e + l * N_EXPERTS, -1), grp, meta)
        act = _expert_gu(rows, wgu, bgu, sched, tm)
        y = _expert_down(act, wdn, bdn, sched, tm)
        final = l == depth - 1
        xp = _combine(y, dest, gate_p, x1p, mod_p, g_fin, tt=256, tok0=0, final=final)
        xs = _combine(y, dest, gate_s, x1s, mod_s, g_fin, tt=n_samp, tok0=n_prompt, final=final)

        fp.append(lf_p[..., :H_FOX])
        fs.append(lf_s[..., :H_FOX])

    hd = (N_HEADS, HEAD_DIM)
    return (xp, xs.reshape(nb, tq, D),
            kp.reshape(depth, B, T, *hd), vp.reshape(depth, B, T, *hd), jnp.stack(fp),
            ks.reshape(depth, nb, tq, *hd), vs.reshape(depth, nb, tq, *hd),
            jnp.stack(fs).reshape(depth, nb, tq, H_FOX))
```
